```python
import math
import jax, jax.numpy as jnp
from jax import lax
import numpy as np

D_MODEL = 1024
BATCH = 8
SEQ = 4096
DEPTH = 4

N_MIXERS = 3
PLE_DIM = 256
EPS = 1e-6
ATTN_HEADS = 16
ATTN_KV_HEADS = 4
ATTN_HEAD_DIM = 64
WINDOW = 128
ATTN_BLOCK = 128
CONV_WIDTH = 31
SSM_EXPAND = 2
SSM_D_INNER = SSM_EXPAND * D_MODEL
SSM_HEAD_DIM = 64
SSM_HEADS = SSM_D_INNER // SSM_HEAD_DIM
SSM_GROUPS = 8
SSM_STATE = 128
SSM_CONV = 4
SSM_CHUNK = 128
MLP_HIDDEN = 4 * D_MODEL

kernel_name = "hybrid_swa_conformer_ssd_trunk"


def rmsnorm(x, g):
    xf = x.astype(jnp.float32)
    y = xf * lax.rsqrt(jnp.mean(xf * xf, axis=-1, keepdims=True) + EPS)
    return (y * g.astype(jnp.float32)).astype(x.dtype)


def layernorm(x, g, b):
    xf = x.astype(jnp.float32)
    mu = jnp.mean(xf, axis=-1, keepdims=True)
    var = jnp.mean(jnp.square(xf - mu), axis=-1, keepdims=True)
    y = (xf - mu) * lax.rsqrt(var + EPS)
    return (y * g.astype(jnp.float32) + b.astype(jnp.float32)).astype(x.dtype)


def alibi_slopes(n_heads):
    return jnp.exp2(-8.0 * (jnp.arange(n_heads, dtype=jnp.float32) + 1.0) / n_heads)


def causal_dwconv(u, w, b):
    k, c = w.shape
    y = lax.conv_general_dilated(
        u, w[:, None, :].astype(u.dtype), window_strides=(1,), padding=[(k - 1, 0)],
        dimension_numbers=('NWC', 'WIO', 'NWC'), feature_group_count=c)
    return y + b


def sliding_window_attention(u, wqkv, gq, gk, sinks, wo):
    b, s, _ = u.shape
    H, KV, HD, BLK = ATTN_HEADS, ATTN_KV_HEADS, ATTN_HEAD_DIM, ATTN_BLOCK
    G = H // KV
    nb = s // BLK
    qkv = u @ wqkv
    q = qkv[..., :H * HD].reshape(b, s, KV, G, HD)
    k = qkv[..., H * HD:(H + KV) * HD].reshape(b, s, KV, HD)
    v = qkv[..., (H + KV) * HD:].reshape(b, s, KV, HD)
    q = rmsnorm(q, gq)
    k = rmsnorm(k, gk)
    qb = q.reshape(b, nb, BLK, KV, G, HD)

    def band(t):
        tb = t.reshape(b, nb, BLK, KV, HD)
        prev = jnp.pad(tb, ((0, 0), (1, 0), (0, 0), (0, 0), (0, 0)))[:, :-1]
        return jnp.concatenate([prev, tb], axis=2)

    kb, vb = band(k), band(v)
    scores = jnp.einsum('bnqkgd,bnskd->bnkgqs', qb, kb).astype(jnp.float32) * (HD ** -0.5)
    qi = jnp.arange(BLK)
    kj = jnp.arange(2 * BLK)
    dist = qi[:, None] - kj[None, :] + BLK
    kpos = jnp.arange(nb)[:, None] * BLK - BLK + kj[None, :]
    mask = ((dist >= 0) & (dist < WINDOW))[None] & (kpos >= 0)[:, None, :]
    slopes = alibi_slopes(H).reshape(KV, G)
    scores = scores - slopes[:, :, None, None] * dist.astype(jnp.float32)
    scores = jnp.where(mask[None, :, None, None], scores, -jnp.inf)
    sink = sinks.astype(jnp.float32).reshape(KV, G)[:, :, None]
    m = jnp.maximum(scores.max(axis=-1), sink)
    e = jnp.exp(scores - m[..., None])
    probs = e / (e.sum(axis=-1) + jnp.exp(sink - m))[..., None]
    o = jnp.einsum('bnkgqs,bnskd->bnqkgd', probs.astype(v.dtype), vb)
    return o.reshape(b, s, H * HD) @ wo


def conformer_conv_module(u, w_pw1, b_pw1, w_dw, b_dw, ln_g, ln_b, w_pw2, b_pw2):
    a = u @ w_pw1 + b_pw1
    val, gate = jnp.split(a, 2, axis=-1)
    y = val * jax.nn.sigmoid(gate)
    y = causal_dwconv(y, w_dw, b_dw)
    y = layernorm(y, ln_g, ln_b)
    y = jax.nn.silu(y)
    return y @ w_pw2 + b_pw2


def ssd_chunked(xs, dt, A, Bm, Cm):
    b, s, _ = xs.shape
    G, N, P, L = SSM_GROUPS, SSM_STATE, SSM_HEAD_DIM, SSM_CHUNK
    R = SSM_HEADS // G
    nc = s // L
    x = xs.reshape(b, nc, L, G, R, P)
    dtc = dt.reshape(b, nc, L, G, R)
    Bc = Bm.reshape(b, nc, L, G, N)
    Cc = Cm.reshape(b, nc, L, G, N)
    dA = dtc * A.reshape(G, R)
    cs = jnp.cumsum(dA, axis=2)
    x_dt = x * dtc[..., None].astype(x.dtype)
    causal = jnp.arange(L)[:, None] >= jnp.arange(L)[None, :]
    diff = cs[:, :, :, None] - cs[:, :, None, :]
    Lmat = jnp.exp(jnp.where(causal[:, :, None, None], diff, -jnp.inf))
    CB = jnp.einsum('bclgn,bcsgn->bclsg', Cc, Bc)
    y_diag = jnp.einsum('bclsg,bclsgr,bcsgrp->bclgrp', CB, Lmat, x_dt)
    decay_states = jnp.exp(cs[:, :, -1:] - cs)
    states = jnp.einsum('bcsgn,bcsgr,bcsgrp->bcgrpn', Bc, decay_states, x_dt).astype(jnp.float32)
    chunk_decay = jnp.exp(cs[:, :, -1])

    def step(h, inp):
        dec, st = inp
        return dec[..., None, None] * h + st, h

    h0 = jnp.zeros((b, G, R, P, N), jnp.float32)
    _, prev = lax.scan(step, h0, (jnp.moveaxis(chunk_decay, 1, 0), jnp.moveaxis(states, 1, 0)))
    prev = jnp.moveaxis(prev, 0, 1)
    y_off = jnp.einsum('bclgn,bcgrpn,bclgr->bclgrp', Cc, prev, jnp.exp(cs))
    return (y_diag + y_off).reshape(b, s, SSM_HEADS * P).astype(xs.dtype)


def mamba2_mixer(u, w_in, w_conv, b_conv, dt_bias, A_log, Dskip, norm_g, w_out):
    b, s, _ = u.shape
    DI, GN = SSM_D_INNER, SSM_GROUPS * SSM_STATE
    zxbcdt = u @ w_in
    z = zxbcdt[..., :DI]
    xbc = zxbcdt[..., DI:2 * DI + 2 * GN]
    dt_raw = zxbcdt[..., 2 * DI + 2 * GN:]
    xbc = jax.nn.silu(causal_dwconv(xbc, w_conv, b_conv))
    xs = xbc[..., :DI]
    Bm = xbc[..., DI:DI + GN]
    Cm = xbc[..., DI + GN:]
    dt = jax.nn.softplus(dt_raw.astype(jnp.float32) + dt_bias.astype(jnp.float32))
    A = -jnp.exp(A_log.astype(jnp.float32))
    y = ssd_chunked(xs, dt, A, Bm, Cm)
    y = y + (xs.reshape(b, s, SSM_HEADS, SSM_HEAD_DIM) * Dskip[:, None]).reshape(b, s, DI)
    y = rmsnorm(y * jax.nn.silu(z), norm_g)
    return y @ w_out


def sqrelu_mlp(u, w1, w2):
    return jnp.square(jax.nn.relu(u @ w1)) @ w2


def _fwd_setup_inputs(seed: int = 0) -> dict:
    key = jax.random.key(seed)
    ks = jax.random.split(key, 48)
    ctr = [0]

    def nk():
        ctr[0] += 1
        return ks[ctr[0] - 1]

    def w(shape, fan_in):
        return jax.random.normal(nk(), shape, jnp.float32) * fan_in ** -0.5

    def gain(shape):
        return 1.0 + 0.05 * jax.random.normal(nk(), shape, jnp.float32)

    def bias(shape):
        return 0.02 * jax.random.normal(nk(), shape, jnp.float32)

    n_a = len(range(0, DEPTH, N_MIXERS))
    n_b = len(range(1, DEPTH, N_MIXERS))
    n_c = len(range(2, DEPTH, N_MIXERS))
    D = D_MODEL
    qkv_w = (ATTN_HEADS + 2 * ATTN_KV_HEADS) * ATTN_HEAD_DIM
    in_w = 2 * SSM_D_INNER + 2 * SSM_GROUPS * SSM_STATE + SSM_HEADS
    conv_c = SSM_D_INNER + 2 * SSM_GROUPS * SSM_STATE

    x = jax.random.normal(nk(), (BATCH, SEQ, D), jnp.float32)
    p = jax.random.normal(nk(), (DEPTH, BATCH, SEQ, PLE_DIM), jnp.float32)
    dt0 = jnp.exp(jax.random.uniform(nk(), (n_c, SSM_HEADS), jnp.float32,
                                     minval=math.log(1e-3), maxval=math.log(1e-1)))
    c_dt_bias = dt0 + jnp.log(-jnp.expm1(-dt0))
    c_A_log = jnp.log(jax.random.uniform(nk(), (n_c, SSM_HEADS), jnp.float32, minval=1.0, maxval=16.0))
    return {
        "x": x,
        "p": p,
        "mix_norm_g": gain((DEPTH, D)),
        "mlp_norm_g": gain((DEPTH, D)),
        "ple_norm_g": gain((DEPTH, D)),
        "a_wqkv": w((n_a, D, qkv_w), D),
        "a_q_norm_g": gain((n_a, ATTN_HEAD_DIM)),
        "a_k_norm_g": gain((n_a, ATTN_HEAD_DIM)),
        "a_sinks": 0.5 * jax.random.normal(nk(), (n_a, ATTN_HEADS), jnp.float32),
        "a_wo": w((n_a, ATTN_HEADS * ATTN_HEAD_DIM, D), ATTN_HEADS * ATTN_HEAD_DIM),
        "b_w_pw1": w((n_b, D, 2 * D), D),
        "b_b_pw1": bias((n_b, 2 * D)),
        "b_w_dw": w((n_b, CONV_WIDTH, D), CONV_WIDTH),
        "b_b_dw": bias((n_b, D)),
        "b_ln_g": gain((n_b, D)),
        "b_ln_b": bias((n_b, D)),
        "b_w_pw2": w((n_b, D, D), D),
        "b_b_pw2": bias((n_b, D)),
        "c_w_in": w((n_c, D, in_w), D),
        "c_w_conv": w((n_c, SSM_CONV, conv_c), SSM_CONV),
        "c_b_conv": bias((n_c, conv_c)),
        "c_dt_bias": c_dt_bias,
        "c_A_log": c_A_log,
        "c_D": gain((n_c, SSM_HEADS)),
        "c_norm_g": gain((n_c, SSM_D_INNER)),
        "c_w_out": w((n_c, SSM_D_INNER, D), SSM_D_INNER),
        "m_w1": w((DEPTH, D, MLP_HIDDEN), D),
        "m_w2": w((DEPTH, MLP_HIDDEN, D), MLP_HIDDEN),
        "ple_w_proj": w((DEPTH, PLE_DIM, D), PLE_DIM),
        "ple_w_gate": w((DEPTH, D, D), D),
    }


def _fwd_reference(x, p, mix_norm_g, mlp_norm_g, ple_norm_g,
              a_wqkv, a_q_norm_g, a_k_norm_g, a_sinks, a_wo,
              b_w_pw1, b_b_pw1, b_w_dw, b_b_dw, b_ln_g, b_ln_b, b_w_pw2, b_b_pw2,
              c_w_in, c_w_conv, c_b_conv, c_dt_bias, c_A_log, c_D, c_norm_g, c_w_out,
              m_w1, m_w2, ple_w_proj, ple_w_gate):
    h = x
    for i in range(DEPTH):
        kind = i % N_MIXERS
        j = i // N_MIXERS
        u = rmsnorm(h, mix_norm_g[i])
        if kind == 0:
            mix = sliding_window_attention(u, a_wqkv[j], a_q_norm_g[j], a_k_norm_g[j],
                                           a_sinks[j], a_wo[j])
        elif kind == 1:
            mix = conformer_conv_module(u, b_w_pw1[j], b_b_pw1[j], b_w_dw[j], b_b_dw[j],
                                        b_ln_g[j], b_ln_b[j], b_w_pw2[j], b_b_pw2[j])
        else:
            mix = mamba2_mixer(u, c_w_in[j], c_w_conv[j], c_b_conv[j], c_dt_bias[j],
                               c_A_log[j], c_D[j], c_norm_g[j], c_w_out[j])
        h = h + mix
        h = h + sqrelu_mlp(rmsnorm(h, mlp_norm_g[i]), m_w1[i], m_w2[i])
        gate = jax.nn.sigmoid(rmsnorm(h, ple_norm_g[i]) @ ple_w_gate[i])
        h = h + gate * (p[i] @ ple_w_proj[i])
    return h


import jax as _jax
import jax.numpy as _jnp

TWIN_FORMAT = 'train_step'
FWD_PARAMS = ['x', 'p', 'mix_norm_g', 'mlp_norm_g', 'ple_norm_g', 'a_wqkv', 'a_q_norm_g', 'a_k_norm_g', 'a_sinks', 'a_wo', 'b_w_pw1', 'b_b_pw1', 'b_w_dw', 'b_b_dw', 'b_ln_g', 'b_ln_b', 'b_w_pw2', 'b_b_pw2', 'c_w_in', 'c_w_conv', 'c_b_conv', 'c_dt_bias', 'c_A_log', 'c_D', 'c_norm_g', 'c_w_out', 'm_w1', 'm_w2', 'ple_w_proj', 'ple_w_gate']
TWIN_WEIGHTS = ['mix_norm_g', 'mlp_norm_g', 'ple_norm_g', 'a_wqkv', 'a_q_norm_g', 'a_k_norm_g', 'a_sinks', 'a_wo', 'b_w_pw1', 'b_b_pw1', 'b_w_dw', 'b_b_dw', 'b_ln_g', 'b_ln_b', 'b_w_pw2', 'b_b_pw2', 'c_w_in', 'c_w_conv', 'c_b_conv', 'c_dt_bias', 'c_A_log', 'c_D', 'c_norm_g', 'c_w_out', 'm_w1', 'm_w2', 'ple_w_proj', 'ple_w_gate']
TWIN_DIFF_INPUT = 'x'
TWIN_INPUTS = ['x', 'p', 'mix_norm_g', 'mlp_norm_g', 'ple_norm_g', 'a_wqkv', 'a_q_norm_g', 'a_k_norm_g', 'a_sinks', 'a_wo', 'b_w_pw1', 'b_b_pw1', 'b_w_dw', 'b_b_dw', 'b_ln_g', 'b_ln_b', 'b_w_pw2', 'b_b_pw2', 'c_w_in', 'c_w_conv', 'c_b_conv', 'c_dt_bias', 'c_A_log', 'c_D', 'c_norm_g', 'c_w_out', 'm_w1', 'm_w2', 'ple_w_proj', 'ple_w_gate', 'loss_target', 'm_mix_norm_g', 'm_mlp_norm_g', 'm_ple_norm_g', 'm_a_wqkv', 'm_a_q_norm_g', 'm_a_k_norm_g', 'm_a_sinks', 'm_a_wo', 'm_b_w_pw1', 'm_b_b_pw1', 'm_b_w_dw', 'm_b_b_dw', 'm_b_ln_g', 'm_b_ln_b', 'm_b_w_pw2', 'm_b_b_pw2', 'm_c_w_in', 'm_c_w_conv', 'm_c_b_conv', 'm_c_dt_bias', 'm_c_A_log', 'm_c_D', 'm_c_norm_g', 'm_c_w_out', 'm_m_w1', 'm_m_w2', 'm_ple_w_proj', 'm_ple_w_gate', 'v_mix_norm_g', 'v_mlp_norm_g', 'v_ple_norm_g', 'v_a_wqkv', 'v_a_q_norm_g', 'v_a_k_norm_g', 'v_a_sinks', 'v_a_wo', 'v_b_w_pw1', 'v_b_b_pw1', 'v_b_w_dw', 'v_b_b_dw', 'v_b_ln_g', 'v_b_ln_b', 'v_b_w_pw2', 'v_b_b_pw2', 'v_c_w_in', 'v_c_w_conv', 'v_c_b_conv', 'v_c_dt_bias', 'v_c_A_log', 'v_c_D', 'v_c_norm_g', 'v_c_w_out', 'v_m_w1', 'v_m_w2', 'v_ple_w_proj', 'v_ple_w_gate']
TWIN_OUTPUTS = ['loss', 'grad_x', 'grad_mix_norm_g', 'grad_mlp_norm_g', 'grad_ple_norm_g', 'grad_a_wqkv', 'grad_a_q_norm_g', 'grad_a_k_norm_g', 'grad_a_sinks', 'grad_a_wo', 'grad_b_w_pw1', 'grad_b_b_pw1', 'grad_b_w_dw', 'grad_b_b_dw', 'grad_b_ln_g', 'grad_b_ln_b', 'grad_b_w_pw2', 'grad_b_b_pw2', 'grad_c_w_in', 'grad_c_w_conv', 'grad_c_b_conv', 'grad_c_dt_bias', 'grad_c_A_log', 'grad_c_D', 'grad_c_norm_g', 'grad_c_w_out', 'grad_m_w1', 'grad_m_w2', 'grad_ple_w_proj', 'grad_ple_w_gate', 'delta_mix_norm_g', 'delta_mlp_norm_g', 'delta_ple_norm_g', 'delta_a_wqkv', 'delta_a_q_norm_g', 'delta_a_k_norm_g', 'delta_a_sinks', 'delta_a_wo', 'delta_b_w_pw1', 'delta_b_b_pw1', 'delta_b_w_dw', 'delta_b_b_dw', 'delta_b_ln_g', 'delta_b_ln_b', 'delta_b_w_pw2', 'delta_b_b_pw2', 'delta_c_w_in', 'delta_c_w_conv', 'delta_c_b_conv', 'delta_c_dt_bias', 'delta_c_A_log', 'delta_c_D', 'delta_c_norm_g', 'delta_c_w_out', 'delta_m_w1', 'delta_m_w2', 'delta_ple_w_proj', 'delta_ple_w_gate', 'new_m_mix_norm_g', 'new_m_mlp_norm_g', 'new_m_ple_norm_g', 'new_m_a_wqkv', 'new_m_a_q_norm_g', 'new_m_a_k_norm_g', 'new_m_a_sinks', 'new_m_a_wo', 'new_m_b_w_pw1', 'new_m_b_b_pw1', 'new_m_b_w_dw', 'new_m_b_b_dw', 'new_m_b_ln_g', 'new_m_b_ln_b', 'new_m_b_w_pw2', 'new_m_b_b_pw2', 'new_m_c_w_in', 'new_m_c_w_conv', 'new_m_c_b_conv', 'new_m_c_dt_bias', 'new_m_c_A_log', 'new_m_c_D', 'new_m_c_norm_g', 'new_m_c_w_out', 'new_m_m_w1', 'new_m_m_w2', 'new_m_ple_w_proj', 'new_m_ple_w_gate', 'new_v_mix_norm_g', 'new_v_mlp_norm_g', 'new_v_ple_norm_g', 'new_v_a_wqkv', 'new_v_a_q_norm_g', 'new_v_a_k_norm_g', 'new_v_a_sinks', 'new_v_a_wo', 'new_v_b_w_pw1', 'new_v_b_b_pw1', 'new_v_b_w_dw', 'new_v_b_b_dw', 'new_v_b_ln_g', 'new_v_b_ln_b', 'new_v_b_w_pw2', 'new_v_b_b_pw2', 'new_v_c_w_in', 'new_v_c_w_conv', 'new_v_c_b_conv', 'new_v_c_dt_bias', 'new_v_c_A_log', 'new_v_c_D', 'new_v_c_norm_g', 'new_v_c_w_out', 'new_v_m_w1', 'new_v_m_w2', 'new_v_ple_w_proj', 'new_v_ple_w_gate']
TWIN_LEAF_KINDS = {'loss': 'loss', 'grad_x': 'grad_x', 'grad_mix_norm_g': 'grad_w', 'grad_mlp_norm_g': 'grad_w', 'grad_ple_norm_g': 'grad_w', 'grad_a_wqkv': 'grad_w', 'grad_a_q_norm_g': 'grad_w', 'grad_a_k_norm_g': 'grad_w', 'grad_a_sinks': 'grad_w', 'grad_a_wo': 'grad_w', 'grad_b_w_pw1': 'grad_w', 'grad_b_b_pw1': 'grad_w', 'grad_b_w_dw': 'grad_w', 'grad_b_b_dw': 'grad_w', 'grad_b_ln_g': 'grad_w', 'grad_b_ln_b': 'grad_w', 'grad_b_w_pw2': 'grad_w', 'grad_b_b_pw2': 'grad_w', 'grad_c_w_in': 'grad_w', 'grad_c_w_conv': 'grad_w', 'grad_c_b_conv': 'grad_w', 'grad_c_dt_bias': 'grad_w', 'grad_c_A_log': 'grad_w', 'grad_c_D': 'grad_w', 'grad_c_norm_g': 'grad_w', 'grad_c_w_out': 'grad_w', 'grad_m_w1': 'grad_w', 'grad_m_w2': 'grad_w', 'grad_ple_w_proj': 'grad_w', 'grad_ple_w_gate': 'grad_w', 'delta_mix_norm_g': 'delta_w', 'delta_mlp_norm_g': 'delta_w', 'delta_ple_norm_g': 'delta_w', 'delta_a_wqkv': 'delta_w', 'delta_a_q_norm_g': 'delta_w', 'delta_a_k_norm_g': 'delta_w', 'delta_a_sinks': 'delta_w', 'delta_a_wo': 'delta_w', 'delta_b_w_pw1': 'delta_w', 'delta_b_b_pw1': 'delta_w', 'delta_b_w_dw': 'delta_w', 'delta_b_b_dw': 'delta_w', 'delta_b_ln_g': 'delta_w', 'delta_b_ln_b': 'delta_w', 'delta_b_w_pw2': 'delta_w', 'delta_b_b_pw2': 'delta_w', 'delta_c_w_in': 'delta_w', 'delta_c_w_conv': 'delta_w', 'delta_c_b_conv': 'delta_w', 'delta_c_dt_bias': 'delta_w', 'delta_c_A_log': 'delta_w', 'delta_c_D': 'delta_w', 'delta_c_norm_g': 'delta_w', 'delta_c_w_out': 'delta_w', 'delta_m_w1': 'delta_w', 'delta_m_w2': 'delta_w', 'delta_ple_w_proj': 'delta_w', 'delta_ple_w_gate': 'delta_w', 'new_m_mix_norm_g': 'new_m', 'new_m_mlp_norm_g': 'new_m', 'new_m_ple_norm_g': 'new_m', 'new_m_a_wqkv': 'new_m', 'new_m_a_q_norm_g': 'new_m', 'new_m_a_k_norm_g': 'new_m', 'new_m_a_sinks': 'new_m', 'new_m_a_wo': 'new_m', 'new_m_b_w_pw1': 'new_m', 'new_m_b_b_pw1': 'new_m', 'new_m_b_w_dw': 'new_m', 'new_m_b_b_dw': 'new_m', 'new_m_b_ln_g': 'new_m', 'new_m_b_ln_b': 'new_m', 'new_m_b_w_pw2': 'new_m', 'new_m_b_b_pw2': 'new_m', 'new_m_c_w_in': 'new_m', 'new_m_c_w_conv': 'new_m', 'new_m_c_b_conv': 'new_m', 'new_m_c_dt_bias': 'new_m', 'new_m_c_A_log': 'new_m', 'new_m_c_D': 'new_m', 'new_m_c_norm_g': 'new_m', 'new_m_c_w_out': 'new_m', 'new_m_m_w1': 'new_m', 'new_m_m_w2': 'new_m', 'new_m_ple_w_proj': 'new_m', 'new_m_ple_w_gate': 'new_m', 'new_v_mix_norm_g': 'new_v', 'new_v_mlp_norm_g': 'new_v', 'new_v_ple_norm_g': 'new_v', 'new_v_a_wqkv': 'new_v', 'new_v_a_q_norm_g': 'new_v', 'new_v_a_k_norm_g': 'new_v', 'new_v_a_sinks': 'new_v', 'new_v_a_wo': 'new_v', 'new_v_b_w_pw1': 'new_v', 'new_v_b_b_pw1': 'new_v', 'new_v_b_w_dw': 'new_v', 'new_v_b_b_dw': 'new_v', 'new_v_b_ln_g': 'new_v', 'new_v_b_ln_b': 'new_v', 'new_v_b_w_pw2': 'new_v', 'new_v_b_b_pw2': 'new_v', 'new_v_c_w_in': 'new_v', 'new_v_c_w_conv': 'new_v', 'new_v_c_b_conv': 'new_v', 'new_v_c_dt_bias': 'new_v', 'new_v_c_A_log': 'new_v', 'new_v_c_D': 'new_v', 'new_v_c_norm_g': 'new_v', 'new_v_c_w_out': 'new_v', 'new_v_m_w1': 'new_v', 'new_v_m_w2': 'new_v', 'new_v_ple_w_proj': 'new_v', 'new_v_ple_w_gate': 'new_v'}


def _forward(args):
    return _fwd_reference(*[args[k] for k in FWD_PARAMS])


def _output_shape():
    def fwd():
        inp = _fwd_setup_inputs(0)
        return _fwd_reference(*[inp[k] for k in FWD_PARAMS])
    out = _jax.eval_shape(fwd)
    return out.shape, out.dtype

N_MICROBATCH = 1
ADAM_LR = 0.001
ADAM_B1 = 0.9
ADAM_B2 = 0.999
ADAM_EPS = 1e-08
ADAM_WD = 0.01
ADAM_STEP = 10
PER_EXAMPLE_BATCH_AXIS = {'x': 0, 'p': 1, 'loss_target': 0}
SHARED_INPUTS = []
_WEIGHT_DTYPES = {'mix_norm_g': _jnp.float32, 'mlp_norm_g': _jnp.float32, 'ple_norm_g': _jnp.float32, 'a_wqkv': _jnp.float32, 'a_q_norm_g': _jnp.float32, 'a_k_norm_g': _jnp.float32, 'a_sinks': _jnp.float32, 'a_wo': _jnp.float32, 'b_w_pw1': _jnp.float32, 'b_b_pw1': _jnp.float32, 'b_w_dw': _jnp.float32, 'b_b_dw': _jnp.float32, 'b_ln_g': _jnp.float32, 'b_ln_b': _jnp.float32, 'b_w_pw2': _jnp.float32, 'b_b_pw2': _jnp.float32, 'c_w_in': _jnp.float32, 'c_w_conv': _jnp.float32, 'c_b_conv': _jnp.float32, 'c_dt_bias': _jnp.float32, 'c_A_log': _jnp.float32, 'c_D': _jnp.float32, 'c_norm_g': _jnp.float32, 'c_w_out': _jnp.float32, 'm_w1': _jnp.float32, 'm_w2': _jnp.float32, 'ple_w_proj': _jnp.float32, 'ple_w_gate': _jnp.float32}
MOMENT_SCALE = {'mix_norm_g': 1.348067e+01, 'mlp_norm_g': 9.721396e+01, 'ple_norm_g': 1.490886e+00, 'a_wqkv': 8.647851e+00, 'a_q_norm_g': 1.730221e+01, 'a_k_norm_g': 1.741221e+01, 'a_sinks': 4.096872e+01, 'a_wo': 9.778105e+00, 'b_w_pw1': 6.604605e+00, 'b_b_pw1': 2.175380e+01, 'b_w_dw': 1.000606e+01, 'b_b_dw': 6.069563e+01, 'b_ln_g': 3.006673e+01, 'b_ln_b': 3.469408e+01, 'b_w_pw2': 1.609217e+01, 'b_b_pw2': 6.411117e+01, 'c_w_in': 6.964825e+00, 'c_w_conv': 7.640210e+00, 'c_b_conv': 1.462728e+01, 'c_dt_bias': 5.412805e+00, 'c_A_log': 3.781958e+01, 'c_D': 4.069862e+01, 'c_norm_g': 4.692403e+01, 'c_w_out': 1.592841e+01, 'm_w1': 7.702211e+00, 'm_w2': 2.929982e+01, 'ple_w_proj': 7.748134e-01, 'ple_w_gate': 1.110510e+00}


def _to_microbatches(a, axis):
    t = _jnp.moveaxis(a, axis, 0)
    t = t.reshape((N_MICROBATCH, t.shape[0] // N_MICROBATCH) + t.shape[1:])
    return _jnp.moveaxis(t, 1, axis + 1)


def setup_inputs(seed: int = 0) -> dict:
    inp = _fwd_setup_inputs(seed)
    key = _jax.random.fold_in(_jax.random.key(seed), 7919)
    shape, _ = _output_shape()
    out = dict(inp)
    out["loss_target"] = _jax.random.normal(_jax.random.fold_in(key, 0), shape, _jnp.float32)
    for i, name in enumerate(TWIN_WEIGHTS):
        w = inp[name].astype(_jnp.float32)
        if MOMENT_SCALE is None:
            s = _jnp.sqrt(_jnp.mean(_jnp.square(w)) + 1e-30)
        else:
            s = MOMENT_SCALE[name]
        km, kv = _jax.random.split(_jax.random.fold_in(key, i + 1))
        out[name] = w
        out["m_" + name] = s * _jax.random.normal(km, w.shape, _jnp.float32)
        out["v_" + name] = (s * s) * _jax.random.uniform(kv, w.shape, _jnp.float32, 0.5, 1.5)
    if N_MICROBATCH > 1:
        for name, axis in PER_EXAMPLE_BATCH_AXIS.items():
            out[name] = _to_microbatches(out[name], axis)
    return {'x': out['x'], 'p': out['p'], 'mix_norm_g': out['mix_norm_g'], 'mlp_norm_g': out['mlp_norm_g'], 'ple_norm_g': out['ple_norm_g'], 'a_wqkv': out['a_wqkv'], 'a_q_norm_g': out['a_q_norm_g'], 'a_k_norm_g': out['a_k_norm_g'], 'a_sinks': out['a_sinks'], 'a_wo': out['a_wo'], 'b_w_pw1': out['b_w_pw1'], 'b_b_pw1': out['b_b_pw1'], 'b_w_dw': out['b_w_dw'], 'b_b_dw': out['b_b_dw'], 'b_ln_g': out['b_ln_g'], 'b_ln_b': out['b_ln_b'], 'b_w_pw2': out['b_w_pw2'], 'b_b_pw2': out['b_b_pw2'], 'c_w_in': out['c_w_in'], 'c_w_conv': out['c_w_conv'], 'c_b_conv': out['c_b_conv'], 'c_dt_bias': out['c_dt_bias'], 'c_A_log': out['c_A_log'], 'c_D': out['c_D'], 'c_norm_g': out['c_norm_g'], 'c_w_out': out['c_w_out'], 'm_w1': out['m_w1'], 'm_w2': out['m_w2'], 'ple_w_proj': out['ple_w_proj'], 'ple_w_gate': out['ple_w_gate'], 'loss_target': out['loss_target'], 'm_mix_norm_g': out['m_mix_norm_g'], 'm_mlp_norm_g': out['m_mlp_norm_g'], 'm_ple_norm_g': out['m_ple_norm_g'], 'm_a_wqkv': out['m_a_wqkv'], 'm_a_q_norm_g': out['m_a_q_norm_g'], 'm_a_k_norm_g': out['m_a_k_norm_g'], 'm_a_sinks': out['m_a_sinks'], 'm_a_wo': out['m_a_wo'], 'm_b_w_pw1': out['m_b_w_pw1'], 'm_b_b_pw1': out['m_b_b_pw1'], 'm_b_w_dw': out['m_b_w_dw'], 'm_b_b_dw': out['m_b_b_dw'], 'm_b_ln_g': out['m_b_ln_g'], 'm_b_ln_b': out['m_b_ln_b'], 'm_b_w_pw2': out['m_b_w_pw2'], 'm_b_b_pw2': out['m_b_b_pw2'], 'm_c_w_in': out['m_c_w_in'], 'm_c_w_conv': out['m_c_w_conv'], 'm_c_b_conv': out['m_c_b_conv'], 'm_c_dt_bias': out['m_c_dt_bias'], 'm_c_A_log': out['m_c_A_log'], 'm_c_D': out['m_c_D'], 'm_c_norm_g': out['m_c_norm_g'], 'm_c_w_out': out['m_c_w_out'], 'm_m_w1': out['m_m_w1'], 'm_m_w2': out['m_m_w2'], 'm_ple_w_proj': out['m_ple_w_proj'], 'm_ple_w_gate': out['m_ple_w_gate'], 'v_mix_norm_g': out['v_mix_norm_g'], 'v_mlp_norm_g': out['v_mlp_norm_g'], 'v_ple_norm_g': out['v_ple_norm_g'], 'v_a_wqkv': out['v_a_wqkv'], 'v_a_q_norm_g': out['v_a_q_norm_g'], 'v_a_k_norm_g': out['v_a_k_norm_g'], 'v_a_sinks': out['v_a_sinks'], 'v_a_wo': out['v_a_wo'], 'v_b_w_pw1': out['v_b_w_pw1'], 'v_b_b_pw1': out['v_b_b_pw1'], 'v_b_w_dw': out['v_b_w_dw'], 'v_b_b_dw': out['v_b_b_dw'], 'v_b_ln_g': out['v_b_ln_g'], 'v_b_ln_b': out['v_b_ln_b'], 'v_b_w_pw2': out['v_b_w_pw2'], 'v_b_b_pw2': out['v_b_b_pw2'], 'v_c_w_in': out['v_c_w_in'], 'v_c_w_conv': out['v_c_w_conv'], 'v_c_b_conv': out['v_c_b_conv'], 'v_c_dt_bias': out['v_c_dt_bias'], 'v_c_A_log': out['v_c_A_log'], 'v_c_D': out['v_c_D'], 'v_c_norm_g': out['v_c_norm_g'], 'v_c_w_out': out['v_c_w_out'], 'v_m_w1': out['v_m_w1'], 'v_m_w2': out['v_m_w2'], 'v_ple_w_proj': out['v_ple_w_proj'], 'v_ple_w_gate': out['v_ple_w_gate']}


def _loss(weights, diff, rest, loss_target):
    with _jax.named_scope("forward"):
        args = {**rest, TWIN_DIFF_INPUT: diff, **{k: w.astype(_WEIGHT_DTYPES[k]) for k, w in weights.items()}}
        y = _forward(args)
    with _jax.named_scope("loss_head"):
        err = _jnp.square(y.astype(_jnp.float32) - loss_target)
        return 0.5 * _jnp.sum(_jnp.mean(err, axis=-1)) if err.ndim else 0.5 * err


def _adamw(w, g, m, v):
    m = ADAM_B1 * m + (1.0 - ADAM_B1) * g
    v = ADAM_B2 * v + (1.0 - ADAM_B2) * _jnp.square(g)
    m_hat = m / (1.0 - ADAM_B1 ** ADAM_STEP)
    v_hat = v / (1.0 - ADAM_B2 ** ADAM_STEP)
    delta = -ADAM_LR * (m_hat / (_jnp.sqrt(v_hat) + ADAM_EPS) + ADAM_WD * w)
    return delta, m, v


def reference(x, p, mix_norm_g, mlp_norm_g, ple_norm_g, a_wqkv, a_q_norm_g, a_k_norm_g, a_sinks, a_wo, b_w_pw1, b_b_pw1, b_w_dw, b_b_dw, b_ln_g, b_ln_b, b_w_pw2, b_b_pw2, c_w_in, c_w_conv, c_b_conv, c_dt_bias, c_A_log, c_D, c_norm_g, c_w_out, m_w1, m_w2, ple_w_proj, ple_w_gate, loss_target, m_mix_norm_g, m_mlp_norm_g, m_ple_norm_g, m_a_wqkv, m_a_q_norm_g, m_a_k_norm_g, m_a_sinks, m_a_wo, m_b_w_pw1, m_b_b_pw1, m_b_w_dw, m_b_b_dw, m_b_ln_g, m_b_ln_b, m_b_w_pw2, m_b_b_pw2, m_c_w_in, m_c_w_conv, m_c_b_conv, m_c_dt_bias, m_c_A_log, m_c_D, m_c_norm_g, m_c_w_out, m_m_w1, m_m_w2, m_ple_w_proj, m_ple_w_gate, v_mix_norm_g, v_mlp_norm_g, v_ple_norm_g, v_a_wqkv, v_a_q_norm_g, v_a_k_norm_g, v_a_sinks, v_a_wo, v_b_w_pw1, v_b_b_pw1, v_b_w_dw, v_b_b_dw, v_b_ln_g, v_b_ln_b, v_b_w_pw2, v_b_b_pw2, v_c_w_in, v_c_w_conv, v_c_b_conv, v_c_dt_bias, v_c_A_log, v_c_D, v_c_norm_g, v_c_w_out, v_m_w1, v_m_w2, v_ple_w_proj, v_ple_w_gate):
    given = dict(x=x, p=p, mix_norm_g=mix_norm_g, mlp_norm_g=mlp_norm_g, ple_norm_g=ple_norm_g, a_wqkv=a_wqkv, a_q_norm_g=a_q_norm_g, a_k_norm_g=a_k_norm_g, a_sinks=a_sinks, a_wo=a_wo, b_w_pw1=b_w_pw1, b_b_pw1=b_b_pw1, b_w_dw=b_w_dw, b_b_dw=b_b_dw, b_ln_g=b_ln_g, b_ln_b=b_ln_b, b_w_pw2=b_w_pw2, b_b_pw2=b_b_pw2, c_w_in=c_w_in, c_w_conv=c_w_conv, c_b_conv=c_b_conv, c_dt_bias=c_dt_bias, c_A_log=c_A_log, c_D=c_D, c_norm_g=c_norm_g, c_w_out=c_w_out, m_w1=m_w1, m_w2=m_w2, ple_w_proj=ple_w_proj, ple_w_gate=ple_w_gate, loss_target=loss_target, m_mix_norm_g=m_mix_norm_g, m_mlp_norm_g=m_mlp_norm_g, m_ple_norm_g=m_ple_norm_g, m_a_wqkv=m_a_wqkv, m_a_q_norm_g=m_a_q_norm_g, m_a_k_norm_g=m_a_k_norm_g, m_a_sinks=m_a_sinks, m_a_wo=m_a_wo, m_b_w_pw1=m_b_w_pw1, m_b_b_pw1=m_b_b_pw1, m_b_w_dw=m_b_w_dw, m_b_b_dw=m_b_b_dw, m_b_ln_g=m_b_ln_g, m_b_ln_b=m_b_ln_b, m_b_w_pw2=m_b_w_pw2, m_b_b_pw2=m_b_b_pw2, m_c_w_in=m_c_w_in, m_c_w_conv=m_c_w_conv, m_c_b_conv=m_c_b_conv, m_c_dt_bias=m_c_dt_bias, m_c_A_log=m_c_A_log, m_c_D=m_c_D, m_c_norm_g=m_c_norm_g, m_c_w_out=m_c_w_out, m_m_w1=m_m_w1, m_m_w2=m_m_w2, m_ple_w_proj=m_ple_w_proj, m_ple_w_gate=m_ple_w_gate, v_mix_norm_g=v_mix_norm_g, v_mlp_norm_g=v_mlp_norm_g, v_ple_norm_g=v_ple_norm_g, v_a_wqkv=v_a_wqkv, v_a_q_norm_g=v_a_q_norm_g, v_a_k_norm_g=v_a_k_norm_g, v_a_sinks=v_a_sinks, v_a_wo=v_a_wo, v_b_w_pw1=v_b_w_pw1, v_b_b_pw1=v_b_b_pw1, v_b_w_dw=v_b_w_dw, v_b_b_dw=v_b_b_dw, v_b_ln_g=v_b_ln_g, v_b_ln_b=v_b_ln_b, v_b_w_pw2=v_b_w_pw2, v_b_b_pw2=v_b_b_pw2, v_c_w_in=v_c_w_in, v_c_w_conv=v_c_w_conv, v_c_b_conv=v_c_b_conv, v_c_dt_bias=v_c_dt_bias, v_c_A_log=v_c_A_log, v_c_D=v_c_D, v_c_norm_g=v_c_norm_g, v_c_w_out=v_c_w_out, v_m_w1=v_m_w1, v_m_w2=v_m_w2, v_ple_w_proj=v_ple_w_proj, v_ple_w_gate=v_ple_w_gate)
    weights = {n: given[n] for n in TWIN_WEIGHTS}
    shared = {n: given[n] for n in SHARED_INPUTS}
    per_example = {n: given[n] for n in ['x', 'p']}
    grad_fn = _jax.value_and_grad(_loss, argnums=(0, 1))

    def one_microbatch(ex, loss_target):
        ex = dict(ex)
        diff = ex.pop(TWIN_DIFF_INPUT)
        return grad_fn(weights, diff, {**shared, **ex}, loss_target)

    if N_MICROBATCH == 1:
        loss, (grad_w, grad_x) = one_microbatch(per_example, given["loss_target"])
    else:
        def body(carry, xs):
            loss_sum, grad_sum = carry
            l_k, (gw_k, gx_k) = one_microbatch(xs[0], xs[1])
            with _jax.named_scope("update"):
                return (loss_sum + l_k, _jax.tree.map(_jnp.add, grad_sum, gw_k)), gx_k

        init = (_jnp.zeros((), _jnp.float32), _jax.tree.map(_jnp.zeros_like, weights))
        (loss, grad_w), grad_x = _jax.lax.scan(body, init, (per_example, given["loss_target"]))
    with _jax.named_scope("update"):
        delta_w, new_m, new_v = {}, {}, {}
        for n in TWIN_WEIGHTS:
            delta_w[n], new_m[n], new_v[n] = _adamw(weights[n], grad_w[n], given["m_" + n], given["v_" + n])
    return (loss, grad_x, *[grad_w[n] for n in TWIN_WEIGHTS], *[delta_w[n] for n in TWIN_WEIGHTS],
            *[new_m[n] for n in TWIN_WEIGHTS], *[new_v[n] for n in TWIN_WEIGHTS])
```

```python
import functools
import math

import jax
import jax.numpy as jnp
from jax import lax
from jax.experimental import pallas as pl
from jax.experimental.pallas import tpu as pltpu

F32 = jnp.float32
BF16 = jnp.bfloat16
SDS = jax.ShapeDtypeStruct

EPS = 1e-6
N_MIXERS = 3
HEAD_DIM = 64
Q_PER_KV = 4
ATTN_BLOCK = 128
SSM_P = 64
SSM_R = 4
SSM_N = 128
SSM_L = 128
ADAM_LR, ADAM_B1, ADAM_B2, ADAM_EPS, ADAM_WD, ADAM_STEP = 0.001, 0.9, 0.999, 1e-08, 0.01, 10
N_DEV = 8
PACK_COLS = 1024
PACK_ROW_ALIGN = 16
VMEM_LIMIT = 56 * 1024 * 1024
MM_VMEM_BUDGET = 40 * 1024 * 1024
ROW_VMEM_BUDGET = 24 * 1024 * 1024
CONV_PAD = 32
CONV_ROWS = 64
NEG_BIG = -1e30

BIG_SHARDED = (("a_wqkv", 2), ("a_wo", 1), ("b_w_pw1", 2), ("b_w_pw2", 1), ("c_w_in", 2), ("c_w_out", 1),
               ("m_w1", 2), ("m_w2", 1), ("ple_w_proj", 2), ("ple_w_gate", 1))
SMALL_SHARDED = (("b_w_dw", 2), ("c_w_conv", 2), ("c_b_conv", 1), ("c_norm_g", 1))
REPLICATED = ("mix_norm_g", "mlp_norm_g", "ple_norm_g", "a_q_norm_g", "a_k_norm_g", "a_sinks", "b_b_pw1", "b_b_dw",
              "b_ln_g", "b_ln_b", "b_b_pw2", "c_dt_bias", "c_A_log", "c_D")
WEIGHTS = ("mix_norm_g", "mlp_norm_g", "ple_norm_g", "a_wqkv", "a_q_norm_g", "a_k_norm_g", "a_sinks", "a_wo",
           "b_w_pw1", "b_b_pw1", "b_w_dw", "b_b_dw", "b_ln_g", "b_ln_b", "b_w_pw2", "b_b_pw2", "c_w_in", "c_w_conv",
           "c_b_conv", "c_dt_bias", "c_A_log", "c_D", "c_norm_g", "c_w_out", "m_w1", "m_w2", "ple_w_proj",
           "ple_w_gate")


def _pcall(body, **kw):
    return pl.pallas_call(body, **kw)


def _params(*sem):
    return pltpu.CompilerParams(dimension_semantics=sem, vmem_limit_bytes=VMEM_LIMIT)


def _sigmoid(x):
    return 1.0 / (1.0 + jnp.exp(-x))


def _dsilu(x):
    s = _sigmoid(x)
    return s * (1.0 + x * (1.0 - s))


def _colsum(x):
    return jnp.sum(x, axis=0, keepdims=True)


def _rowmean(x):
    return jnp.mean(x, axis=-1, keepdims=True)


def _rowwise(fn, rows, pars, row_outs, acc_outs=(), *, name, tm=None):
    rows = [r if isinstance(r, tuple) else (r, r.shape[1], 0) for r in rows]
    t = rows[0][0].shape[0]
    per_row = sum(w * a.dtype.itemsize for a, w, _ in rows) + sum(w * jnp.dtype(d).itemsize for w, d in row_outs)
    if tm is None:
        tm = 1024
        while tm > 8 and (t % tm or 2 * tm * per_row > ROW_VMEM_BUDGET):
            tm //= 2
    assert t % tm == 0, (name, t, tm)
    n_in, n_row = len(rows) + len(pars), len(row_outs)

    def body(*refs):
        outs = fn(*[r[...] for r in refs[:n_in]])
        outs = outs if isinstance(outs, tuple) else (outs,)
        for ref, o in zip(refs[n_in:n_in + n_row], outs[:n_row]):
            ref[...] = o.astype(ref.dtype)
        acc_refs = refs[n_in + n_row:]
        if acc_refs:
            @pl.when(pl.program_id(0) == 0)
            def _():
                for ref in acc_refs:
                    ref[...] = jnp.zeros_like(ref)
            for ref, o in zip(acc_refs, outs[n_row:]):
                ref[...] += o

    in_specs = [pl.BlockSpec((tm, w), lambda i, cb=cb: (i, cb)) for _, w, cb in rows]
    in_specs += [pl.BlockSpec(p.shape, lambda i: (0, 0)) for p in pars]
    out_specs = [pl.BlockSpec((tm, w), lambda i: (i, 0)) for w, _ in row_outs]
    out_specs += [pl.BlockSpec((1, w), lambda i: (0, 0)) for w in acc_outs]
    out_shape = [SDS((t, w), d) for w, d in row_outs] + [SDS((1, w), F32) for w in acc_outs]
    return _pcall(body, grid=(t // tm,), in_specs=in_specs, out_specs=out_specs, out_shape=out_shape, name=name,
                  compiler_params=_params("arbitrary"))(*[a for a, _, _ in rows], *pars)


_DOT_DIMS = {"nn": (((1,), (0,)), ((), ())), "nt": (((1,), (1,)), ((), ())), "tn": (((0,), (0,)), ((), ()))}


def _mm(a, b, form, out_dtypes, *, name, epi=None, rows=(), pars=()):
    if form == "tn":
        k, m = a.shape
        n = b.shape[1]
    else:
        m, k = a.shape
        n = b.shape[0] if form == "nt" else b.shape[1]
    rows = [r if isinstance(r, tuple) else (r, 0) for r in rows]
    tn = next((c for c in (512, 256, 128) if n % c == 0), n)

    def need(tm_):
        blk = tm_ * k * a.dtype.itemsize + tn * k * b.dtype.itemsize
        blk += sum(tm_ * tn * jnp.dtype(d).itemsize for d in out_dtypes) + sum(tm_ * tn * r.dtype.itemsize for r, _ in rows)
        return 2 * blk

    tm = next((c for c in (1024, 512, 256, 128) if m % c == 0 and need(c) <= MM_VMEM_BUDGET), None)
    assert tm is not None, (name, m, k, n)
    nx = len(rows) + len(pars)

    def body(*refs):
        acc = lax.dot_general(refs[0][...].astype(BF16), refs[1][...].astype(BF16), _DOT_DIMS[form],
                              preferred_element_type=F32)
        outs = epi(acc, *[r[...] for r in refs[2:2 + nx]]) if epi else acc
        outs = outs if isinstance(outs, tuple) else (outs,)
        for ref, o in zip(refs[2 + nx:], outs):
            ref[...] = o.astype(ref.dtype)

    a_spec = pl.BlockSpec((k, tm), lambda i, j: (0, i)) if form == "tn" else pl.BlockSpec((tm, k), lambda i, j: (i, 0))
    b_spec = pl.BlockSpec((tn, k), lambda i, j: (j, 0)) if form == "nt" else pl.BlockSpec((k, tn), lambda i, j: (0, j))
    in_specs = [a_spec, b_spec]
    in_specs += [pl.BlockSpec((tm, tn), lambda i, j, off=off: (i, j + off)) for _, off in rows]
    in_specs += [pl.BlockSpec((1, tn), lambda i, j: (0, j)) for _ in pars]
    out_specs = [pl.BlockSpec((tm, tn), lambda i, j: (i, j)) for _ in out_dtypes]
    out_shape = [SDS((m, n), d) for d in out_dtypes]
    res = _pcall(body, grid=(m // tm, n // tn), in_specs=in_specs, out_specs=out_specs, out_shape=out_shape, name=name,
                 compiler_params=_params("arbitrary", "arbitrary"))(a, b, *[r for r, _ in rows], *pars)
    return res[0] if len(out_dtypes) == 1 else res


def _rms_fwd(x, g, name):
    def fn(x, g):
        return x * lax.rsqrt(_rowmean(x * x) + EPS) * g
    return _rowwise(fn, [x], [g], [(x.shape[1], BF16)], name=name)[0]


def _rms_bwd(dy, x, g, dres, name):
    def fn(dy, x, dres, g):
        rstd = lax.rsqrt(_rowmean(x * x) + EPS)
        xh = x * rstd
        dxh = dy * g
        tot = dres + rstd * (dxh - xh * _rowmean(dxh * xh))
        return tot, tot, _colsum(dy * xh), _colsum(tot)
    d = x.shape[1]
    return _rowwise(fn, [dy, x, dres], [g], [(d, F32), (d, BF16)], [d, d], name=name)


def _ple_fwd(h, gl, pp, name):
    return _rowwise(lambda h, gl, pp: h + _sigmoid(gl) * pp, [h, gl, pp], [], [(h.shape[1], F32)], name=name)[0]


def _ple_bwd(dh, gl, pp, name):
    def fn(dh, gl, pp):
        gate = _sigmoid(gl)
        return dh * pp * gate * (1.0 - gate), dh * gate
    d = dh.shape[1]
    return _rowwise(fn, [dh, gl, pp], [], [(d, BF16), (d, BF16)], name=name)


def _loss_and_grad(h, tgt, name):
    d = h.shape[1]

    def fn(h, tgt):
        err = h - tgt
        return err * (1.0 / d), _colsum(err * err)
    return _rowwise(fn, [h, tgt], [], [(d, F32)], [d], name=name)


def _add(a, b, name):
    return _rowwise(lambda a, b: a + b, [a, b], [], [(a.shape[1], F32)], name=name)[0]


def _adamw(w, g, m, v, name):
    def fn(w, g, m, v):
        m = ADAM_B1 * m + (1.0 - ADAM_B1) * g
        v = ADAM_B2 * v + (1.0 - ADAM_B2) * (g * g)
        m_hat = m / (1.0 - ADAM_B1 ** ADAM_STEP)
        v_hat = v / (1.0 - ADAM_B2 ** ADAM_STEP)
        return -ADAM_LR * (m_hat / (jnp.sqrt(v_hat) + ADAM_EPS) + ADAM_WD * w), m, v
    c = w.shape[1]
    return _rowwise(fn, [w, g, m, v], [], [(c, F32)] * 3, name=name)


def _sum_blocks(x, name):
    n = x.shape[0]

    def body(x_ref, o_ref):
        acc = x_ref[0]
        for j in range(1, n):
            acc = acc + x_ref[j]
        o_ref[...] = acc
    return _pcall(body, out_shape=SDS(x.shape[1:], x.dtype), name=name, compiler_params=_params())(x)


def _shifted(chunk, off, rows):
    if off % 8 == 0:
        return chunk[off:off + rows]
    return pltpu.roll(chunk, chunk.shape[0] - off, 0)[:rows]


def _conv_cols(c):
    return 256 if c % 256 == 0 else 128


def _dwconv_fwd(y, w, b, silu, name):
    t, c = y.shape
    taps = w.shape[0]
    cb, tb, pad = _conv_cols(c), CONV_ROWS, CONV_PAD
    ypad = jnp.pad(y, ((pad, 0), (0, 0)))

    def body(y_ref, w_ref, b_ref, *out_refs):
        def step(i, carry):
            t0 = pl.multiple_of(i * tb, tb)
            chunk = y_ref[pl.ds(t0, tb + pad), :]
            acc = jnp.broadcast_to(b_ref[...], (tb, cb))
            for k in range(taps):
                acc = acc + _shifted(chunk, pad - (taps - 1) + k, tb) * w_ref[k:k + 1, :]
            out_refs[0][pl.ds(t0, tb), :] = acc
            if silu:
                out_refs[1][pl.ds(t0, tb), :] = acc * _sigmoid(acc)
            return carry
        lax.fori_loop(0, t // tb, step, 0)

    n_out = 2 if silu else 1
    return _pcall(body, grid=(c // cb,),
                  in_specs=[pl.BlockSpec((t + pad, cb), lambda j: (0, j)), pl.BlockSpec((taps, cb), lambda j: (0, j)),
                            pl.BlockSpec((1, cb), lambda j: (0, j))],
                  out_specs=[pl.BlockSpec((t, cb), lambda j: (0, j))] * n_out,
                  out_shape=[SDS((t, c), F32)] * n_out, name=name, compiler_params=_params("arbitrary"))(ypad, w, b)


def _dwconv_bwd(y, dout, w, name):
    t, c = y.shape
    taps = w.shape[0]
    cb, tb, pad = _conv_cols(c), CONV_ROWS, CONV_PAD
    taps_pad = -(-taps // 8) * 8
    ypad = jnp.pad(y, ((pad, 0), (0, 0)))
    dpad = jnp.pad(dout, ((0, pad), (0, 0)))

    def body(y_ref, d_ref, w_ref, dy_ref, dw_ref, db_ref):
        dw_ref[...] = jnp.zeros_like(dw_ref)

        def step(i, db):
            t0 = pl.multiple_of(i * tb, tb)
            ychunk = y_ref[pl.ds(t0, tb + pad), :]
            dchunk = d_ref[pl.ds(t0, tb + pad), :]
            d0 = dchunk[:tb]
            acc = jnp.zeros((tb, cb), F32)
            for k in range(taps):
                acc = acc + _shifted(dchunk, taps - 1 - k, tb) * w_ref[k:k + 1, :]
                dw_ref[k:k + 1, :] += _colsum(d0 * _shifted(ychunk, pad - (taps - 1) + k, tb))
            dy_ref[pl.ds(t0, tb), :] = acc
            return db + _colsum(d0)
        db_ref[...] = lax.fori_loop(0, t // tb, step, jnp.zeros((1, cb), F32))

    dy, dw, db = _pcall(
        body, grid=(c // cb,),
        in_specs=[pl.BlockSpec((t + pad, cb), lambda j: (0, j)), pl.BlockSpec((t + pad, cb), lambda j: (0, j)),
                  pl.BlockSpec((taps, cb), lambda j: (0, j))],
        out_specs=[pl.BlockSpec((t, cb), lambda j: (0, j)), pl.BlockSpec((taps_pad, cb), lambda j: (0, j)),
                   pl.BlockSpec((1, cb), lambda j: (0, j))],
        out_shape=[SDS((t, c), F32), SDS((taps_pad, c), F32), SDS((1, c), F32)], name=name,
        compiler_params=_params("arbitrary"))(ypad, dpad, w)
    return dy, dw[:taps], db


def _head_norm(x, g):
    rstd = lax.rsqrt(_rowmean(x * x) + EPS)
    xh = x * rstd
    return xh * g, xh, rstd


def _head_norm_bwd(dy, xh, rstd, g):
    dxh = dy * g
    return rstd * (dxh - xh * _rowmean(dxh * xh)), _colsum(dy * xh)


def _attn_probs(q_ref, kp_ref, kc_ref, gq_ref, gk_ref, sinks_ref, n_heads):
    g, blk, hd = Q_PER_KV, ATTN_BLOCK, HEAD_DIM
    kv, n = pl.program_id(0), pl.program_id(1)
    q_raw = q_ref[...].reshape(g * blk, hd)
    k_raw = jnp.concatenate([kp_ref[...], kc_ref[...]], axis=0)
    qn, qh, q_rstd = _head_norm(q_raw, gq_ref[...])
    kn, kh, k_rstd = _head_norm(k_raw, gk_ref[...])
    s = lax.dot_general(qn.astype(BF16), kn.astype(BF16), _DOT_DIMS["nt"], preferred_element_type=F32) * hd ** -0.5
    row = lax.broadcasted_iota(jnp.int32, (g * blk, 1), 0)
    g_row = lax.shift_right_logical(row, 7)
    qi = lax.bitwise_and(row, blk - 1)
    kj = lax.broadcasted_iota(jnp.int32, (1, 2 * blk), 1)
    dist = qi - kj + blk
    valid = (dist >= 0) & (dist < blk) & (kj >= jnp.where(n > 0, 0, blk))
    head = (kv * g + g_row + 1).astype(F32)
    slope = jnp.exp(head * (-8.0 * math.log(2.0) / n_heads))
    s = jnp.where(valid, s - slope * dist.astype(F32), NEG_BIG)
    sink = jnp.zeros((g * blk, 1), F32)
    for gi in range(g):
        sink = jnp.where(g_row == gi, sinks_ref[kv * g + gi], sink)
    m = jnp.maximum(jnp.max(s, axis=1, keepdims=True), sink)
    e = jnp.exp(s - m)
    e_sink = jnp.exp(sink - m)
    inv = 1.0 / (jnp.sum(e, axis=1, keepdims=True) + e_sink)
    return (q_raw, qn, qh, q_rstd), (k_raw, kn, kh, k_rstd), e * inv, e_sink * inv


def _attn_specs(t):
    g, blk, hd = Q_PER_KV, ATTN_BLOCK, HEAD_DIM
    q_spec = pl.BlockSpec((None, g, blk, hd), lambda a, n: (a, 0, n, 0))
    prev_spec = pl.BlockSpec((None, blk, hd), lambda a, n: (a, jnp.maximum(n - 1, 0), 0))
    cur_spec = pl.BlockSpec((None, blk, hd), lambda a, n: (a, n, 0))
    gain_spec = pl.BlockSpec((1, hd), lambda a, n: (0, 0))
    sink_spec = pl.BlockSpec(memory_space=pltpu.SMEM)
    return q_spec, prev_spec, cur_spec, gain_spec, sink_spec


def _attn_fwd(q, k, v, gq, gk, sinks, name):
    n_kv, g, t, hd = q.shape
    blk = ATTN_BLOCK
    n_heads = n_kv * g

    def body(q_ref, kp_ref, kc_ref, vp_ref, vc_ref, gq_ref, gk_ref, sinks_ref, o_ref):
        _, _, p, _ = _attn_probs(q_ref, kp_ref, kc_ref, gq_ref, gk_ref, sinks_ref, n_heads)
        vb = jnp.concatenate([vp_ref[...], vc_ref[...]], axis=0).astype(BF16)
        o = jnp.dot(p.astype(BF16), vb, preferred_element_type=F32)
        o_ref[...] = o.reshape(g, blk, hd).astype(o_ref.dtype)

    q_spec, prev_spec, cur_spec, gain_spec, sink_spec = _attn_specs(t)
    return _pcall(body, grid=(n_kv, t // blk),
                  in_specs=[q_spec, prev_spec, cur_spec, prev_spec, cur_spec, gain_spec, gain_spec, sink_spec],
                  out_specs=q_spec, out_shape=SDS(q.shape, BF16), name=name,
                  compiler_params=_params("arbitrary", "arbitrary"))(q, k, k, v, v, gq, gk, sinks)


def _attn_bwd(q, k, v, do, gq, gk, sinks, name):
    n_kv, g, t, hd = q.shape
    blk = ATTN_BLOCK
    n_heads = n_kv * g

    def body(q_ref, kp_ref, kc_ref, vp_ref, vc_ref, do_ref, gq_ref, gk_ref, sinks_ref,
             dq_ref, dkp_ref, dkc_ref, dvp_ref, dvc_ref, dgq_ref, dgk_ref, dsink_ref):
        kv, n = pl.program_id(0), pl.program_id(1)
        (_, qn, qh, q_rstd), (_, kn, kh, k_rstd), p, p_sink = _attn_probs(
            q_ref, kp_ref, kc_ref, gq_ref, gk_ref, sinks_ref, n_heads)
        vb = jnp.concatenate([vp_ref[...], vc_ref[...]], axis=0).astype(BF16)
        dob = do_ref[...].reshape(g * blk, hd).astype(BF16)
        dp = lax.dot_general(dob, vb, _DOT_DIMS["nt"], preferred_element_type=F32)
        delta = jnp.sum(p * dp, axis=1, keepdims=True)
        ds = (p * (dp - delta) * hd ** -0.5).astype(BF16)
        dv = lax.dot_general(p.astype(BF16), dob, _DOT_DIMS["tn"], preferred_element_type=F32)
        dqn = jnp.dot(ds, kn.astype(BF16), preferred_element_type=F32)
        dkn = lax.dot_general(ds, qn.astype(BF16), _DOT_DIMS["tn"], preferred_element_type=F32)
        dq, dgq = _head_norm_bwd(dqn, qh, q_rstd, gq_ref[...])
        dk, dgk = _head_norm_bwd(dkn, kh, k_rstd, gk_ref[...])
        dq_ref[...] = dq.reshape(g, blk, hd)
        dkp_ref[...] = dk[:blk]
        dkc_ref[...] = dk[blk:]
        dvp_ref[...] = dv[:blk]
        dvc_ref[...] = dv[blk:]

        @pl.when((kv == 0) & (n == 0))
        def _():
            dgq_ref[...] = jnp.zeros_like(dgq_ref)
            dgk_ref[...] = jnp.zeros_like(dgk_ref)

        @pl.when(n == 0)
        def _():
            dsink_ref[...] = jnp.zeros_like(dsink_ref)
        dgq_ref[...] += dgq
        dgk_ref[...] += dgk
        dsink = jnp.sum((-p_sink * delta).reshape(g, blk, 1), axis=1)
        dsink_ref[...] += jnp.broadcast_to(dsink, (g, 128))

    q_spec, prev_spec, cur_spec, gain_spec, sink_spec = _attn_specs(t)
    kv_shape = SDS(k.shape, F32)
    return _pcall(
        body, grid=(n_kv, t // blk),
        in_specs=[q_spec, prev_spec, cur_spec, prev_spec, cur_spec, q_spec, gain_spec, gain_spec, sink_spec],
        out_specs=[q_spec, cur_spec, cur_spec, cur_spec, cur_spec, gain_spec, gain_spec,
                   pl.BlockSpec((None, g, 128), lambda a, n: (a, 0, 0))],
        out_shape=[SDS(q.shape, F32), kv_shape, kv_shape, kv_shape, kv_shape, SDS((1, hd), F32), SDS((1, hd), F32),
                   SDS((n_kv, g, 128), F32)],
        name=name, compiler_params=_params("arbitrary", "arbitrary"))(q, k, k, v, v, do, gq, gk, sinks)


def _col_head(width):
    return lax.shift_right_logical(lax.broadcasted_iota(jnp.int32, (1, width), 1), 6)


def _expand_heads(v, col_head):
    out = jnp.zeros((v.shape[0], col_head.shape[1]), F32)
    for r in range(SSM_R):
        out = jnp.where(col_head == r, v[:, r:r + 1], out)
    return out


def _head_sums(m, col_head):
    lane = lax.broadcasted_iota(jnp.int32, (1, SSM_R), 1)
    out = jnp.zeros((m.shape[0], SSM_R), F32)
    for r in range(SSM_R):
        out = jnp.where(lane == r, jnp.sum(jnp.where(col_head == r, m, 0.0), axis=1, keepdims=True), out)
    return out


def _ssd_specs(d_inner, n_groups, n_chunks, rev):
    l, w, n = SSM_L, SSM_R * SSM_P, SSM_N

    def cc(c):
        return n_chunks - 1 - c if rev else c
    x_spec = pl.BlockSpec((l, w), lambda g, c: (cc(c), g))
    b_spec = pl.BlockSpec((l, n), lambda g, c: (cc(c), d_inner // n + g))
    c_spec = pl.BlockSpec((l, n), lambda g, c: (cc(c), d_inner // n + n_groups + g))
    col_spec = pl.BlockSpec((None, l, SSM_R), lambda g, c: (g, cc(c), 0))
    row_spec = pl.BlockSpec((None, None, SSM_R, l), lambda g, c: (g, cc(c), 0, 0))
    state_spec = pl.BlockSpec((None, None, n, w), lambda g, c: (g, cc(c), 0, 0))
    return x_spec, b_spec, c_spec, col_spec, row_spec, state_spec


def _decay_matrix(a, a_row, r):
    l = SSM_L
    causal = lax.broadcasted_iota(jnp.int32, (l, l), 0) >= lax.broadcasted_iota(jnp.int32, (l, l), 1)
    return jnp.exp(jnp.where(causal, a[:, r:r + 1] - a_row[r:r + 1, :], NEG_BIG))


def _ssd_fwd(xbc, dt_g, cs_g, cs_row, d_inner, name):
    t = xbc.shape[0]
    l, w, n = SSM_L, SSM_R * SSM_P, SSM_N
    n_groups, n_chunks = d_inner // w, t // l

    def body(x_ref, b_ref, c_ref, dt_ref, a_ref, ar_ref, y_ref, prev_ref, h_scr):
        @pl.when(pl.program_id(1) == 0)
        def _():
            h_scr[...] = jnp.zeros_like(h_scr)
        col_head = _col_head(w)
        a, a_row = a_ref[...], ar_ref[...]
        h = h_scr[...]
        prev_ref[...] = h
        xdt = x_ref[...] * _expand_heads(dt_ref[...], col_head)
        xb, bb, cb = xdt.astype(BF16), b_ref[...].astype(BF16), c_ref[...].astype(BF16)
        a_last = a[l - 1:l, :]
        cbt = lax.dot_general(cb, bb, _DOT_DIMS["nt"], preferred_element_type=F32)
        y = jnp.zeros((l, w), F32)
        for r in range(SSM_R):
            m = (cbt * _decay_matrix(a, a_row, r)).astype(BF16)
            y = jnp.where(col_head == r, jnp.dot(m, xb, preferred_element_type=F32), y)
        y_off = jnp.dot(cb, h.astype(BF16), preferred_element_type=F32)
        y_ref[...] = y + _expand_heads(jnp.exp(a), col_head) * y_off
        fx = (xdt * _expand_heads(jnp.exp(a_last - a), col_head)).astype(BF16)
        states = lax.dot_general(bb, fx, _DOT_DIMS["tn"], preferred_element_type=F32)
        h_scr[...] = _expand_heads(jnp.exp(a_last), col_head) * h + states

    x_spec, b_spec, c_spec, col_spec, row_spec, state_spec = _ssd_specs(d_inner, n_groups, n_chunks, False)
    return _pcall(body, grid=(n_groups, n_chunks),
                  in_specs=[x_spec, b_spec, c_spec, col_spec, col_spec, row_spec],
                  out_specs=[x_spec, state_spec],
                  out_shape=[SDS((t, d_inner), F32), SDS((n_groups, n_chunks, n, w), F32)],
                  scratch_shapes=[pltpu.VMEM((n, w), F32)], name=name,
                  compiler_params=_params("arbitrary", "arbitrary"))(xbc, xbc, xbc, dt_g, cs_g, cs_row)


def _ssd_bwd(xbc, dt_g, cs_g, cs_row, prev, dy, d_inner, name):
    t = xbc.shape[0]
    l, w, n = SSM_L, SSM_R * SSM_P, SSM_N
    n_groups, n_chunks = d_inner // w, t // l

    def body(x_ref, b_ref, c_ref, dt_ref, a_ref, ar_ref, prev_ref, dy_ref,
             dx_ref, db_ref, dc_ref, ddtx_ref, dda_ref, dh_scr):
        @pl.when(pl.program_id(1) == 0)
        def _():
            dh_scr[...] = jnp.zeros_like(dh_scr)
        col_head = _col_head(w)
        a, a_row = a_ref[...], ar_ref[...]
        x, dyv = x_ref[...], dy_ref[...]
        h, dhn = prev_ref[...], dh_scr[...]
        dtx = _expand_heads(dt_ref[...], col_head)
        xdt = x * dtx
        a_last = a[l - 1:l, :]
        e_last = jnp.exp(a_last)
        xb, bb, cb = xdt.astype(BF16), b_ref[...].astype(BF16), c_ref[...].astype(BF16)
        hb, dhnb, dyb = h.astype(BF16), dhn.astype(BF16), dyv.astype(BF16)
        edy = (_expand_heads(jnp.exp(a), col_head) * dyv).astype(BF16)
        dc = lax.dot_general(edy, hb, _DOT_DIMS["nt"], preferred_element_type=F32)
        dh_scr[...] = (_expand_heads(e_last, col_head) * dhn
                       + lax.dot_general(cb, edy, _DOT_DIMS["tn"], preferred_element_type=F32))
        fx = xdt * _expand_heads(jnp.exp(a_last - a), col_head)
        g1 = jnp.dot(bb, dhnb, preferred_element_type=F32)
        db = lax.dot_general(fx.astype(BF16), dhnb, _DOT_DIMS["nt"], preferred_element_type=F32)
        q = _head_sums(fx * g1, col_head)
        d_e_last = _head_sums(_colsum(dhn * h), col_head)
        cbt = lax.dot_general(cb, bb, _DOT_DIMS["nt"], preferred_element_type=F32)
        dcbt = jnp.zeros((l, l), F32)
        dxd = jnp.zeros((l, w), F32)
        dda_diag = jnp.zeros((l, SSM_R), F32)
        rows_l = lax.broadcasted_iota(jnp.int32, (l, l), 0)
        cols_l = lax.broadcasted_iota(jnp.int32, (l, l), 1)
        upper = (cols_l >= rows_l).astype(F32)
        head_lane = lax.broadcasted_iota(jnp.int32, (1, SSM_R), 1)
        for r in range(SSM_R):
            decay = _decay_matrix(a, a_row, r)
            dyr = jnp.where(col_head == r, dyv, 0.0).astype(BF16)
            dm = lax.dot_general(dyr, xb, _DOT_DIMS["nt"], preferred_element_type=F32)
            dcbt = dcbt + dm * decay
            m = cbt * decay
            dxr = lax.dot_general(m.astype(BF16), dyb, _DOT_DIMS["tn"], preferred_element_type=F32)
            dxd = jnp.where(col_head == r, dxr, dxd)
            below = jnp.dot(upper, dm * m, precision=lax.Precision.HIGHEST, preferred_element_type=F32)
            dda_r = jnp.sum(jnp.where(cols_l < rows_l, below, 0.0), axis=1, keepdims=True)
            dda_diag = jnp.where(head_lane == r, dda_r, dda_diag)
        dcbb = dcbt.astype(BF16)
        dc_ref[...] = dc + jnp.dot(dcbb, bb, preferred_element_type=F32)
        db_ref[...] = db + lax.dot_general(dcbb, cb, _DOT_DIMS["tn"], preferred_element_type=F32)
        dxt = dxd + _expand_heads(jnp.exp(a_last - a), col_head) * g1
        dx_ref[...] = dxt * dtx
        ddtx_ref[...] = _head_sums(dxt * x, col_head)
        y_off = _expand_heads(jnp.exp(a), col_head) * jnp.dot(cb, hb, preferred_element_type=F32)
        row = lax.broadcasted_iota(jnp.int32, (l, 1), 0)
        da = _head_sums(dyv * y_off, col_head) + jnp.where(row == l - 1, e_last * d_e_last, 0.0)
        strict_lower = (cols_l < rows_l).astype(F32)
        dda_ref[...] = (dda_diag + jnp.dot(upper, da, precision=lax.Precision.HIGHEST, preferred_element_type=F32)
                        + jnp.dot(strict_lower, q, precision=lax.Precision.HIGHEST, preferred_element_type=F32))

    x_spec, b_spec, c_spec, col_spec, row_spec, state_spec = _ssd_specs(d_inner, n_groups, n_chunks, True)
    bc_out = pl.BlockSpec((l, n), lambda g, c: (n_chunks - 1 - c, g))
    return _pcall(body, grid=(n_groups, n_chunks),
                  in_specs=[x_spec, b_spec, c_spec, col_spec, col_spec, row_spec, state_spec, x_spec],
                  out_specs=[x_spec, bc_out, bc_out, col_spec, col_spec],
                  out_shape=[SDS((t, d_inner), F32), SDS((t, n_groups * n), F32), SDS((t, n_groups * n), F32),
                             SDS(dt_g.shape, F32), SDS(dt_g.shape, F32)],
                  scratch_shapes=[pltpu.VMEM((n, w), F32)], name=name,
                  compiler_params=_params("arbitrary", "arbitrary"))(xbc, xbc, xbc, dt_g, cs_g, cs_row, prev, dy)


def _softplus(x):
    return jnp.maximum(x, 0.0) + jnp.log1p(jnp.exp(-jnp.abs(x)))


def _dt_fwd(dt_raw, bias, a_log, name):
    l = SSM_L

    def fn(raw, bias, a_log):
        dt = _softplus(raw + bias)
        lower = (lax.broadcasted_iota(jnp.int32, (l, l), 0) >= lax.broadcasted_iota(jnp.int32, (l, l), 1)).astype(F32)
        cs = jnp.dot(lower, dt * -jnp.exp(a_log), precision=lax.Precision.HIGHEST, preferred_element_type=F32)
        return dt, cs
    wd = dt_raw.shape[1]
    return _rowwise(fn, [dt_raw], [bias, a_log], [(wd, F32), (wd, F32)], name=name, tm=l)


def _dt_bwd(ddtx, dda, dt_raw, dt, bias, a_log, name):
    def fn(ddtx, dda, raw, dt, bias, a_log):
        a = -jnp.exp(a_log)
        draw = (ddtx + dda * a) * _sigmoid(raw + bias)
        return draw, _colsum(draw), _colsum(dda * dt) * a
    wd = dt_raw.shape[1]
    return _rowwise(fn, [ddtx, dda, dt_raw, dt], [bias, a_log], [(wd, BF16)], [wd, wd], name=name)


MESH_IDS = pl.DeviceIdType.MESH
ANY_SPEC = pl.BlockSpec(memory_space=pl.ANY)


def _mesh_pos():
    return lax.axis_index("x"), lax.axis_index("y"), lax.axis_index("c")


def _all_gather(x, name):
    def body(x_ref, out_ref, send_sems, recv_sems, local_sem):
        mx, my, mc = _mesh_pos()
        me, sibling = (mx, my, mc), (mx, my, 1 - mc)
        chips = [(1 - mx, my), (mx, 1 - my), (1 - mx, 1 - my)]

        def slot(px, py, pc):
            return out_ref.at[4 * px + 2 * py + pc]

        def copy(k, block, to, src=None):
            return pltpu.make_async_remote_copy(
                src_ref=slot(*block) if src is None else src, dst_ref=slot(*block), send_sem=send_sems.at[k],
                recv_sem=recv_sems.at[k], device_id=to, device_id_type=MESH_IDS)

        mine = pltpu.make_async_copy(x_ref, slot(*me), local_sem)
        mine.start()
        first = [copy(0, me, sibling, src=x_ref)]
        first += [copy(1 + j, me, (*chip, mc), src=x_ref) for j, chip in enumerate(chips)]
        for cp in first:
            cp.start()
        passed = [copy(4 + j, (*chip, mc), sibling) for j, chip in enumerate(chips)]
        for j, chip in enumerate(chips):
            copy(1 + j, (*chip, mc), me).wait_recv()
            passed[j].start()
        copy(0, sibling, me).wait_recv()
        for j, chip in enumerate(chips):
            copy(4 + j, (*chip, 1 - mc), me).wait_recv()
        for cp in first + passed:
            cp.wait_send()
        mine.wait()

    return _pcall(body, out_shape=SDS((N_DEV,) + x.shape, x.dtype), in_specs=[ANY_SPEC], out_specs=ANY_SPEC,
                  scratch_shapes=[pltpu.SemaphoreType.DMA((7,)), pltpu.SemaphoreType.DMA((7,)),
                                  pltpu.SemaphoreType.DMA], name=name)(x)


def _swap_with_sibling(g, name):
    def body(g_ref, out_ref, send_sems, recv_sems):
        mx, my, mc = _mesh_pos()
        copies = []
        for k in range(4):
            copies.append(pltpu.make_async_remote_copy(
                src_ref=g_ref.at[2 * k + (1 - mc)], dst_ref=out_ref.at[k], send_sem=send_sems.at[k],
                recv_sem=recv_sems.at[k], device_id=(mx, my, 1 - mc), device_id_type=MESH_IDS))
        for cp in copies:
            cp.start()
        for cp in copies:
            cp.wait()

    return _pcall(body, out_shape=SDS((4,) + g.shape[1:], g.dtype), in_specs=[ANY_SPEC], out_specs=ANY_SPEC,
                  scratch_shapes=[pltpu.SemaphoreType.DMA((4,)), pltpu.SemaphoreType.DMA((4,))], name=name)(g)


def _swap_with_chips(part, name):
    def body(p_ref, out_ref, send_sems, recv_sems):
        mx, my, mc = _mesh_pos()
        chips = [(1 - mx, my), (mx, 1 - my), (1 - mx, 1 - my)]
        copies = []
        for j, (px, py) in enumerate(chips):
            copies.append(pltpu.make_async_remote_copy(
                src_ref=p_ref.at[2 * px + py], dst_ref=out_ref.at[j], send_sem=send_sems.at[j],
                recv_sem=recv_sems.at[j], device_id=(px, py, mc), device_id_type=MESH_IDS))
        for cp in copies:
            cp.start()
        for cp in copies:
            cp.wait()

    return _pcall(body, out_shape=SDS((3,) + part.shape[1:], part.dtype), in_specs=[ANY_SPEC], out_specs=ANY_SPEC,
                  scratch_shapes=[pltpu.SemaphoreType.DMA((3,)), pltpu.SemaphoreType.DMA((3,))], name=name)(part)


def _row_tile(rows, cols, n_arrays):
    tr = 512
    while tr > 8 and (rows % tr or 2 * n_arrays * tr * cols * 4 > ROW_VMEM_BUDGET):
        tr //= 2
    assert rows % tr == 0
    return tr


def _add_core_blocks(g, recv, core, name):
    _, r, c = g.shape
    tr = _row_tile(r, c, 3)

    def body(core_ref, g_ref, r_ref, o_ref):
        o_ref[...] = g_ref[...] + r_ref[...]

    grid_spec = pltpu.PrefetchScalarGridSpec(
        num_scalar_prefetch=1, grid=(4, r // tr),
        in_specs=[pl.BlockSpec((None, None, tr, c), lambda k, i, core_ref: (k, core_ref[0], i, 0)),
                  pl.BlockSpec((None, tr, c), lambda k, i, core_ref: (k, i, 0))],
        out_specs=pl.BlockSpec((None, tr, c), lambda k, i, core_ref: (k, i, 0)))
    return _pcall(body, grid_spec=grid_spec, out_shape=SDS((4, r, c), F32), name=name,
                  compiler_params=_params("arbitrary", "arbitrary"))(core, g.reshape(4, 2, r, c), recv)


def _add_chip_blocks(part, recv, chip, name):
    _, r, c = part.shape
    tr = _row_tile(r, c, 5)

    def body(chip_ref, p_ref, r_ref, o_ref):
        o_ref[...] = ((p_ref[...] + r_ref[0]) + r_ref[1]) + r_ref[2]

    grid_spec = pltpu.PrefetchScalarGridSpec(
        num_scalar_prefetch=1, grid=(r // tr,),
        in_specs=[pl.BlockSpec((None, tr, c), lambda i, chip_ref: (chip_ref[0], i, 0)),
                  pl.BlockSpec((3, tr, c), lambda i, chip_ref: (0, i, 0))],
        out_specs=pl.BlockSpec((tr, c), lambda i, chip_ref: (i, 0)))
    return _pcall(body, grid_spec=grid_spec, out_shape=SDS((r, c), F32), name=name,
                  compiler_params=_params("arbitrary"))(chip, part, recv)


def _reduce_scatter(g):
    core = lax.axis_index("c").astype(jnp.int32).reshape(1)
    chip = (2 * lax.axis_index("x") + lax.axis_index("y")).astype(jnp.int32).reshape(1)
    part = _add_core_blocks(g, _swap_with_sibling(g, "rs_swap_sibling"), core, "rs_add_core")
    return _add_chip_blocks(part, _swap_with_chips(part, "rs_swap_chips"), chip, "rs_add_chips")


def _slab_rows(n):
    return -(-n // (PACK_COLS * PACK_ROW_ALIGN)) * PACK_ROW_ALIGN


def _to_slab(a, lead=()):
    flat = a.reshape(lead + (-1,))
    rows = _slab_rows(flat.shape[-1])
    flat = jnp.pad(flat, [(0, 0)] * len(lead) + [(0, rows * PACK_COLS - flat.shape[-1])])
    return flat.reshape(lead + (rows, PACK_COLS))


def _pack(arrays, lead=()):
    return jnp.concatenate([_to_slab(a, lead) for a in arrays], axis=len(lead))


def _unpack(slab, shapes, lead=()):
    out, row = [], 0
    for shape in shapes:
        n = math.prod(shape)
        rows = _slab_rows(n)
        part = lax.slice_in_dim(slab, row, row + rows, axis=len(lead)).reshape(lead + (rows * PACK_COLS,))
        out.append(lax.slice_in_dim(part, 0, n, axis=len(lead)).reshape(lead + tuple(shape)))
        row += rows
    return out


def _join_shards(t, axis):
    t = jnp.moveaxis(t, 0, axis)
    return t.reshape(t.shape[:axis] + (t.shape[axis] * t.shape[axis + 1],) + t.shape[axis + 2:])


def _split_shards(full, axis):
    s = full.shape
    t = full.reshape(s[:axis] + (N_DEV, s[axis] // N_DEV) + s[axis + 1:])
    return jnp.moveaxis(t, axis, 0)


def _attention_fwd(u, h, w, j, tag):
    t, d = u.shape
    n_heads = d // HEAD_DIM
    n_kv = n_heads // Q_PER_KV
    qkv = _mm(u, w["a_wqkv"][j], "nn", [F32], name=tag + "_qkv")
    q = qkv[:, :d].reshape(t, n_kv, Q_PER_KV, HEAD_DIM).transpose(1, 2, 0, 3)
    k = qkv[:, d:d + n_kv * HEAD_DIM].reshape(t, n_kv, HEAD_DIM).transpose(1, 0, 2)
    v = qkv[:, d + n_kv * HEAD_DIM:].reshape(t, n_kv, HEAD_DIM).transpose(1, 0, 2)
    gq, gk, sinks = w["a_q_norm_g"][j][None], w["a_k_norm_g"][j][None], w["a_sinks"][j]
    o = _attn_fwd(q, k, v, gq, gk, sinks, tag + "_attn")
    o2 = o.transpose(2, 0, 1, 3).reshape(t, d)
    h1 = _mm(o2, w["a_wo"][j], "nn", [F32], name=tag + "_wo", epi=lambda acc, h: acc + h, rows=[h])
    return h1, (q, k, v, o2)


def _attention_bwd(ctx, u, dh1, dh1b, w, j, tag, grads):
    q, k, v, o2 = ctx
    t, d = u.shape
    n_kv = q.shape[0]
    gq, gk, sinks = w["a_q_norm_g"][j][None], w["a_k_norm_g"][j][None], w["a_sinks"][j]
    grads["a_wo"][j] = _mm(o2, dh1b, "tn", [F32], name=tag + "_dwo")
    do2 = _mm(dh1b, w["a_wo"][j], "nt", [F32], name=tag + "_do")
    do = do2.reshape(t, n_kv, Q_PER_KV, HEAD_DIM).transpose(1, 2, 0, 3)
    dq, dkp, dkc, dvp, dvc, dgq, dgk, dsink = _attn_bwd(q, k, v, do, gq, gk, sinks, tag + "_attn_bwd")

    def fold(prev_part, cur_part):
        shifted = jnp.concatenate([prev_part[:, ATTN_BLOCK:], jnp.zeros_like(prev_part[:, :ATTN_BLOCK])], axis=1)
        both = _add(shifted.reshape(n_kv * t, HEAD_DIM), cur_part.reshape(n_kv * t, HEAD_DIM), tag + "_fold")
        return both.reshape(n_kv, t, HEAD_DIM).transpose(1, 0, 2).reshape(t, n_kv * HEAD_DIM)
    dqkv = jnp.concatenate([dq.transpose(2, 0, 1, 3).reshape(t, d), fold(dkp, dkc), fold(dvp, dvc)], axis=1).astype(BF16)
    grads["a_q_norm_g"][j] = dgq[0]
    grads["a_k_norm_g"][j] = dgk[0]
    grads["a_sinks"][j] = dsink[:, :, 0].reshape(-1)
    grads["a_wqkv"][j] = _mm(u, dqkv, "tn", [F32], name=tag + "_dwqkv")
    return _mm(dqkv, w["a_wqkv"][j], "nt", [F32], name=tag + "_du")


def _conformer_fwd(u, h, w, j, tag):
    t, d = u.shape
    a = _mm(u, w["b_w_pw1"][j], "nn", [F32], name=tag + "_pw1", epi=lambda acc, b: acc + b, pars=[w["b_b_pw1"][j][None]])
    y = _rowwise(lambda val, gate: val * _sigmoid(gate), [(a, d, 0), (a, d, 1)], [], [(d, F32)], name=tag + "_glu")[0]
    y2 = _dwconv_fwd(y, w["b_w_dw"][j], w["b_b_dw"][j][None], False, tag + "_dw")[0]

    def ln_silu(y2, g, b):
        mu = _rowmean(y2)
        yc = y2 - mu
        y3 = yc * lax.rsqrt(_rowmean(yc * yc) + EPS) * g + b
        return y3 * _sigmoid(y3)
    y4 = _rowwise(ln_silu, [y2], [w["b_ln_g"][j][None], w["b_ln_b"][j][None]], [(d, BF16)], name=tag + "_ln")[0]
    h1 = _mm(y4, w["b_w_pw2"][j], "nn", [F32], name=tag + "_pw2", epi=lambda acc, h, b: acc + h + b, rows=[h],
             pars=[w["b_b_pw2"][j][None]])
    return h1, (a, y, y2, y4)


def _conformer_bwd(ctx, u, dh1, dh1b, dh1_colsum, w, j, tag, grads):
    a, y, y2, y4 = ctx
    t, d = u.shape
    grads["b_w_pw2"][j] = _mm(y4, dh1b, "tn", [F32], name=tag + "_dwpw2")
    grads["b_b_pw2"][j] = dh1_colsum[0]
    dy4 = _mm(dh1b, w["b_w_pw2"][j], "nt", [F32], name=tag + "_dy4")

    def ln_silu_bwd(dy4, y2, g, b):
        mu = _rowmean(y2)
        yc = y2 - mu
        rstd = lax.rsqrt(_rowmean(yc * yc) + EPS)
        xh = yc * rstd
        dy3 = dy4 * _dsilu(xh * g + b)
        dxh = dy3 * g
        return rstd * (dxh - _rowmean(dxh) - xh * _rowmean(dxh * xh)), _colsum(dy3 * xh), _colsum(dy3)
    dy2, dlg, dlb = _rowwise(ln_silu_bwd, [dy4, y2], [w["b_ln_g"][j][None], w["b_ln_b"][j][None]], [(d, F32)], [d, d],
                             name=tag + "_ln_bwd")
    grads["b_ln_g"][j], grads["b_ln_b"][j] = dlg[0], dlb[0]
    dy, dw_dw, db_dw = _dwconv_bwd(y, dy2, w["b_w_dw"][j], tag + "_dw_bwd")
    grads["b_w_dw"][j], grads["b_b_dw"][j] = dw_dw, db_dw[0]

    def glu_bwd(dy, val, gate):
        s = _sigmoid(gate)
        dval, dgate = dy * s, dy * val * s * (1.0 - s)
        return dval, dgate, _colsum(dval), _colsum(dgate)
    dval, dgate, dbv, dbg = _rowwise(glu_bwd, [dy, (a, d, 0), (a, d, 1)], [], [(d, BF16), (d, BF16)], [d, d],
                                     name=tag + "_glu_bwd")
    da = jnp.concatenate([dval, dgate], axis=1)
    grads["b_b_pw1"][j] = jnp.concatenate([dbv[0], dbg[0]])
    grads["b_w_pw1"][j] = _mm(u, da, "tn", [F32], name=tag + "_dwpw1")
    return _mm(da, w["b_w_pw1"][j], "nt", [F32], name=tag + "_du")


def _ssd_layouts(v, n_groups):
    t = v.shape[0]
    vg = v[:, :n_groups * SSM_R].reshape(t, n_groups, SSM_R).transpose(1, 0, 2)
    return vg, vg.reshape(n_groups, t // SSM_L, SSM_L, SSM_R).transpose(0, 1, 3, 2)


def _from_group_layout(vg, width):
    n_groups, t, _ = vg.shape
    v = vg.transpose(1, 0, 2).reshape(t, n_groups * SSM_R)
    return jnp.pad(v, ((0, 0), (0, width - n_groups * SSM_R)))


def _pad_lanes(v, width):
    return jnp.pad(v, [(0, 0)] * (v.ndim - 1) + [(0, width - v.shape[-1])])


def _mamba_weights(w, j, d_inner, conv_c):
    w_in = w["c_w_in"][j]
    n_heads = w_in.shape[1] - d_inner - conv_c
    return (w_in[:, :d_inner], w_in[:, d_inner:d_inner + conv_c], _pad_lanes(w_in[:, d_inner + conv_c:], 128), n_heads)


def _mamba_fwd(u, h, w, j, tag):
    t, d = u.shape
    d_inner = w["c_w_out"][j].shape[0]
    conv_c = w["c_w_conv"][j].shape[1]
    w_z, w_xbc, w_dt, n_heads = _mamba_weights(w, j, d_inner, conv_c)
    n_groups = n_heads // SSM_R
    z = _mm(u, w_z, "nn", [F32], name=tag + "_in_z")
    xbc_raw = _mm(u, w_xbc, "nn", [F32], name=tag + "_in_xbc")
    dt_raw = _mm(u, w_dt, "nn", [F32], name=tag + "_in_dt")
    pre, xbc = _dwconv_fwd(xbc_raw, w["c_w_conv"][j], w["c_b_conv"][j][None], True, tag + "_conv")
    bias, a_log = _pad_lanes(w["c_dt_bias"][j][None], 128), _pad_lanes(w["c_A_log"][j][None], 128)
    dt, cs = _dt_fwd(dt_raw, bias, a_log, tag + "_dt")
    dt_g, _ = _ssd_layouts(dt, n_groups)
    cs_g, cs_row = _ssd_layouts(cs, n_groups)
    y, prev = _ssd_fwd(xbc, dt_g, cs_g, cs_row, d_inner, tag + "_ssd")
    d_cols = jnp.repeat(w["c_D"][j], SSM_P)[None]
    g = w["c_norm_g"][j][None]

    def gated_norm(y, xs, z, d_cols, g):
        yg = (y + xs * d_cols) * (z * _sigmoid(z))
        return yg * lax.rsqrt(_rowmean(yg * yg) + EPS) * g
    yn = _rowwise(gated_norm, [y, (xbc, d_inner, 0), z], [d_cols, g], [(d_inner, BF16)], name=tag + "_gnorm")[0]
    h1 = _mm(yn, w["c_w_out"][j], "nn", [F32], name=tag + "_out", epi=lambda acc, h: acc + h, rows=[h])
    return h1, (z, xbc_raw, dt_raw, pre, xbc, dt, dt_g, cs_g, cs_row, y, prev, yn)


def _mamba_bwd(ctx, u, dh1, dh1b, w, j, tag, grads):
    z, xbc_raw, dt_raw, pre, xbc, dt, dt_g, cs_g, cs_row, y, prev, yn = ctx
    t, d = u.shape
    d_inner = w["c_w_out"][j].shape[0]
    conv_c = w["c_w_conv"][j].shape[1]
    w_z, w_xbc, w_dt, n_heads = _mamba_weights(w, j, d_inner, conv_c)
    n_groups = n_heads // SSM_R
    grads["c_w_out"][j] = _mm(yn, dh1b, "tn", [F32], name=tag + "_dwout")
    dyn = _mm(dh1b, w["c_w_out"][j], "nt", [F32], name=tag + "_dyn")
    d_cols = jnp.repeat(w["c_D"][j], SSM_P)[None]
    g = w["c_norm_g"][j][None]

    def gated_norm_bwd(dyn, y, xs, z, d_cols, g):
        sg = _sigmoid(z)
        yt = y + xs * d_cols
        yg = yt * (z * sg)
        rstd = lax.rsqrt(_rowmean(yg * yg) + EPS)
        xh = yg * rstd
        dxh = dyn * g
        dyg = rstd * (dxh - xh * _rowmean(dxh * xh))
        dyt = dyg * (z * sg)
        return dyt, dyg * yt * (sg * (1.0 + z * (1.0 - sg))), dyt * d_cols, _colsum(dyn * xh), _colsum(dyt * xs)
    dyt, dz, dxs_skip, dg, dd_cols = _rowwise(
        gated_norm_bwd, [dyn, y, (xbc, d_inner, 0), z], [d_cols, g], [(d_inner, F32), (d_inner, BF16), (d_inner, F32)],
        [d_inner, d_inner], name=tag + "_gnorm_bwd")
    grads["c_norm_g"][j] = dg[0]
    grads["c_D"][j] = dd_cols.reshape(n_heads, SSM_P).sum(axis=1)
    dx, db, dc, ddtx_g, dda_g = _ssd_bwd(xbc, dt_g, cs_g, cs_row, prev, dyt, d_inner, tag + "_ssd_bwd")
    dxs = _add(dx, dxs_skip, tag + "_dxs")
    dpost = jnp.concatenate([dxs, db, dc], axis=1)
    dpre = _rowwise(lambda dpost, pre: dpost * _dsilu(pre), [dpost, pre], [], [(conv_c, F32)], name=tag + "_silu_bwd")[0]
    dxbc, dw_conv, db_conv = _dwconv_bwd(xbc_raw, dpre, w["c_w_conv"][j], tag + "_conv_bwd")
    grads["c_w_conv"][j], grads["c_b_conv"][j] = dw_conv, db_conv[0]
    bias, a_log = _pad_lanes(w["c_dt_bias"][j][None], 128), _pad_lanes(w["c_A_log"][j][None], 128)
    ddt_raw, dbias, da_log = _dt_bwd(_from_group_layout(ddtx_g, 128), _from_group_layout(dda_g, 128), dt_raw, dt, bias,
                                     a_log, tag + "_dt_bwd")
    grads["c_dt_bias"][j], grads["c_A_log"][j] = dbias[0, :n_heads], da_log[0, :n_heads]
    dxbc_b = dxbc.astype(BF16)
    grads["c_w_in"][j] = jnp.concatenate(
        [_mm(u, dz, "tn", [F32], name=tag + "_dwin_z"), _mm(u, dxbc_b, "tn", [F32], name=tag + "_dwin_xbc"),
         _mm(u, ddt_raw, "tn", [F32], name=tag + "_dwin_dt")[:, :n_heads]], axis=1)
    du = _mm(dz, w_z, "nt", [F32], name=tag + "_du_z")
    du = _mm(dxbc_b, w_xbc, "nt", [F32], name=tag + "_du_xbc", epi=lambda acc, r: acc + r, rows=[du])
    return _mm(ddt_raw, w_dt, "nt", [F32], name=tag + "_du_dt", epi=lambda acc, r: acc + r, rows=[du])


def _local_step(x, p, tgt, w):
    depth = p.shape[0]
    t, d = x.shape
    grads = {name: {} for name in WEIGHTS}
    saved = []
    h = x
    for i in range(depth):
        kind, j, tag = i % N_MIXERS, i // N_MIXERS, f"l{i}"
        u = _rms_fwd(h, w["mix_norm_g"][i][None], tag + "_rms_mix")
        fwd = (_attention_fwd, _conformer_fwd, _mamba_fwd)[kind]
        h1, ctx = fwd(u, h, w, j, tag)
        u2 = _rms_fwd(h1, w["mlp_norm_g"][i][None], tag + "_rms_mlp")
        a, r = _mm(u2, w["m_w1"][i], "nn", [F32, BF16], name=tag + "_w1",
                   epi=lambda acc: (acc, jnp.square(jnp.maximum(acc, 0.0))))
        h2 = _mm(r, w["m_w2"][i], "nn", [F32], name=tag + "_w2", epi=lambda acc, h: acc + h, rows=[h1])
        u3 = _rms_fwd(h2, w["ple_norm_g"][i][None], tag + "_rms_ple")
        gl = _mm(u3, w["ple_w_gate"][i], "nn", [F32], name=tag + "_gate")
        pp = _mm(p[i], w["ple_w_proj"][i], "nn", [F32], name=tag + "_proj")
        h3 = _ple_fwd(h2, gl, pp, tag + "_ple")
        saved.append((h, u, ctx, h1, u2, a, r, h2, u3, gl, pp))
        h = h3

    dh, sq_err = _loss_and_grad(h, tgt, "loss")
    for i in reversed(range(depth)):
        kind, j, tag = i % N_MIXERS, i // N_MIXERS, f"l{i}"
        h0, u, ctx, h1, u2, a, r, h2, u3, gl, pp = saved[i]
        dgl, dpp = _ple_bwd(dh, gl, pp, tag + "_ple_bwd")
        grads["ple_w_proj"][i] = _mm(p[i], dpp, "tn", [F32], name=tag + "_dwproj")
        grads["ple_w_gate"][i] = _mm(u3, dgl, "tn", [F32], name=tag + "_dwgate")
        du3 = _mm(dgl, w["ple_w_gate"][i], "nt", [F32], name=tag + "_du3")
        dh2, dh2b, dg, _ = _rms_bwd(du3, h2, w["ple_norm_g"][i][None], dh, tag + "_rms_ple_bwd")
        grads["ple_norm_g"][i] = dg[0]
        grads["m_w2"][i] = _mm(r, dh2b, "tn", [F32], name=tag + "_dw2")
        da = _mm(dh2b, w["m_w2"][i], "nt", [BF16], name=tag + "_da",
                 epi=lambda acc, a: acc * (2.0 * jnp.maximum(a, 0.0)), rows=[a])
        grads["m_w1"][i] = _mm(u2, da, "tn", [F32], name=tag + "_dw1")
        du2 = _mm(da, w["m_w1"][i], "nt", [F32], name=tag + "_du2")
        dh1, dh1b, dg, dh1_colsum = _rms_bwd(du2, h1, w["mlp_norm_g"][i][None], dh2, tag + "_rms_mlp_bwd")
        grads["mlp_norm_g"][i] = dg[0]
        if kind == 0:
            du = _attention_bwd(ctx, u, dh1, dh1b, w, j, tag, grads)
        elif kind == 1:
            du = _conformer_bwd(ctx, u, dh1, dh1b, dh1_colsum, w, j, tag, grads)
        else:
            du = _mamba_bwd(ctx, u, dh1, dh1b, w, j, tag, grads)
        dh, _, dg, _ = _rms_bwd(du, h0, w["mix_norm_g"][i][None], dh1, tag + "_rms_mix_bwd")
        grads["mix_norm_g"][i] = dg[0]
    grads = {name: jnp.stack([g[k] for k in sorted(g)]) for name, g in grads.items()}
    return sq_err, dh, grads


def kernel(x, p, mix_norm_g, mlp_norm_g, ple_norm_g, a_wqkv, a_q_norm_g, a_k_norm_g, a_sinks, a_wo, b_w_pw1, b_b_pw1, b_w_dw, b_b_dw, b_ln_g, b_ln_b, b_w_pw2, b_b_pw2, c_w_in, c_w_conv, c_b_conv, c_dt_bias, c_A_log, c_D, c_norm_g, c_w_out, m_w1, m_w2, ple_w_proj, ple_w_gate, loss_target, m_mix_norm_g, m_mlp_norm_g, m_ple_norm_g, m_a_wqkv, m_a_q_norm_g, m_a_k_norm_g, m_a_sinks, m_a_wo, m_b_w_pw1, m_b_b_pw1, m_b_w_dw, m_b_b_dw, m_b_ln_g, m_b_ln_b, m_b_w_pw2, m_b_b_pw2, m_c_w_in, m_c_w_conv, m_c_b_conv, m_c_dt_bias, m_c_A_log, m_c_D, m_c_norm_g, m_c_w_out, m_m_w1, m_m_w2, m_ple_w_proj, m_ple_w_gate, v_mix_norm_g, v_mlp_norm_g, v_ple_norm_g, v_a_wqkv, v_a_q_norm_g, v_a_k_norm_g, v_a_sinks, v_a_wo, v_b_w_pw1, v_b_b_pw1, v_b_w_dw, v_b_b_dw, v_b_ln_g, v_b_ln_b, v_b_w_pw2, v_b_b_pw2, v_c_w_in, v_c_w_conv, v_c_b_conv, v_c_dt_bias, v_c_A_log, v_c_D, v_c_norm_g, v_c_w_out, v_m_w1, v_m_w2, v_ple_w_proj, v_ple_w_gate):
    args = dict(locals())
    wl = {n: args[n] for n in WEIGHTS}
    ml = {n: args["m_" + n] for n in WEIGHTS}
    vl = {n: args["v_" + n] for n in WEIGHTS}
    sharded = BIG_SHARDED + SMALL_SHARDED
    d_model = x.shape[-1]

    whole = {n: wl[n] for n in REPLICATED}
    big = _all_gather(_pack([wl[n].astype(BF16) for n, _ in BIG_SHARDED]), "gather_weights")
    for (n, axis), t in zip(BIG_SHARDED, _unpack(big, [wl[n].shape for n, _ in BIG_SHARDED], (N_DEV,))):
        whole[n] = _join_shards(t, axis)
    small = _all_gather(_pack([wl[n] for n, _ in SMALL_SHARDED]), "gather_small")
    for (n, axis), t in zip(SMALL_SHARDED, _unpack(small, [wl[n].shape for n, _ in SMALL_SHARDED], (N_DEV,))):
        whole[n] = _join_shards(t, axis)

    sq_err, grad_x, grads = _local_step(x[0], p[:, 0], loss_target[0], whole)
    loss = lax.psum(0.5 * jnp.sum(sq_err) / d_model, ("x", "y", "c"))

    g_sh = _reduce_scatter(_pack([_split_shards(grads[n], axis) for n, axis in sharded], (N_DEV,)))
    g_rep = _sum_blocks(_all_gather(_pack([grads[n] for n in REPLICATED]), "gather_small_grads"), "sum_small_grads")

    out = {}
    for names, g_slab in (([n for n, _ in sharded], g_sh), (list(REPLICATED), g_rep)):
        shapes = [wl[n].shape for n in names]
        delta, new_m, new_v = _adamw(_pack([wl[n] for n in names]), g_slab, _pack([ml[n] for n in names]),
                                     _pack([vl[n] for n in names]), "adamw_" + names[0])
        for kind, slab in (("grad", g_slab), ("delta", delta), ("new_m", new_m), ("new_v", new_v)):
            for n, a in zip(names, _unpack(slab, shapes)):
                out[kind + "_" + n] = a
    return (loss, grad_x[None], *[out[k + "_" + n] for k in ("grad", "delta", "new_m", "new_v") for n in WEIGHTS])
```

```python
import math
from typing import Any, NamedTuple

import jax
import jax.numpy as jnp
from jax import lax
from jax.experimental import pallas as pl
from jax.experimental.pallas import tpu as pltpu

F32 = jnp.float32
BF16 = jnp.bfloat16
SDS = jax.ShapeDtypeStruct

EPS = 1e-6
N_MIXERS = 3
HEAD_DIM = 64
Q_PER_KV = 4
ATTN_BLOCK = 128
SSM_P = 64
SSM_R = 4
SSM_N = 128
SSM_L = 128
ADAM_LR, ADAM_B1, ADAM_B2, ADAM_EPS, ADAM_WD, ADAM_STEP = 0.001, 0.9, 0.999, 1e-08, 0.01, 10
N_DEV = 8
PACK_COLS = 1024
PACK_ROW_ALIGN = 16
VMEM_LIMIT = 56 * 1024 * 1024
MM_VMEM_BUDGET = 40 * 1024 * 1024
ROW_VMEM_BUDGET = 24 * 1024 * 1024
CONV_PAD = 32
CONV_ROWS = 64
NEG_BIG = -1e30

BIG_SHARDED = (("a_wqkv", 2), ("a_wo", 1), ("b_w_pw1", 2), ("b_w_pw2", 1), ("c_w_in", 2), ("c_w_out", 1),
               ("m_w1", 2), ("m_w2", 1), ("ple_w_proj", 2), ("ple_w_gate", 1))
SMALL_SHARDED = (("b_w_dw", 2), ("c_w_conv", 2), ("c_b_conv", 1), ("c_norm_g", 1))
REPLICATED = ("mix_norm_g", "mlp_norm_g", "ple_norm_g", "a_q_norm_g", "a_k_norm_g", "a_sinks", "b_b_pw1", "b_b_dw",
              "b_ln_g", "b_ln_b", "b_b_pw2", "c_dt_bias", "c_A_log", "c_D")
DIRECT = frozenset(("a_wo", "b_w_pw1", "b_w_pw2", "c_w_out", "m_w1", "m_w2", "ple_w_proj", "ple_w_gate"))
SHARD_AXIS = dict(BIG_SHARDED + SMALL_SHARDED)
WEIGHTS = ("mix_norm_g", "mlp_norm_g", "ple_norm_g", "a_wqkv", "a_q_norm_g", "a_k_norm_g", "a_sinks", "a_wo",
           "b_w_pw1", "b_b_pw1", "b_w_dw", "b_b_dw", "b_ln_g", "b_ln_b", "b_w_pw2", "b_b_pw2", "c_w_in", "c_w_conv",
           "c_b_conv", "c_dt_bias", "c_A_log", "c_D", "c_norm_g", "c_w_out", "m_w1", "m_w2", "ple_w_proj",
           "ple_w_gate")


def _pcall(body, **kw):
    return pl.pallas_call(body, **kw)


def _params(*sem):
    return pltpu.CompilerParams(dimension_semantics=sem, vmem_limit_bytes=VMEM_LIMIT)


def _sigmoid(x):
    return 1.0 / (1.0 + jnp.exp(-x))


def _dsilu(x):
    s = _sigmoid(x)
    return s * (1.0 + x * (1.0 - s))


def _colsum(x):
    return jnp.sum(x, axis=0, keepdims=True)


def _rowmean(x):
    return jnp.mean(x, axis=-1, keepdims=True)


def _rowwise(fn, rows, pars, row_outs, acc_outs=(), *, name, tm=None):
    rows = [r if isinstance(r, tuple) else (r, r.shape[1], 0) for r in rows]
    t = rows[0][0].shape[0]
    per_row = sum(w * a.dtype.itemsize for a, w, _ in rows) + sum(w * jnp.dtype(d).itemsize for w, d in row_outs)
    if tm is None:
        tm = 1024
        while tm > 8 and (t % tm or 2 * tm * per_row > ROW_VMEM_BUDGET):
            tm //= 2
    assert t % tm == 0, (name, t, tm)
    n_in, n_row = len(rows) + len(pars), len(row_outs)

    def body(*refs):
        outs = fn(*[r[...] for r in refs[:n_in]])
        outs = outs if isinstance(outs, tuple) else (outs,)
        for ref, o in zip(refs[n_in:n_in + n_row], outs[:n_row]):
            ref[...] = o.astype(ref.dtype)
        acc_refs = refs[n_in + n_row:]
        if acc_refs:
            @pl.when(pl.program_id(0) == 0)
            def _():
                for ref in acc_refs:
                    ref[...] = jnp.zeros_like(ref)
            for ref, o in zip(acc_refs, outs[n_row:]):
                ref[...] += o

    in_specs = [pl.BlockSpec((tm, w), lambda i, cb=cb: (i, cb)) for _, w, cb in rows]
    in_specs += [pl.BlockSpec(p.shape, lambda i: (0, 0)) for p in pars]
    out_specs = [pl.BlockSpec((tm, w), lambda i: (i, 0)) for w, _ in row_outs]
    out_specs += [pl.BlockSpec((1, w), lambda i: (0, 0)) for w in acc_outs]
    out_shape = [SDS((t, w), d) for w, d in row_outs] + [SDS((1, w), F32) for w in acc_outs]
    return _pcall(body, grid=(t // tm,), in_specs=in_specs, out_specs=out_specs, out_shape=out_shape, name=name,
                  compiler_params=_params("arbitrary"))(*[a for a, _, _ in rows], *pars)


_DOT_DIMS = {"nn": (((1,), (0,)), ((), ())), "nt": (((1,), (1,)), ((), ())), "tn": (((0,), (0,)), ((), ()))}


class _Sharded(NamedTuple):
    g: Any
    layer: int
    axis: int


def _mm(a, b, form, out_dtypes, *, name, epi=None, rows=(), pars=(), shard_out=None):
    sharded = isinstance(b, _Sharded)
    if sharded:
        wa, wb = b.g.shape[2:]
        b_shape = (N_DEV * wa, wb) if b.axis == 0 else (wa, N_DEV * wb)
        b_item, layer = b.g.dtype.itemsize, b.layer
    else:
        b_shape, b_item = b.shape, b.dtype.itemsize
    if form == "tn":
        k, m = a.shape
        n = b_shape[1]
    else:
        m, k = a.shape
        n = b_shape[0] if form == "nt" else b_shape[1]
    rows = [r if isinstance(r, tuple) else (r, 0) for r in rows]
    tn = next((c for c in (512, 256, 128) if n % c == 0), n)
    tm_max = 1024
    if sharded and b.axis == 1 and form == "nn":
        tn = min(tn, wb)
    if shard_out == 1:
        tn = min(tn, n // N_DEV)
    if shard_out == 0:
        tm_max = m // N_DEV

    def need(tm_):
        blk = tm_ * k * a.dtype.itemsize + tn * k * b_item
        blk += sum(tm_ * tn * jnp.dtype(d).itemsize for d in out_dtypes) + sum(tm_ * tn * r.dtype.itemsize for r, _ in rows)
        return 2 * blk

    tm = next((c for c in (1024, 512, 256, 128, 64, 32, 16) if c <= tm_max and m % c == 0 and need(c) <= MM_VMEM_BUDGET), None)
    assert tm is not None, (name, m, k, n)
    nx = len(rows) + len(pars)
    pieces = sharded and form == "nt" and b.axis == 1

    def body(*refs):
        av = refs[0][...].astype(BF16)
        if pieces:
            acc = jnp.zeros((tm, tn), F32)
            for d in range(N_DEV):
                acc += lax.dot_general(av[:, d * wb:(d + 1) * wb], refs[1][d], _DOT_DIMS["nt"], preferred_element_type=F32)
        else:
            bv = refs[1][...]
            if sharded and bv.ndim == 3:
                bv = bv.reshape(bv.shape[0] * bv.shape[1], bv.shape[2])
            acc = lax.dot_general(av, bv.astype(BF16), _DOT_DIMS[form], preferred_element_type=F32)
        outs = epi(acc, *[r[...] for r in refs[2:2 + nx]]) if epi else acc
        outs = outs if isinstance(outs, tuple) else (outs,)
        for ref, o in zip(refs[2 + nx:], outs):
            ref[...] = o.astype(ref.dtype)

    a_spec = pl.BlockSpec((k, tm), lambda i, j: (0, i)) if form == "tn" else pl.BlockSpec((tm, k), lambda i, j: (i, 0))
    if not sharded:
        b_spec = pl.BlockSpec((tn, k), lambda i, j: (j, 0)) if form == "nt" else pl.BlockSpec((k, tn), lambda i, j: (0, j))
    elif form == "nn" and b.axis == 0:
        b_spec = pl.BlockSpec((N_DEV, None, wa, tn), lambda i, j: (0, layer, 0, j))
    elif form == "nn":
        per = wb // tn
        b_spec = pl.BlockSpec((None, None, wa, tn), lambda i, j: (j // per, layer, 0, j % per))
    elif form == "nt" and b.axis == 0 and tn >= wa:
        b_spec = pl.BlockSpec((tn // wa, None, wa, k), lambda i, j: (j, layer, 0, 0))
    elif form == "nt" and b.axis == 0:
        per = wa // tn
        b_spec = pl.BlockSpec((None, None, tn, k), lambda i, j: (j // per, layer, j % per, 0))
    else:
        assert form == "nt", (name, form)
        b_spec = pl.BlockSpec((N_DEV, None, tn, wb), lambda i, j: (0, layer, j, 0))
    in_specs = [a_spec, b_spec]
    in_specs += [pl.BlockSpec((tm, tn), lambda i, j, off=off: (i, j + off)) for _, off in rows]
    in_specs += [pl.BlockSpec((1, tn), lambda i, j: (0, j)) for _ in pars]
    if shard_out == 0:
        per_m = m // N_DEV // tm
        out_specs = [pl.BlockSpec((None, tm, tn), lambda i, j: (i // per_m, i % per_m, j)) for _ in out_dtypes]
        out_shape = [SDS((N_DEV, m // N_DEV, n), d) for d in out_dtypes]
    elif shard_out == 1:
        per_n = n // N_DEV // tn
        out_specs = [pl.BlockSpec((None, tm, tn), lambda i, j: (j // per_n, i, j % per_n)) for _ in out_dtypes]
        out_shape = [SDS((N_DEV, m, n // N_DEV), d) for d in out_dtypes]
    else:
        out_specs = [pl.BlockSpec((tm, tn), lambda i, j: (i, j)) for _ in out_dtypes]
        out_shape = [SDS((m, n), d) for d in out_dtypes]
    res = _pcall(body, grid=(m // tm, n // tn), in_specs=in_specs, out_specs=out_specs, out_shape=out_shape, name=name,
                 compiler_params=_params("arbitrary", "arbitrary"))(a, b.g if sharded else b, *[r for r, _ in rows], *pars)
    return res[0] if len(out_dtypes) == 1 else res


def _rms_fwd(x, g, name):
    def fn(x, g):
        return x * lax.rsqrt(_rowmean(x * x) + EPS) * g
    return _rowwise(fn, [x], [g], [(x.shape[1], BF16)], name=name)[0]


def _rms_bwd(dy, x, g, dres, name):
    def fn(dy, x, dres, g):
        rstd = lax.rsqrt(_rowmean(x * x) + EPS)
        xh = x * rstd
        dxh = dy * g
        tot = dres + rstd * (dxh - xh * _rowmean(dxh * xh))
        return tot, tot, _colsum(dy * xh), _colsum(tot)
    d = x.shape[1]
    return _rowwise(fn, [dy, x, dres], [g], [(d, F32), (d, BF16)], [d, d], name=name)


def _ple_fwd(h, gl, pp, name):
    return _rowwise(lambda h, gl, pp: h + _sigmoid(gl) * pp, [h, gl, pp], [], [(h.shape[1], F32)], name=name)[0]


def _ple_bwd(dh, gl, pp, name):
    def fn(dh, gl, pp):
        gate = _sigmoid(gl)
        return dh * pp * gate * (1.0 - gate), dh * gate
    d = dh.shape[1]
    return _rowwise(fn, [dh, gl, pp], [], [(d, BF16), (d, BF16)], name=name)


def _loss_and_grad(h, tgt, name):
    d = h.shape[1]

    def fn(h, tgt):
        err = h - tgt
        return err * (1.0 / d), _colsum(err * err)
    return _rowwise(fn, [h, tgt], [], [(d, F32)], [d], name=name)


def _add(a, b, name):
    return _rowwise(lambda a, b: a + b, [a, b], [], [(a.shape[1], F32)], name=name)[0]


def _adamw(w, g, m, v, name):
    def fn(w, g, m, v):
        m = ADAM_B1 * m + (1.0 - ADAM_B1) * g
        v = ADAM_B2 * v + (1.0 - ADAM_B2) * (g * g)
        m_hat = m / (1.0 - ADAM_B1 ** ADAM_STEP)
        v_hat = v / (1.0 - ADAM_B2 ** ADAM_STEP)
        return -ADAM_LR * (m_hat / (jnp.sqrt(v_hat) + ADAM_EPS) + ADAM_WD * w), m, v
    c = w.shape[1]
    return _rowwise(fn, [w, g, m, v], [], [(c, F32)] * 3, name=name)


def _sum_blocks(x, name):
    n = x.shape[0]

    def body(x_ref, o_ref):
        acc = x_ref[0]
        for j in range(1, n):
            acc = acc + x_ref[j]
        o_ref[...] = acc
    return _pcall(body, out_shape=SDS(x.shape[1:], x.dtype), name=name, compiler_params=_params())(x)


def _shifted(chunk, off, rows):
    if off % 8 == 0:
        return chunk[off:off + rows]
    return pltpu.roll(chunk, chunk.shape[0] - off, 0)[:rows]


def _conv_cols(c):
    return 256 if c % 256 == 0 else 128


def _rows_at(i, tb):
    return pl.ds(i * tb, tb) if isinstance(i, int) else pl.ds(pl.multiple_of(i * tb, 8), tb)


def _chunk_before(ref, i, tb, pad):
    if isinstance(i, int) and i == 0:
        return jnp.concatenate([jnp.zeros((pad, ref.shape[1]), F32), ref[0:tb, :]], axis=0)
    if isinstance(i, int):
        return ref[i * tb - pad:(i + 1) * tb, :]
    return ref[pl.ds(pl.multiple_of(i * tb - pad, 8), tb + pad), :]


def _chunk_after(ref, i, tb, pad, last):
    if last:
        return jnp.concatenate([ref[i * tb:(i + 1) * tb, :], jnp.zeros((pad, ref.shape[1]), F32)], axis=0)
    if isinstance(i, int):
        return ref[i * tb:(i + 1) * tb + pad, :]
    return ref[pl.ds(pl.multiple_of(i * tb, 8), tb + pad), :]


def _dwconv_fwd(y, w, b, silu, name):
    t, c = y.shape
    taps = w.shape[0]
    cb, tb, pad = _conv_cols(c), CONV_ROWS, CONV_PAD

    def body(y_ref, w_ref, b_ref, *out_refs):
        def step(i, carry):
            chunk = _chunk_before(y_ref, i, tb, pad)
            acc = jnp.broadcast_to(b_ref[...], (tb, cb))
            for k in range(taps):
                acc = acc + _shifted(chunk, pad - (taps - 1) + k, tb) * w_ref[k:k + 1, :]
            rows = _rows_at(i, tb)
            out_refs[0][rows, :] = acc
            if silu:
                out_refs[1][rows, :] = acc * _sigmoid(acc)
            return carry
        step(0, 0)
        lax.fori_loop(1, t // tb, step, 0)

    n_out = 2 if silu else 1
    return _pcall(body, grid=(c // cb,),
                  in_specs=[pl.BlockSpec((t, cb), lambda j: (0, j)), pl.BlockSpec((taps, cb), lambda j: (0, j)),
                            pl.BlockSpec((1, cb), lambda j: (0, j))],
                  out_specs=[pl.BlockSpec((t, cb), lambda j: (0, j))] * n_out,
                  out_shape=[SDS((t, c), F32)] * n_out, name=name, compiler_params=_params("arbitrary"))(y, w, b)


def _dwconv_bwd(y, dout, w, name):
    t, c = y.shape
    taps = w.shape[0]
    cb, tb, pad = _conv_cols(c), CONV_ROWS, CONV_PAD
    taps_pad = -(-taps // 8) * 8
    n_steps = t // tb
    assert n_steps >= 2

    def body(y_ref, d_ref, w_ref, dy_ref, dw_ref, db_ref):
        dw_ref[...] = jnp.zeros_like(dw_ref)

        def step(i, db, last=False):
            ychunk = _chunk_before(y_ref, i, tb, pad)
            dchunk = _chunk_after(d_ref, i, tb, pad, last)
            d0 = dchunk[:tb]
            acc = jnp.zeros((tb, cb), F32)
            for k in range(taps):
                acc = acc + _shifted(dchunk, taps - 1 - k, tb) * w_ref[k:k + 1, :]
                dw_ref[k:k + 1, :] += _colsum(d0 * _shifted(ychunk, pad - (taps - 1) + k, tb))
            dy_ref[_rows_at(i, tb), :] = acc
            return db + _colsum(d0)
        db = step(0, jnp.zeros((1, cb), F32))
        db = lax.fori_loop(1, n_steps - 1, step, db)
        db_ref[...] = step(n_steps - 1, db, last=True)

    dy, dw, db = _pcall(
        body, grid=(c // cb,),
        in_specs=[pl.BlockSpec((t, cb), lambda j: (0, j)), pl.BlockSpec((t, cb), lambda j: (0, j)),
                  pl.BlockSpec((taps, cb), lambda j: (0, j))],
        out_specs=[pl.BlockSpec((t, cb), lambda j: (0, j)), pl.BlockSpec((taps_pad, cb), lambda j: (0, j)),
                   pl.BlockSpec((1, cb), lambda j: (0, j))],
        out_shape=[SDS((t, c), F32), SDS((taps_pad, c), F32), SDS((1, c), F32)], name=name,
        compiler_params=_params("arbitrary"))(y, dout, w)
    return dy, dw[:taps], db


def _head_norm(x, g):
    rstd = lax.rsqrt(_rowmean(x * x) + EPS)
    xh = x * rstd
    return xh * g, xh, rstd


def _head_norm_bwd(dy, xh, rstd, g):
    dxh = dy * g
    return rstd * (dxh - xh * _rowmean(dxh * xh)), _colsum(dy * xh)


def _attn_probs(q_ref, kp_ref, kc_ref, gq_ref, gk_ref, sinks_ref, n_heads):
    g, blk, hd = Q_PER_KV, ATTN_BLOCK, HEAD_DIM
    kv, n = pl.program_id(0), pl.program_id(1)
    q_raw = q_ref[...].reshape(g * blk, hd)
    k_raw = jnp.concatenate([kp_ref[...], kc_ref[...]], axis=0)
    qn, qh, q_rstd = _head_norm(q_raw, gq_ref[...])
    kn, kh, k_rstd = _head_norm(k_raw, gk_ref[...])
    s = lax.dot_general(qn.astype(BF16), kn.astype(BF16), _DOT_DIMS["nt"], preferred_element_type=F32) * hd ** -0.5
    row = lax.broadcasted_iota(jnp.int32, (g * blk, 1), 0)
    g_row = lax.shift_right_logical(row, 7)
    qi = lax.bitwise_and(row, blk - 1)
    kj = lax.broadcasted_iota(jnp.int32, (1, 2 * blk), 1)
    dist = qi - kj + blk
    valid = (dist >= 0) & (dist < blk) & (kj >= jnp.where(n > 0, 0, blk))
    head = (kv * g + g_row + 1).astype(F32)
    slope = jnp.exp(head * (-8.0 * math.log(2.0) / n_heads))
    s = jnp.where(valid, s - slope * dist.astype(F32), NEG_BIG)
    sink = jnp.zeros((g * blk, 1), F32)
    for gi in range(g):
        sink = jnp.where(g_row == gi, sinks_ref[kv * g + gi], sink)
    m = jnp.maximum(jnp.max(s, axis=1, keepdims=True), sink)
    e = jnp.exp(s - m)
    e_sink = jnp.exp(sink - m)
    inv = 1.0 / (jnp.sum(e, axis=1, keepdims=True) + e_sink)
    return (q_raw, qn, qh, q_rstd), (k_raw, kn, kh, k_rstd), e * inv, e_sink * inv


def _attn_specs(t):
    g, blk, hd = Q_PER_KV, ATTN_BLOCK, HEAD_DIM
    q_spec = pl.BlockSpec((None, g, blk, hd), lambda a, n: (a, 0, n, 0))
    prev_spec = pl.BlockSpec((None, blk, hd), lambda a, n: (a, jnp.maximum(n - 1, 0), 0))
    cur_spec = pl.BlockSpec((None, blk, hd), lambda a, n: (a, n, 0))
    gain_spec = pl.BlockSpec((1, hd), lambda a, n: (0, 0))
    sink_spec = pl.BlockSpec(memory_space=pltpu.SMEM)
    return q_spec, prev_spec, cur_spec, gain_spec, sink_spec


def _attn_fwd(q, k, v, gq, gk, sinks, name):
    n_kv, g, t, hd = q.shape
    blk = ATTN_BLOCK
    n_heads = n_kv * g

    def body(q_ref, kp_ref, kc_ref, vp_ref, vc_ref, gq_ref, gk_ref, sinks_ref, o_ref):
        _, _, p, _ = _attn_probs(q_ref, kp_ref, kc_ref, gq_ref, gk_ref, sinks_ref, n_heads)
        vb = jnp.concatenate([vp_ref[...], vc_ref[...]], axis=0).astype(BF16)
        o = jnp.dot(p.astype(BF16), vb, preferred_element_type=F32)
        o_ref[...] = o.reshape(g, blk, hd).astype(o_ref.dtype)

    q_spec, prev_spec, cur_spec, gain_spec, sink_spec = _attn_specs(t)
    return _pcall(body, grid=(n_kv, t // blk),
                  in_specs=[q_spec, prev_spec, cur_spec, prev_spec, cur_spec, gain_spec, gain_spec, sink_spec],
                  out_specs=q_spec, out_shape=SDS(q.shape, BF16), name=name,
                  compiler_params=_params("arbitrary", "arbitrary"))(q, k, k, v, v, gq, gk, sinks)


def _attn_bwd(q, k, v, do, gq, gk, sinks, name):
    n_kv, g, t, hd = q.shape
    blk = ATTN_BLOCK
    n_heads = n_kv * g

    def body(q_ref, kp_ref, kc_ref, vp_ref, vc_ref, do_ref, gq_ref, gk_ref, sinks_ref,
             dq_ref, dkp_ref, dkc_ref, dvp_ref, dvc_ref, dgq_ref, dgk_ref, dsink_ref):
        kv, n = pl.program_id(0), pl.program_id(1)
        (_, qn, qh, q_rstd), (_, kn, kh, k_rstd), p, p_sink = _attn_probs(
            q_ref, kp_ref, kc_ref, gq_ref, gk_ref, sinks_ref, n_heads)
        vb = jnp.concatenate([vp_ref[...], vc_ref[...]], axis=0).astype(BF16)
        dob = do_ref[...].reshape(g * blk, hd).astype(BF16)
        dp = lax.dot_general(dob, vb, _DOT_DIMS["nt"], preferred_element_type=F32)
        delta = jnp.sum(p * dp, axis=1, keepdims=True)
        ds = (p * (dp - delta) * hd ** -0.5).astype(BF16)
        dv = lax.dot_general(p.astype(BF16), dob, _DOT_DIMS["tn"], preferred_element_type=F32)
        dqn = jnp.dot(ds, kn.astype(BF16), preferred_element_type=F32)
        dkn = lax.dot_general(ds, qn.astype(BF16), _DOT_DIMS["tn"], preferred_element_type=F32)
        dq, dgq = _head_norm_bwd(dqn, qh, q_rstd, gq_ref[...])
        dk, dgk = _head_norm_bwd(dkn, kh, k_rstd, gk_ref[...])
        dq_ref[...] = dq.reshape(g, blk, hd)
        dkp_ref[...] = dk[:blk]
        dkc_ref[...] = dk[blk:]
        dvp_ref[...] = dv[:blk]
        dvc_ref[...] = dv[blk:]

        @pl.when((kv == 0) & (n == 0))
        def _():
            dgq_ref[...] = jnp.zeros_like(dgq_ref)
            dgk_ref[...] = jnp.zeros_like(dgk_ref)

        @pl.when(n == 0)
        def _():
            dsink_ref[...] = jnp.zeros_like(dsink_ref)
        dgq_ref[...] += dgq
        dgk_ref[...] += dgk
        dsink = jnp.sum((-p_sink * delta).reshape(g, blk, 1), axis=1)
        dsink_ref[...] += jnp.broadcast_to(dsink, (g, 128))

    q_spec, prev_spec, cur_spec, gain_spec, sink_spec = _attn_specs(t)
    kv_shape = SDS(k.shape, F32)
    return _pcall(
        body, grid=(n_kv, t // blk),
        in_specs=[q_spec, prev_spec, cur_spec, prev_spec, cur_spec, q_spec, gain_spec, gain_spec, sink_spec],
        out_specs=[q_spec, cur_spec, cur_spec, cur_spec, cur_spec, gain_spec, gain_spec,
                   pl.BlockSpec((None, g, 128), lambda a, n: (a, 0, 0))],
        out_shape=[SDS(q.shape, F32), kv_shape, kv_shape, kv_shape, kv_shape, SDS((1, hd), F32), SDS((1, hd), F32),
                   SDS((n_kv, g, 128), F32)],
        name=name, compiler_params=_params("arbitrary", "arbitrary"))(q, k, k, v, v, do, gq, gk, sinks)


def _col_head(width):
    return lax.shift_right_logical(lax.broadcasted_iota(jnp.int32, (1, width), 1), 6)


def _expand_heads(v, col_head):
    out = jnp.zeros((v.shape[0], col_head.shape[1]), F32)
    for r in range(SSM_R):
        out = jnp.where(col_head == r, v[:, r:r + 1], out)
    return out


def _head_sums(m, col_head):
    lane = lax.broadcasted_iota(jnp.int32, (1, SSM_R), 1)
    out = jnp.zeros((m.shape[0], SSM_R), F32)
    for r in range(SSM_R):
        out = jnp.where(lane == r, jnp.sum(jnp.where(col_head == r, m, 0.0), axis=1, keepdims=True), out)
    return out


def _ssd_specs(d_inner, n_groups, n_chunks, rev):
    l, w, n = SSM_L, SSM_R * SSM_P, SSM_N

    def cc(c):
        return n_chunks - 1 - c if rev else c
    x_spec = pl.BlockSpec((l, w), lambda g, c: (cc(c), g))
    b_spec = pl.BlockSpec((l, n), lambda g, c: (cc(c), d_inner // n + g))
    c_spec = pl.BlockSpec((l, n), lambda g, c: (cc(c), d_inner // n + n_groups + g))
    col_spec = pl.BlockSpec((None, l, SSM_R), lambda g, c: (g, cc(c), 0))
    row_spec = pl.BlockSpec((None, None, SSM_R, l), lambda g, c: (g, cc(c), 0, 0))
    state_spec = pl.BlockSpec((None, None, n, w), lambda g, c: (g, cc(c), 0, 0))
    return x_spec, b_spec, c_spec, col_spec, row_spec, state_spec


def _decay_matrix(a, a_row, r):
    l = SSM_L
    causal = lax.broadcasted_iota(jnp.int32, (l, l), 0) >= lax.broadcasted_iota(jnp.int32, (l, l), 1)
    return jnp.exp(jnp.where(causal, a[:, r:r + 1] - a_row[r:r + 1, :], NEG_BIG))


def _ssd_fwd(xbc, dt_g, cs_g, cs_row, d_inner, name):
    t = xbc.shape[0]
    l, w, n = SSM_L, SSM_R * SSM_P, SSM_N
    n_groups, n_chunks = d_inner // w, t // l

    def body(x_ref, b_ref, c_ref, dt_ref, a_ref, ar_ref, y_ref, prev_ref, h_scr):
        @pl.when(pl.program_id(1) == 0)
        def _():
            h_scr[...] = jnp.zeros_like(h_scr)
        col_head = _col_head(w)
        a, a_row = a_ref[...], ar_ref[...]
        h = h_scr[...]
        prev_ref[...] = h
        xdt = x_ref[...] * _expand_heads(dt_ref[...], col_head)
        xb, bb, cb = xdt.astype(BF16), b_ref[...].astype(BF16), c_ref[...].astype(BF16)
        a_last = a[l - 1:l, :]
        cbt = lax.dot_general(cb, bb, _DOT_DIMS["nt"], preferred_element_type=F32)
        y = jnp.zeros((l, w), F32)
        for r in range(SSM_R):
            m = (cbt * _decay_matrix(a, a_row, r)).astype(BF16)
            y = jnp.where(col_head == r, jnp.dot(m, xb, preferred_element_type=F32), y)
        y_off = jnp.dot(cb, h.astype(BF16), preferred_element_type=F32)
        y_ref[...] = y + _expand_heads(jnp.exp(a), col_head) * y_off
        fx = (xdt * _expand_heads(jnp.exp(a_last - a), col_head)).astype(BF16)
        states = lax.dot_general(bb, fx, _DOT_DIMS["tn"], preferred_element_type=F32)
        h_scr[...] = _expand_heads(jnp.exp(a_last), col_head) * h + states

    x_spec, b_spec, c_spec, col_spec, row_spec, state_spec = _ssd_specs(d_inner, n_groups, n_chunks, False)
    return _pcall(body, grid=(n_groups, n_chunks),
                  in_specs=[x_spec, b_spec, c_spec, col_spec, col_spec, row_spec],
                  out_specs=[x_spec, state_spec],
                  out_shape=[SDS((t, d_inner), F32), SDS((n_groups, n_chunks, n, w), F32)],
                  scratch_shapes=[pltpu.VMEM((n, w), F32)], name=name,
                  compiler_params=_params("arbitrary", "arbitrary"))(xbc, xbc, xbc, dt_g, cs_g, cs_row)


def _ssd_bwd(xbc, dt_g, cs_g, cs_row, prev, dy, d_inner, name):
    t = xbc.shape[0]
    l, w, n = SSM_L, SSM_R * SSM_P, SSM_N
    n_groups, n_chunks = d_inner // w, t // l

    def body(x_ref, b_ref, c_ref, dt_ref, a_ref, ar_ref, prev_ref, dy_ref,
             dx_ref, db_ref, dc_ref, ddtx_ref, dda_ref, dh_scr):
        @pl.when(pl.program_id(1) == 0)
        def _():
            dh_scr[...] = jnp.zeros_like(dh_scr)
        col_head = _col_head(w)
        a, a_row = a_ref[...], ar_ref[...]
        x, dyv = x_ref[...], dy_ref[...]
        h, dhn = prev_ref[...], dh_scr[...]
        dtx = _expand_heads(dt_ref[...], col_head)
        xdt = x * dtx
        a_last = a[l - 1:l, :]
        e_last = jnp.exp(a_last)
        xb, bb, cb = xdt.astype(BF16), b_ref[...].astype(BF16), c_ref[...].astype(BF16)
        hb, dhnb, dyb = h.astype(BF16), dhn.astype(BF16), dyv.astype(BF16)
        edy = (_expand_heads(jnp.exp(a), col_head) * dyv).astype(BF16)
        dc = lax.dot_general(edy, hb, _DOT_DIMS["nt"], preferred_element_type=F32)
        dh_scr[...] = (_expand_heads(e_last, col_head) * dhn
                       + lax.dot_general(cb, edy, _DOT_DIMS["tn"], preferred_element_type=F32))
        fx = xdt * _expand_heads(jnp.exp(a_last - a), col_head)
        g1 = jnp.dot(bb, dhnb, preferred_element_type=F32)
        db = lax.dot_general(fx.astype(BF16), dhnb, _DOT_DIMS["nt"], preferred_element_type=F32)
        q = _head_sums(fx * g1, col_head)
        d_e_last = _head_sums(_colsum(dhn * h), col_head)
        cbt = lax.dot_general(cb, bb, _DOT_DIMS["nt"], preferred_element_type=F32)
        dcbt = jnp.zeros((l, l), F32)
        dxd = jnp.zeros((l, w), F32)
        dda_diag = jnp.zeros((l, SSM_R), F32)
        rows_l = lax.broadcasted_iota(jnp.int32, (l, l), 0)
        cols_l = lax.broadcasted_iota(jnp.int32, (l, l), 1)
        upper = (cols_l >= rows_l).astype(F32)
        head_lane = lax.broadcasted_iota(jnp.int32, (1, SSM_R), 1)
        for r in range(SSM_R):
            decay = _decay_matrix(a, a_row, r)
            dyr = jnp.where(col_head == r, dyv, 0.0).astype(BF16)
            dm = lax.dot_general(dyr, xb, _DOT_DIMS["nt"], preferred_element_type=F32)
            dcbt = dcbt + dm * decay
            m = cbt * decay
            dxr = lax.dot_general(m.astype(BF16), dyb, _DOT_DIMS["tn"], preferred_element_type=F32)
            dxd = jnp.where(col_head == r, dxr, dxd)
            below = jnp.dot(upper, dm * m, precision=lax.Precision.HIGHEST, preferred_element_type=F32)
            dda_r = jnp.sum(jnp.where(cols_l < rows_l, below, 0.0), axis=1, keepdims=True)
            dda_diag = jnp.where(head_lane == r, dda_r, dda_diag)
        dcbb = dcbt.astype(BF16)
        dc_ref[...] = dc + jnp.dot(dcbb, bb, preferred_element_type=F32)
        db_ref[...] = db + lax.dot_general(dcbb, cb, _DOT_DIMS["tn"], preferred_element_type=F32)
        dxt = dxd + _expand_heads(jnp.exp(a_last - a), col_head) * g1
        dx_ref[...] = dxt * dtx
        ddtx_ref[...] = _head_sums(dxt * x, col_head)
        y_off = _expand_heads(jnp.exp(a), col_head) * jnp.dot(cb, hb, preferred_element_type=F32)
        row = lax.broadcasted_iota(jnp.int32, (l, 1), 0)
        da = _head_sums(dyv * y_off, col_head) + jnp.where(row == l - 1, e_last * d_e_last, 0.0)
        strict_lower = (cols_l < rows_l).astype(F32)
        dda_ref[...] = (dda_diag + jnp.dot(upper, da, precision=lax.Precision.HIGHEST, preferred_element_type=F32)
                        + jnp.dot(strict_lower, q, precision=lax.Precision.HIGHEST, preferred_element_type=F32))

    x_spec, b_spec, c_spec, col_spec, row_spec, state_spec = _ssd_specs(d_inner, n_groups, n_chunks, True)
    bc_out = pl.BlockSpec((l, n), lambda g, c: (n_chunks - 1 - c, g))
    return _pcall(body, grid=(n_groups, n_chunks),
                  in_specs=[x_spec, b_spec, c_spec, col_spec, col_spec, row_spec, state_spec, x_spec],
                  out_specs=[x_spec, bc_out, bc_out, col_spec, col_spec],
                  out_shape=[SDS((t, d_inner), F32), SDS((t, n_groups * n), F32), SDS((t, n_groups * n), F32),
                             SDS(dt_g.shape, F32), SDS(dt_g.shape, F32)],
                  scratch_shapes=[pltpu.VMEM((n, w), F32)], name=name,
                  compiler_params=_params("arbitrary", "arbitrary"))(xbc, xbc, xbc, dt_g, cs_g, cs_row, prev, dy)


def _softplus(x):
    return jnp.maximum(x, 0.0) + jnp.log1p(jnp.exp(-jnp.abs(x)))


def _dt_fwd(dt_raw, bias, a_log, name):
    l = SSM_L

    def fn(raw, bias, a_log):
        dt = _softplus(raw + bias)
        lower = (lax.broadcasted_iota(jnp.int32, (l, l), 0) >= lax.broadcasted_iota(jnp.int32, (l, l), 1)).astype(F32)
        cs = jnp.dot(lower, dt * -jnp.exp(a_log), precision=lax.Precision.HIGHEST, preferred_element_type=F32)
        return dt, cs
    wd = dt_raw.shape[1]
    return _rowwise(fn, [dt_raw], [bias, a_log], [(wd, F32), (wd, F32)], name=name, tm=l)


def _dt_bwd(ddtx, dda, dt_raw, dt, bias, a_log, name):
    def fn(ddtx, dda, raw, dt, bias, a_log):
        a = -jnp.exp(a_log)
        draw = (ddtx + dda * a) * _sigmoid(raw + bias)
        return draw, _colsum(draw), _colsum(dda * dt) * a
    wd = dt_raw.shape[1]
    return _rowwise(fn, [ddtx, dda, dt_raw, dt], [bias, a_log], [(wd, BF16)], [wd, wd], name=name)


MESH_IDS = pl.DeviceIdType.MESH
ANY_SPEC = pl.BlockSpec(memory_space=pl.ANY)


def _mesh_pos():
    return lax.axis_index("x"), lax.axis_index("y"), lax.axis_index("c")


def _all_gather(arrays, name):
    n = len(arrays)

    def body(*refs):
        x_refs, out_refs = refs[:n], refs[n:2 * n]
        send_sems, recv_sems, local_sems = refs[2 * n:]
        mx, my, mc = _mesh_pos()
        me, sibling = (mx, my, mc), (mx, my, 1 - mc)
        chips = [(1 - mx, my), (mx, 1 - my), (1 - mx, 1 - my)]

        def copy(a, k, block, to, src=None):
            px, py, pc = block
            slot = out_refs[a].at[4 * px + 2 * py + pc]
            return pltpu.make_async_remote_copy(
                src_ref=slot if src is None else src, dst_ref=slot, send_sem=send_sems.at[7 * a + k],
                recv_sem=recv_sems.at[7 * a + k], device_id=to, device_id_type=MESH_IDS)

        mine = [pltpu.make_async_copy(x_refs[a], out_refs[a].at[4 * mx + 2 * my + mc], local_sems.at[a])
                for a in range(n)]
        sends = []
        for a in range(n):
            mine[a].start()
            first = [copy(a, 0, me, sibling, src=x_refs[a])]
            first += [copy(a, 1 + j, me, (*chip, mc), src=x_refs[a]) for j, chip in enumerate(chips)]
            for cp in first:
                cp.start()
            sends += first
        for j, chip in enumerate(chips):
            for a in range(n):
                copy(a, 1 + j, (*chip, mc), me).wait_recv()
                passed = copy(a, 4 + j, (*chip, mc), sibling)
                passed.start()
                sends.append(passed)
        for a in range(n):
            copy(a, 0, sibling, me).wait_recv()
            for j, chip in enumerate(chips):
                copy(a, 4 + j, (*chip, 1 - mc), me).wait_recv()
        for cp in sends:
            cp.wait_send()
        for cp in mine:
            cp.wait()

    return _pcall(body, out_shape=[SDS((N_DEV,) + x.shape, x.dtype) for x in arrays], in_specs=[ANY_SPEC] * n,
                  out_specs=[ANY_SPEC] * n,
                  scratch_shapes=[pltpu.SemaphoreType.DMA((7 * n,)), pltpu.SemaphoreType.DMA((7 * n,)),
                                  pltpu.SemaphoreType.DMA((n,))], name=name)(*arrays)


def _swap_with_sibling(arrays, name):
    n = len(arrays)

    def body(*refs):
        g_refs, out_refs = refs[:n], refs[n:2 * n]
        send_sems, recv_sems = refs[2 * n:]
        mx, my, mc = _mesh_pos()
        copies = [pltpu.make_async_remote_copy(
            src_ref=g_refs[a].at[2 * k + (1 - mc)], dst_ref=out_refs[a].at[k], send_sem=send_sems.at[4 * a + k],
            recv_sem=recv_sems.at[4 * a + k], device_id=(mx, my, 1 - mc), device_id_type=MESH_IDS)
            for a in range(n) for k in range(4)]
        for cp in copies:
            cp.start()
        for cp in copies:
            cp.wait()

    return _pcall(body, out_shape=[SDS((4,) + g.shape[1:], g.dtype) for g in arrays], in_specs=[ANY_SPEC] * n,
                  out_specs=[ANY_SPEC] * n,
                  scratch_shapes=[pltpu.SemaphoreType.DMA((4 * n,)), pltpu.SemaphoreType.DMA((4 * n,))],
                  name=name)(*arrays)


def _swap_with_chips(arrays, name):
    n = len(arrays)

    def body(*refs):
        p_refs, out_refs = refs[:n], refs[n:2 * n]
        send_sems, recv_sems = refs[2 * n:]
        mx, my, mc = _mesh_pos()
        chips = [(1 - mx, my), (mx, 1 - my), (1 - mx, 1 - my)]
        copies = [pltpu.make_async_remote_copy(
            src_ref=p_refs[a].at[2 * px + py], dst_ref=out_refs[a].at[j], send_sem=send_sems.at[3 * a + j],
            recv_sem=recv_sems.at[3 * a + j], device_id=(px, py, mc), device_id_type=MESH_IDS)
            for a in range(n) for j, (px, py) in enumerate(chips)]
        for cp in copies:
            cp.start()
        for cp in copies:
            cp.wait()

    return _pcall(body, out_shape=[SDS((3,) + p.shape[1:], p.dtype) for p in arrays], in_specs=[ANY_SPEC] * n,
                  out_specs=[ANY_SPEC] * n,
                  scratch_shapes=[pltpu.SemaphoreType.DMA((3 * n,)), pltpu.SemaphoreType.DMA((3 * n,))],
                  name=name)(*arrays)


def _add_core_blocks(gs, recvs, core, name):
    n = len(gs)

    def body(core_ref, *refs):
        for a in range(n):
            refs[2 * n + a][...] = (refs[a][...].astype(F32) + refs[n + a][...].astype(F32)).astype(refs[2 * n + a].dtype)

    in_specs = [pl.BlockSpec((None, None) + g.shape[1:], lambda k, core_ref: (k, core_ref[0], 0, 0)) for g in gs]
    in_specs += [pl.BlockSpec((None,) + r.shape[1:], lambda k, core_ref: (k, 0, 0)) for r in recvs]
    grid_spec = pltpu.PrefetchScalarGridSpec(
        num_scalar_prefetch=1, grid=(4,), in_specs=in_specs,
        out_specs=[pl.BlockSpec((None,) + r.shape[1:], lambda k, core_ref: (k, 0, 0)) for r in recvs])
    return _pcall(body, grid_spec=grid_spec, out_shape=[SDS(r.shape, r.dtype) for r in recvs], name=name,
                  compiler_params=_params("arbitrary"))(core, *[g.reshape((4, 2) + g.shape[1:]) for g in gs], *recvs)


def _add_chip_blocks(parts, recvs, chip, name):
    n = len(parts)
    n_row_blocks = next(nb for nb in (4, 2, 1) if all(p.shape[1] % (nb * PACK_ROW_ALIGN) == 0 for p in parts))

    def body(chip_ref, *refs):
        for a in range(n):
            r_ref = refs[n + a]
            refs[2 * n + a][...] = ((refs[a][...].astype(F32) + r_ref[0].astype(F32)) + r_ref[1].astype(F32)
                                    ) + r_ref[2].astype(F32)

    def rows(p):
        return p.shape[1] // n_row_blocks
    in_specs = [pl.BlockSpec((None, rows(p), p.shape[2]), lambda i, chip_ref: (chip_ref[0], i, 0)) for p in parts]
    in_specs += [pl.BlockSpec((3, rows(p), p.shape[2]), lambda i, chip_ref: (0, i, 0)) for p in parts]
    grid_spec = pltpu.PrefetchScalarGridSpec(
        num_scalar_prefetch=1, grid=(n_row_blocks,), in_specs=in_specs,
        out_specs=[pl.BlockSpec((rows(p), p.shape[2]), lambda i, chip_ref: (i, 0)) for p in parts])
    return _pcall(body, grid_spec=grid_spec, out_shape=[SDS(p.shape[1:], F32) for p in parts], name=name,
                  compiler_params=_params("arbitrary"))(chip, *parts, *recvs)


def _reduce_scatter(gs, tag):
    core = lax.axis_index("c").astype(jnp.int32).reshape(1)
    chip = (2 * lax.axis_index("x") + lax.axis_index("y")).astype(jnp.int32).reshape(1)
    parts = _add_core_blocks(gs, _swap_with_sibling(gs, tag + "_rs_sibling"), core, tag + "_rs_add_core")
    return _add_chip_blocks(parts, _swap_with_chips(parts, tag + "_rs_chips"), chip, tag + "_rs_add_chips")


def _slab_rows(n):
    return -(-n // (PACK_COLS * PACK_ROW_ALIGN)) * PACK_ROW_ALIGN


def _pack(arrays):
    def slab(a):
        flat = a.reshape(-1)
        rows = _slab_rows(flat.shape[0])
        return jnp.pad(flat, (0, rows * PACK_COLS - flat.shape[0])).reshape(rows, PACK_COLS)
    return jnp.concatenate([slab(a) for a in arrays], axis=0)


def _unpack(slab, shapes, lead=()):
    out, row = [], 0
    for shape in shapes:
        n = math.prod(shape)
        rows = _slab_rows(n)
        part = lax.slice_in_dim(slab, row, row + rows, axis=len(lead)).reshape(lead + (rows * PACK_COLS,))
        out.append(lax.slice_in_dim(part, 0, n, axis=len(lead)).reshape(lead + tuple(shape)))
        row += rows
    return out


def _join_shards(t, axis):
    t = jnp.moveaxis(t, 0, axis)
    return t.reshape(t.shape[:axis] + (t.shape[axis] * t.shape[axis + 1],) + t.shape[axis + 2:])


def _split_shards(full, axis):
    s = full.shape
    t = full.reshape(s[:axis] + (N_DEV, s[axis] // N_DEV) + s[axis + 1:])
    return jnp.moveaxis(t, axis, 0)


def _weight(w, name, layer):
    if name in DIRECT:
        return _Sharded(w[name], layer, SHARD_AXIS[name] - 1)
    return w[name][layer]


def _weight_grad(lhs, rhs, name, tag):
    axis = SHARD_AXIS[name] - 1
    if name in DIRECT:
        return _mm(lhs, rhs, "tn", [BF16], name=tag, shard_out=axis)
    return _split_shards(_mm(lhs, rhs, "tn", [F32], name=tag), axis).astype(BF16)


def _attention_fwd(u, h, w, j, tag):
    t, d = u.shape
    n_heads = d // HEAD_DIM
    n_kv = n_heads // Q_PER_KV
    qkv = _mm(u, _weight(w, "a_wqkv", j), "nn", [F32], name=tag + "_qkv")
    q = qkv[:, :d].reshape(t, n_kv, Q_PER_KV, HEAD_DIM).transpose(1, 2, 0, 3)
    k = qkv[:, d:d + n_kv * HEAD_DIM].reshape(t, n_kv, HEAD_DIM).transpose(1, 0, 2)
    v = qkv[:, d + n_kv * HEAD_DIM:].reshape(t, n_kv, HEAD_DIM).transpose(1, 0, 2)
    gq, gk, sinks = w["a_q_norm_g"][j][None], w["a_k_norm_g"][j][None], w["a_sinks"][j]
    o = _attn_fwd(q, k, v, gq, gk, sinks, tag + "_attn")
    o2 = o.transpose(2, 0, 1, 3).reshape(t, d)
    h1 = _mm(o2, _weight(w, "a_wo", j), "nn", [F32], name=tag + "_wo", epi=lambda acc, h: acc + h, rows=[h])
    return h1, (q, k, v, o2)


def _attention_bwd(ctx, u, dh1, dh1b, w, j, tag, grads, big):
    q, k, v, o2 = ctx
    t, d = u.shape
    n_kv = q.shape[0]
    gq, gk, sinks = w["a_q_norm_g"][j][None], w["a_k_norm_g"][j][None], w["a_sinks"][j]
    big.append(("a_wo", j, _weight_grad(o2, dh1b, "a_wo", tag + "_dwo")))
    do2 = _mm(dh1b, _weight(w, "a_wo", j), "nt", [F32], name=tag + "_do")
    do = do2.reshape(t, n_kv, Q_PER_KV, HEAD_DIM).transpose(1, 2, 0, 3)
    dq, dkp, dkc, dvp, dvc, dgq, dgk, dsink = _attn_bwd(q, k, v, do, gq, gk, sinks, tag + "_attn_bwd")

    def fold(prev_part, cur_part):
        shifted = jnp.concatenate([prev_part[:, ATTN_BLOCK:], jnp.zeros_like(prev_part[:, :ATTN_BLOCK])], axis=1)
        both = _add(shifted.reshape(n_kv * t, HEAD_DIM), cur_part.reshape(n_kv * t, HEAD_DIM), tag + "_fold")
        return both.reshape(n_kv, t, HEAD_DIM).transpose(1, 0, 2).reshape(t, n_kv * HEAD_DIM)
    dqkv = jnp.concatenate([dq.transpose(2, 0, 1, 3).reshape(t, d), fold(dkp, dkc), fold(dvp, dvc)], axis=1).astype(BF16)
    grads["a_q_norm_g"][j] = dgq[0]
    grads["a_k_norm_g"][j] = dgk[0]
    grads["a_sinks"][j] = dsink[:, :, 0].reshape(-1)
    big.append(("a_wqkv", j, _weight_grad(u, dqkv, "a_wqkv", tag + "_dwqkv")))
    return _mm(dqkv, _weight(w, "a_wqkv", j), "nt", [F32], name=tag + "_du")


def _conformer_fwd(u, h, w, j, tag):
    t, d = u.shape
    a = _mm(u, _weight(w, "b_w_pw1", j), "nn", [F32], name=tag + "_pw1", epi=lambda acc, b: acc + b, pars=[w["b_b_pw1"][j][None]])
    y = _rowwise(lambda val, gate: val * _sigmoid(gate), [(a, d, 0), (a, d, 1)], [], [(d, F32)], name=tag + "_glu")[0]
    y2 = _dwconv_fwd(y, w["b_w_dw"][j], w["b_b_dw"][j][None], False, tag + "_dw")[0]

    def ln_silu(y2, g, b):
        mu = _rowmean(y2)
        yc = y2 - mu
        y3 = yc * lax.rsqrt(_rowmean(yc * yc) + EPS) * g + b
        return y3 * _sigmoid(y3)
    y4 = _rowwise(ln_silu, [y2], [w["b_ln_g"][j][None], w["b_ln_b"][j][None]], [(d, BF16)], name=tag + "_ln")[0]
    h1 = _mm(y4, _weight(w, "b_w_pw2", j), "nn", [F32], name=tag + "_pw2", epi=lambda acc, h, b: acc + h + b, rows=[h],
             pars=[w["b_b_pw2"][j][None]])
    return h1, (a, y, y2, y4)


def _conformer_bwd(ctx, u, dh1, dh1b, dh1_colsum, w, j, tag, grads, big):
    a, y, y2, y4 = ctx
    t, d = u.shape
    big.append(("b_w_pw2", j, _weight_grad(y4, dh1b, "b_w_pw2", tag + "_dwpw2")))
    grads["b_b_pw2"][j] = dh1_colsum[0]
    dy4 = _mm(dh1b, _weight(w, "b_w_pw2", j), "nt", [F32], name=tag + "_dy4")

    def ln_silu_bwd(dy4, y2, g, b):
        mu = _rowmean(y2)
        yc = y2 - mu
        rstd = lax.rsqrt(_rowmean(yc * yc) + EPS)
        xh = yc * rstd
        dy3 = dy4 * _dsilu(xh * g + b)
        dxh = dy3 * g
        return rstd * (dxh - _rowmean(dxh) - xh * _rowmean(dxh * xh)), _colsum(dy3 * xh), _colsum(dy3)
    dy2, dlg, dlb = _rowwise(ln_silu_bwd, [dy4, y2], [w["b_ln_g"][j][None], w["b_ln_b"][j][None]], [(d, F32)], [d, d],
                             name=tag + "_ln_bwd")
    grads["b_ln_g"][j], grads["b_ln_b"][j] = dlg[0], dlb[0]
    dy, dw_dw, db_dw = _dwconv_bwd(y, dy2, w["b_w_dw"][j], tag + "_dw_bwd")
    grads["b_w_dw"][j], grads["b_b_dw"][j] = dw_dw, db_dw[0]

    def glu_bwd(dy, val, gate):
        s = _sigmoid(gate)
        dval, dgate = dy * s, dy * val * s * (1.0 - s)
        return dval, dgate, _colsum(dval), _colsum(dgate)
    dval, dgate, dbv, dbg = _rowwise(glu_bwd, [dy, (a, d, 0), (a, d, 1)], [], [(d, BF16), (d, BF16)], [d, d],
                                     name=tag + "_glu_bwd")
    da = jnp.concatenate([dval, dgate], axis=1)
    grads["b_b_pw1"][j] = jnp.concatenate([dbv[0], dbg[0]])
    big.append(("b_w_pw1", j, _weight_grad(u, da, "b_w_pw1", tag + "_dwpw1")))
    return _mm(da, _weight(w, "b_w_pw1", j), "nt", [F32], name=tag + "_du")


def _ssd_layouts(v, n_groups):
    t = v.shape[0]
    vg = v[:, :n_groups * SSM_R].reshape(t, n_groups, SSM_R).transpose(1, 0, 2)
    return vg, vg.reshape(n_groups, t // SSM_L, SSM_L, SSM_R).transpose(0, 1, 3, 2)


def _from_group_layout(vg, width):
    n_groups, t, _ = vg.shape
    v = vg.transpose(1, 0, 2).reshape(t, n_groups * SSM_R)
    return jnp.pad(v, ((0, 0), (0, width - n_groups * SSM_R)))


def _pad_lanes(v, width):
    return jnp.pad(v, [(0, 0)] * (v.ndim - 1) + [(0, width - v.shape[-1])])


def _mamba_weights(w, j, d_inner, conv_c):
    w_in = w["c_w_in"][j]
    n_heads = w_in.shape[1] - d_inner - conv_c
    return (w_in[:, :d_inner], w_in[:, d_inner:d_inner + conv_c], _pad_lanes(w_in[:, d_inner + conv_c:], 128), n_heads)


def _mamba_fwd(u, h, w, j, tag):
    t, d = u.shape
    d_inner = w["c_D"].shape[1] * SSM_P
    conv_c = w["c_w_conv"][j].shape[1]
    w_z, w_xbc, w_dt, n_heads = _mamba_weights(w, j, d_inner, conv_c)
    n_groups = n_heads // SSM_R
    z = _mm(u, w_z, "nn", [F32], name=tag + "_in_z")
    xbc_raw = _mm(u, w_xbc, "nn", [F32], name=tag + "_in_xbc")
    dt_raw = _mm(u, w_dt, "nn", [F32], name=tag + "_in_dt")
    pre, xbc = _dwconv_fwd(xbc_raw, w["c_w_conv"][j], w["c_b_conv"][j][None], True, tag + "_conv")
    bias, a_log = _pad_lanes(w["c_dt_bias"][j][None], 128), _pad_lanes(w["c_A_log"][j][None], 128)
    dt, cs = _dt_fwd(dt_raw, bias, a_log, tag + "_dt")
    dt_g, _ = _ssd_layouts(dt, n_groups)
    cs_g, cs_row = _ssd_layouts(cs, n_groups)
    y, prev = _ssd_fwd(xbc, dt_g, cs_g, cs_row, d_inner, tag + "_ssd")
    d_cols = jnp.repeat(w["c_D"][j], SSM_P)[None]
    g = w["c_norm_g"][j][None]

    def gated_norm(y, xs, z, d_cols, g):
        yg = (y + xs * d_cols) * (z * _sigmoid(z))
        return yg * lax.rsqrt(_rowmean(yg * yg) + EPS) * g
    yn = _rowwise(gated_norm, [y, (xbc, d_inner, 0), z], [d_cols, g], [(d_inner, BF16)], name=tag + "_gnorm")[0]
    h1 = _mm(yn, _weight(w, "c_w_out", j), "nn", [F32], name=tag + "_out", epi=lambda acc, h: acc + h, rows=[h])
    return h1, (z, xbc_raw, dt_raw, pre, xbc, dt, dt_g, cs_g, cs_row, y, prev, yn)


def _mamba_bwd(ctx, u, dh1, dh1b, w, j, tag, grads, big):
    z, xbc_raw, dt_raw, pre, xbc, dt, dt_g, cs_g, cs_row, y, prev, yn = ctx
    t, d = u.shape
    d_inner = w["c_D"].shape[1] * SSM_P
    conv_c = w["c_w_conv"][j].shape[1]
    w_z, w_xbc, w_dt, n_heads = _mamba_weights(w, j, d_inner, conv_c)
    n_groups = n_heads // SSM_R
    big.append(("c_w_out", j, _weight_grad(yn, dh1b, "c_w_out", tag + "_dwout")))
    dyn = _mm(dh1b, _weight(w, "c_w_out", j), "nt", [F32], name=tag + "_dyn")
    d_cols = jnp.repeat(w["c_D"][j], SSM_P)[None]
    g = w["c_norm_g"][j][None]

    def gated_norm_bwd(dyn, y, xs, z, d_cols, g):
        sg = _sigmoid(z)
        yt = y + xs * d_cols
        yg = yt * (z * sg)
        rstd = lax.rsqrt(_rowmean(yg * yg) + EPS)
        xh = yg * rstd
        dxh = dyn * g
        dyg = rstd * (dxh - xh * _rowmean(dxh * xh))
        dyt = dyg * (z * sg)
        return dyt, dyg * yt * (sg * (1.0 + z * (1.0 - sg))), dyt * d_cols, _colsum(dyn * xh), _colsum(dyt * xs)
    dyt, dz, dxs_skip, dg, dd_cols = _rowwise(
        gated_norm_bwd, [dyn, y, (xbc, d_inner, 0), z], [d_cols, g], [(d_inner, F32), (d_inner, BF16), (d_inner, F32)],
        [d_inner, d_inner], name=tag + "_gnorm_bwd")
    grads["c_norm_g"][j] = dg[0]
    grads["c_D"][j] = dd_cols.reshape(n_heads, SSM_P).sum(axis=1)
    dx, db, dc, ddtx_g, dda_g = _ssd_bwd(xbc, dt_g, cs_g, cs_row, prev, dyt, d_inner, tag + "_ssd_bwd")
    dxs = _add(dx, dxs_skip, tag + "_dxs")
    dpost = jnp.concatenate([dxs, db, dc], axis=1)
    dpre = _rowwise(lambda dpost, pre: dpost * _dsilu(pre), [dpost, pre], [], [(conv_c, F32)], name=tag + "_silu_bwd")[0]
    dxbc, dw_conv, db_conv = _dwconv_bwd(xbc_raw, dpre, w["c_w_conv"][j], tag + "_conv_bwd")
    grads["c_w_conv"][j], grads["c_b_conv"][j] = dw_conv, db_conv[0]
    bias, a_log = _pad_lanes(w["c_dt_bias"][j][None], 128), _pad_lanes(w["c_A_log"][j][None], 128)
    ddt_raw, dbias, da_log = _dt_bwd(_from_group_layout(ddtx_g, 128), _from_group_layout(dda_g, 128), dt_raw, dt, bias,
                                     a_log, tag + "_dt_bwd")
    grads["c_dt_bias"][j], grads["c_A_log"][j] = dbias[0, :n_heads], da_log[0, :n_heads]
    dxbc_b = dxbc.astype(BF16)
    dw_in = jnp.concatenate(
        [_mm(u, dz, "tn", [F32], name=tag + "_dwin_z"), _mm(u, dxbc_b, "tn", [F32], name=tag + "_dwin_xbc"),
         _mm(u, ddt_raw, "tn", [F32], name=tag + "_dwin_dt")[:, :n_heads]], axis=1)
    big.append(("c_w_in", j, _split_shards(dw_in, SHARD_AXIS["c_w_in"] - 1).astype(BF16)))
    du = _mm(dz, w_z, "nt", [F32], name=tag + "_du_z")
    du = _mm(dxbc_b, w_xbc, "nt", [F32], name=tag + "_du_xbc", epi=lambda acc, r: acc + r, rows=[du])
    return _mm(ddt_raw, w_dt, "nt", [F32], name=tag + "_du_dt", epi=lambda acc, r: acc + r, rows=[du])


def _local_step(x, p, tgt, w, reduce_fn):
    depth = p.shape[0]
    grads = {name: {} for name in REPLICATED + tuple(n for n, _ in SMALL_SHARDED)}
    reduced = {name: {} for name, _ in BIG_SHARDED}
    saved = []
    h = x
    for i in range(depth):
        kind, j, tag = i % N_MIXERS, i // N_MIXERS, f"l{i}"
        u = _rms_fwd(h, w["mix_norm_g"][i][None], tag + "_rms_mix")
        fwd = (_attention_fwd, _conformer_fwd, _mamba_fwd)[kind]
        h1, ctx = fwd(u, h, w, j, tag)
        u2 = _rms_fwd(h1, w["mlp_norm_g"][i][None], tag + "_rms_mlp")
        a, r = _mm(u2, _weight(w, "m_w1", i), "nn", [F32, BF16], name=tag + "_w1",
                   epi=lambda acc: (acc, jnp.square(jnp.maximum(acc, 0.0))))
        h2 = _mm(r, _weight(w, "m_w2", i), "nn", [F32], name=tag + "_w2", epi=lambda acc, h: acc + h, rows=[h1])
        u3 = _rms_fwd(h2, w["ple_norm_g"][i][None], tag + "_rms_ple")
        gl = _mm(u3, _weight(w, "ple_w_gate", i), "nn", [F32], name=tag + "_gate")
        pp = _mm(p[i], _weight(w, "ple_w_proj", i), "nn", [F32], name=tag + "_proj")
        h3 = _ple_fwd(h2, gl, pp, tag + "_ple")
        saved.append((h, u, ctx, h1, u2, a, r, h2, u3, gl, pp))
        h = h3

    dh, sq_err = _loss_and_grad(h, tgt, "loss")
    for i in reversed(range(depth)):
        kind, j, tag = i % N_MIXERS, i // N_MIXERS, f"l{i}"
        h0, u, ctx, h1, u2, a, r, h2, u3, gl, pp = saved[i]
        big = []
        dgl, dpp = _ple_bwd(dh, gl, pp, tag + "_ple_bwd")
        big.append(("ple_w_proj", i, _weight_grad(p[i], dpp, "ple_w_proj", tag + "_dwproj")))
        big.append(("ple_w_gate", i, _weight_grad(u3, dgl, "ple_w_gate", tag + "_dwgate")))
        du3 = _mm(dgl, _weight(w, "ple_w_gate", i), "nt", [F32], name=tag + "_du3")
        dh2, dh2b, dg, _ = _rms_bwd(du3, h2, w["ple_norm_g"][i][None], dh, tag + "_rms_ple_bwd")
        grads["ple_norm_g"][i] = dg[0]
        big.append(("m_w2", i, _weight_grad(r, dh2b, "m_w2", tag + "_dw2")))
        da = _mm(dh2b, _weight(w, "m_w2", i), "nt", [BF16], name=tag + "_da",
                 epi=lambda acc, a: acc * (2.0 * jnp.maximum(a, 0.0)), rows=[a])
        big.append(("m_w1", i, _weight_grad(u2, da, "m_w1", tag + "_dw1")))
        du2 = _mm(da, _weight(w, "m_w1", i), "nt", [F32], name=tag + "_du2")
        dh1, dh1b, dg, dh1_colsum = _rms_bwd(du2, h1, w["mlp_norm_g"][i][None], dh2, tag + "_rms_mlp_bwd")
        grads["mlp_norm_g"][i] = dg[0]
        if kind == 0:
            du = _attention_bwd(ctx, u, dh1, dh1b, w, j, tag, grads, big)
        elif kind == 1:
            du = _conformer_bwd(ctx, u, dh1, dh1b, dh1_colsum, w, j, tag, grads, big)
        else:
            du = _mamba_bwd(ctx, u, dh1, dh1b, w, j, tag, grads, big)
        dh, _, dg, _ = _rms_bwd(du, h0, w["mix_norm_g"][i][None], dh1, tag + "_rms_mix_bwd")
        grads["mix_norm_g"][i] = dg[0]
        for (name, layer, _), g in zip(big, reduce_fn(tag, [(name, g) for name, _, g in big])):
            reduced[name][layer] = g

    def stack(d):
        return {name: jnp.stack([g[k] for k in sorted(g)]) for name, g in d.items()}
    return sq_err, dh, stack(reduced), stack(grads)


def kernel(x, p, mix_norm_g, mlp_norm_g, ple_norm_g, a_wqkv, a_q_norm_g, a_k_norm_g, a_sinks, a_wo, b_w_pw1, b_b_pw1, b_w_dw, b_b_dw, b_ln_g, b_ln_b, b_w_pw2, b_b_pw2, c_w_in, c_w_conv, c_b_conv, c_dt_bias, c_A_log, c_D, c_norm_g, c_w_out, m_w1, m_w2, ple_w_proj, ple_w_gate, loss_target, m_mix_norm_g, m_mlp_norm_g, m_ple_norm_g, m_a_wqkv, m_a_q_norm_g, m_a_k_norm_g, m_a_sinks, m_a_wo, m_b_w_pw1, m_b_b_pw1, m_b_w_dw, m_b_b_dw, m_b_ln_g, m_b_ln_b, m_b_w_pw2, m_b_b_pw2, m_c_w_in, m_c_w_conv, m_c_b_conv, m_c_dt_bias, m_c_A_log, m_c_D, m_c_norm_g, m_c_w_out, m_m_w1, m_m_w2, m_ple_w_proj, m_ple_w_gate, v_mix_norm_g, v_mlp_norm_g, v_ple_norm_g, v_a_wqkv, v_a_q_norm_g, v_a_k_norm_g, v_a_sinks, v_a_wo, v_b_w_pw1, v_b_b_pw1, v_b_w_dw, v_b_b_dw, v_b_ln_g, v_b_ln_b, v_b_w_pw2, v_b_b_pw2, v_c_w_in, v_c_w_conv, v_c_b_conv, v_c_dt_bias, v_c_A_log, v_c_D, v_c_norm_g, v_c_w_out, v_m_w1, v_m_w2, v_ple_w_proj, v_ple_w_gate):
    args = dict(locals())
    wl = {n: args[n] for n in WEIGHTS}
    ml = {n: args["m_" + n] for n in WEIGHTS}
    vl = {n: args["v_" + n] for n in WEIGHTS}
    d_model = x.shape[-1]
    small_sharded = [n for n, _ in SMALL_SHARDED]

    whole = {n: wl[n] for n in REPLICATED}
    gathered = _all_gather([wl[n].astype(BF16) for n, _ in BIG_SHARDED], "gather_weights")
    for (n, axis), g in zip(BIG_SHARDED, gathered):
        whole[n] = g if n in DIRECT else _join_shards(g, axis)
    small = _all_gather([_pack([wl[n] for n in small_sharded])], "gather_small")[0]
    for (n, axis), t in zip(SMALL_SHARDED, _unpack(small, [wl[n].shape for n in small_sharded], (N_DEV,))):
        whole[n] = _join_shards(t, axis)

    def reduce_fn(tag, items):
        return _reduce_scatter([g for _, g in items], tag)
    sq_err, grad_x, g_big, g_small = _local_step(x[0], p[:, 0], loss_target[0], whole, reduce_fn)
    loss = lax.psum(0.5 * jnp.sum(sq_err) / d_model, ("x", "y", "c"))

    small_names = list(REPLICATED) + small_sharded
    summed = _sum_blocks(_all_gather([_pack([g_small[n] for n in small_names])], "gather_small_grads")[0],
                         "sum_small_grads")
    g_small = dict(zip(small_names, _unpack(summed, [g_small[n].shape for n in small_names])))
    me = 4 * lax.axis_index("x") + 2 * lax.axis_index("y") + lax.axis_index("c")
    for n, axis in SMALL_SHARDED:
        size = wl[n].shape[axis]
        g_small[n] = lax.dynamic_slice_in_dim(g_small[n], me * size, size, axis)

    out = {}
    for n, _ in BIG_SHARDED:
        shape = wl[n].shape
        two_d = (shape[0] * shape[1], shape[2])
        delta, new_m, new_v = _adamw(wl[n].reshape(two_d), g_big[n].reshape(two_d), ml[n].reshape(two_d),
                                     vl[n].reshape(two_d), "adamw_" + n)
        out.update({"grad_" + n: g_big[n], "delta_" + n: delta.reshape(shape), "new_m_" + n: new_m.reshape(shape),
                    "new_v_" + n: new_v.reshape(shape)})
    g_slab = _pack([g_small[n] for n in small_names])
    delta, new_m, new_v = _adamw(_pack([wl[n] for n in small_names]), g_slab, _pack([ml[n] for n in small_names]),
                                 _pack([vl[n] for n in small_names]), "adamw_small")
    for kind, slab in (("delta", delta), ("new_m", new_m), ("new_v", new_v)):
        out.update({kind + "_" + n: a for n, a in zip(small_names, _unpack(slab, [wl[n].shape for n in small_names]))})
    out.update({"grad_" + n: g_small[n] for n in small_names})
    return (loss, grad_x[None], *[out[k + "_" + n] for k in ("grad", "delta", "new_m", "new_v") for n in WEIGHTS])
```

```python
import math
from typing import Any, NamedTuple

import jax
import jax.numpy as jnp
from jax import lax
from jax.experimental import pallas as pl
from jax.experimental.pallas import tpu as pltpu

F32 = jnp.float32
BF16 = jnp.bfloat16
SDS = jax.ShapeDtypeStruct

EPS = 1e-6
N_MIXERS = 3
HEAD_DIM = 64
Q_PER_KV = 4
ATTN_BLOCK = 128
SSM_P = 64
SSM_R = 4
SSM_N = 128
SSM_L = 128
ADAM_LR, ADAM_B1, ADAM_B2, ADAM_EPS, ADAM_WD, ADAM_STEP = 0.001, 0.9, 0.999, 1e-08, 0.01, 10
N_DEV = 8
PACK_COLS = 1024
PACK_ROW_ALIGN = 16
VMEM_LIMIT = 56 * 1024 * 1024
MM_VMEM_BUDGET = 40 * 1024 * 1024
ROW_VMEM_BUDGET = 24 * 1024 * 1024
CONV_PAD = 32
CONV_ROWS = 64
NEG_BIG = -1e30

BIG_SHARDED = (("a_wqkv", 2), ("a_wo", 1), ("b_w_pw1", 2), ("b_w_pw2", 1), ("c_w_in", 2), ("c_w_out", 1),
               ("m_w1", 2), ("m_w2", 1), ("ple_w_proj", 2), ("ple_w_gate", 1))
SMALL_SHARDED = (("b_w_dw", 2), ("c_w_conv", 2), ("c_b_conv", 1), ("c_norm_g", 1))
REPLICATED = ("mix_norm_g", "mlp_norm_g", "ple_norm_g", "a_q_norm_g", "a_k_norm_g", "a_sinks", "b_b_pw1", "b_b_dw",
              "b_ln_g", "b_ln_b", "b_b_pw2", "c_dt_bias", "c_A_log", "c_D")
DIRECT = frozenset(("a_wo", "b_w_pw1", "b_w_pw2", "c_w_out", "m_w1", "m_w2", "ple_w_proj", "ple_w_gate"))
SHARD_AXIS = dict(BIG_SHARDED + SMALL_SHARDED)
MIXER_WEIGHTS = (("a_wqkv", "a_wo"), ("b_w_pw1", "b_w_pw2"), ("c_w_in", "c_w_out"))
LAYER_WEIGHTS = ("m_w1", "m_w2", "ple_w_proj", "ple_w_gate")
WEIGHTS = ("mix_norm_g", "mlp_norm_g", "ple_norm_g", "a_wqkv", "a_q_norm_g", "a_k_norm_g", "a_sinks", "a_wo",
           "b_w_pw1", "b_b_pw1", "b_w_dw", "b_b_dw", "b_ln_g", "b_ln_b", "b_w_pw2", "b_b_pw2", "c_w_in", "c_w_conv",
           "c_b_conv", "c_dt_bias", "c_A_log", "c_D", "c_norm_g", "c_w_out", "m_w1", "m_w2", "ple_w_proj",
           "ple_w_gate")


def _pcall(body, **kw):
    return pl.pallas_call(body, **kw)


def _params(*sem):
    return pltpu.CompilerParams(dimension_semantics=sem, vmem_limit_bytes=VMEM_LIMIT)


def _sigmoid(x):
    return 1.0 / (1.0 + jnp.exp(-x))


def _dsilu(x):
    s = _sigmoid(x)
    return s * (1.0 + x * (1.0 - s))


def _colsum(x):
    return jnp.sum(x, axis=0, keepdims=True)


def _rowmean(x):
    return jnp.mean(x, axis=-1, keepdims=True)


def _rowwise(fn, rows, pars, row_outs, acc_outs=(), *, name, tm=None):
    rows = [r if isinstance(r, tuple) else (r, r.shape[1], 0) for r in rows]
    t = rows[0][0].shape[0]
    per_row = sum(w * a.dtype.itemsize for a, w, _ in rows) + sum(w * jnp.dtype(d).itemsize for w, d in row_outs)
    if tm is None:
        tm = 1024
        while tm > 8 and (t % tm or 2 * tm * per_row > ROW_VMEM_BUDGET):
            tm //= 2
    assert t % tm == 0, (name, t, tm)
    n_in, n_row = len(rows) + len(pars), len(row_outs)

    def body(*refs):
        outs = fn(*[r[...] for r in refs[:n_in]])
        outs = outs if isinstance(outs, tuple) else (outs,)
        for ref, o in zip(refs[n_in:n_in + n_row], outs[:n_row]):
            ref[...] = o.astype(ref.dtype)
        acc_refs = refs[n_in + n_row:]
        if acc_refs:
            @pl.when(pl.program_id(0) == 0)
            def _():
                for ref in acc_refs:
                    ref[...] = jnp.zeros_like(ref)
            for ref, o in zip(acc_refs, outs[n_row:]):
                ref[...] += o

    in_specs = [pl.BlockSpec((tm, w), lambda i, cb=cb: (i, cb)) for _, w, cb in rows]
    in_specs += [pl.BlockSpec(p.shape, lambda i: (0, 0)) for p in pars]
    out_specs = [pl.BlockSpec((tm, w), lambda i: (i, 0)) for w, _ in row_outs]
    out_specs += [pl.BlockSpec((1, w), lambda i: (0, 0)) for w in acc_outs]
    out_shape = [SDS((t, w), d) for w, d in row_outs] + [SDS((1, w), F32) for w in acc_outs]
    return _pcall(body, grid=(t // tm,), in_specs=in_specs, out_specs=out_specs, out_shape=out_shape, name=name,
                  compiler_params=_params("arbitrary"))(*[a for a, _, _ in rows], *pars)


_DOT_DIMS = {"nn": (((1,), (0,)), ((), ())), "nt": (((1,), (1,)), ((), ())), "tn": (((0,), (0,)), ((), ()))}


class _Sharded(NamedTuple):
    g: Any
    layer: int
    axis: int


def _mm(a, b, form, out_dtypes, *, name, epi=None, rows=(), pars=(), shard_out=None):
    sharded = isinstance(b, _Sharded)
    if sharded:
        wa, wb = b.g.shape[2:]
        b_shape = (N_DEV * wa, wb) if b.axis == 0 else (wa, N_DEV * wb)
        b_item, layer = b.g.dtype.itemsize, b.layer
    else:
        b_shape, b_item = b.shape, b.dtype.itemsize
    if form == "tn":
        k, m = a.shape
        n = b_shape[1]
    else:
        m, k = a.shape
        n = b_shape[0] if form == "nt" else b_shape[1]
    rows = [r if isinstance(r, tuple) else (r, 0) for r in rows]
    tn = next((c for c in (512, 256, 128) if n % c == 0), n)
    b_shards = tn // wb if sharded and b.axis == 1 and form == "nn" and wb < tn else 1
    out_rows, out_cols = (m // N_DEV if shard_out == 0 else None), (n // N_DEV if shard_out == 1 else None)
    if out_cols is not None and out_cols > tn:
        tn = next(c for c in (512, 256, 128) if out_cols % c == 0)

    def need(tm_):
        blk = tm_ * k * a.dtype.itemsize + tn * k * b_item
        blk += sum(tm_ * tn * jnp.dtype(d).itemsize for d in out_dtypes) + sum(tm_ * tn * r.dtype.itemsize for r, _ in rows)
        return 2 * blk

    tm = next((c for c in (1024, 512, 256, 128, 64, 32, 16) if m % c == 0 and need(c) <= MM_VMEM_BUDGET
               and (out_rows is None or out_rows % c == 0 or c % out_rows == 0)), None)
    assert tm is not None, (name, m, k, n)
    nx = len(rows) + len(pars)
    pieces = sharded and form == "nt" and b.axis == 1

    def body(*refs):
        av = refs[0][...].astype(BF16)
        if pieces:
            acc = jnp.zeros((tm, tn), F32)
            for d in range(N_DEV):
                acc += lax.dot_general(av[:, d * wb:(d + 1) * wb], refs[1][d], _DOT_DIMS["nt"], preferred_element_type=F32)
        else:
            bv = refs[1][...]
            if b_shards > 1:
                acc = jnp.concatenate([jnp.dot(av, bv[d], preferred_element_type=F32) for d in range(b_shards)], axis=1)
            else:
                if sharded and bv.ndim == 3:
                    bv = bv.reshape(bv.shape[0] * bv.shape[1], bv.shape[2])
                acc = lax.dot_general(av, bv.astype(BF16), _DOT_DIMS[form], preferred_element_type=F32)
        outs = epi(acc, *[r[...] for r in refs[2:2 + nx]]) if epi else acc
        outs = outs if isinstance(outs, tuple) else (outs,)
        for ref, o in zip(refs[2 + nx:], outs):
            if out_rows is not None and tm > out_rows:
                ref[...] = o.reshape(tm // out_rows, out_rows, tn).astype(ref.dtype)
            elif out_cols is not None and tn > out_cols:
                for d in range(tn // out_cols):
                    ref[d] = o[:, d * out_cols:(d + 1) * out_cols].astype(ref.dtype)
            else:
                ref[...] = o.astype(ref.dtype)

    a_spec = pl.BlockSpec((k, tm), lambda i, j: (0, i)) if form == "tn" else pl.BlockSpec((tm, k), lambda i, j: (i, 0))
    if not sharded:
        b_spec = pl.BlockSpec((tn, k), lambda i, j: (j, 0)) if form == "nt" else pl.BlockSpec((k, tn), lambda i, j: (0, j))
    elif form == "nn" and b.axis == 0:
        b_spec = pl.BlockSpec((N_DEV, None, wa, tn), lambda i, j: (0, layer, 0, j))
    elif form == "nn" and b_shards > 1:
        b_spec = pl.BlockSpec((b_shards, None, wa, wb), lambda i, j: (j, layer, 0, 0))
    elif form == "nn":
        per = wb // tn
        b_spec = pl.BlockSpec((None, None, wa, tn), lambda i, j: (j // per, layer, 0, j % per))
    elif form == "nt" and b.axis == 0 and tn >= wa:
        b_spec = pl.BlockSpec((tn // wa, None, wa, k), lambda i, j: (j, layer, 0, 0))
    elif form == "nt" and b.axis == 0:
        per = wa // tn
        b_spec = pl.BlockSpec((None, None, tn, k), lambda i, j: (j // per, layer, j % per, 0))
    else:
        assert form == "nt", (name, form)
        b_spec = pl.BlockSpec((N_DEV, None, tn, wb), lambda i, j: (0, layer, j, 0))
    in_specs = [a_spec, b_spec]
    in_specs += [pl.BlockSpec((tm, tn), lambda i, j, off=off: (i, j + off)) for _, off in rows]
    in_specs += [pl.BlockSpec((1, tn), lambda i, j: (0, j)) for _ in pars]
    if shard_out == 0 and tm > out_rows:
        out_specs = [pl.BlockSpec((tm // out_rows, out_rows, tn), lambda i, j: (i, 0, j)) for _ in out_dtypes]
        out_shape = [SDS((N_DEV, out_rows, n), d) for d in out_dtypes]
    elif shard_out == 0:
        per_m = out_rows // tm
        out_specs = [pl.BlockSpec((None, tm, tn), lambda i, j: (i // per_m, i % per_m, j)) for _ in out_dtypes]
        out_shape = [SDS((N_DEV, out_rows, n), d) for d in out_dtypes]
    elif shard_out == 1 and tn > out_cols:
        out_specs = [pl.BlockSpec((tn // out_cols, tm, out_cols), lambda i, j: (j, i, 0)) for _ in out_dtypes]
        out_shape = [SDS((N_DEV, m, out_cols), d) for d in out_dtypes]
    elif shard_out == 1:
        per_n = out_cols // tn
        out_specs = [pl.BlockSpec((None, tm, tn), lambda i, j: (j // per_n, i, j % per_n)) for _ in out_dtypes]
        out_shape = [SDS((N_DEV, m, out_cols), d) for d in out_dtypes]
    else:
        out_specs = [pl.BlockSpec((tm, tn), lambda i, j: (i, j)) for _ in out_dtypes]
        out_shape = [SDS((m, n), d) for d in out_dtypes]
    res = _pcall(body, grid=(m // tm, n // tn), in_specs=in_specs, out_specs=out_specs, out_shape=out_shape, name=name,
                 compiler_params=_params("arbitrary", "arbitrary"))(a, b.g if sharded else b, *[r for r, _ in rows], *pars)
    return res[0] if len(out_dtypes) == 1 else res


def _rms_fwd(x, g, name, after=None):
    def fn(x, g, *_):
        return x * lax.rsqrt(_rowmean(x * x) + EPS) * g
    return _rowwise(fn, [x], [g] + ([] if after is None else [after]), [(x.shape[1], BF16)], name=name)[0]


def _rms_bwd(dy, x, g, dres, name):
    def fn(dy, x, dres, g):
        rstd = lax.rsqrt(_rowmean(x * x) + EPS)
        xh = x * rstd
        dxh = dy * g
        tot = dres + rstd * (dxh - xh * _rowmean(dxh * xh))
        return tot, tot, _colsum(dy * xh), _colsum(tot)
    d = x.shape[1]
    return _rowwise(fn, [dy, x, dres], [g], [(d, F32), (d, BF16)], [d, d], name=name)


def _ple_fwd(h, gl, pp, name):
    return _rowwise(lambda h, gl, pp: h + _sigmoid(gl) * pp, [h, gl, pp], [], [(h.shape[1], F32)], name=name)[0]


def _ple_bwd(dh, gl, pp, name, after=None):
    def fn(dh, gl, pp, *_):
        gate = _sigmoid(gl)
        return dh * pp * gate * (1.0 - gate), dh * gate
    d = dh.shape[1]
    return _rowwise(fn, [dh, gl, pp], [] if after is None else [after], [(d, BF16), (d, BF16)], name=name)


def _loss_and_grad(h, tgt, name):
    d = h.shape[1]

    def fn(h, tgt):
        err = h - tgt
        return err * (1.0 / d), _colsum(err * err)
    return _rowwise(fn, [h, tgt], [], [(d, F32)], [d], name=name)


def _add(a, b, name):
    return _rowwise(lambda a, b: a + b, [a, b], [], [(a.shape[1], F32)], name=name)[0]


def _adamw(w, g, m, v, name):
    def fn(w, g, m, v):
        m = ADAM_B1 * m + (1.0 - ADAM_B1) * g
        v = ADAM_B2 * v + (1.0 - ADAM_B2) * (g * g)
        m_hat = m / (1.0 - ADAM_B1 ** ADAM_STEP)
        v_hat = v / (1.0 - ADAM_B2 ** ADAM_STEP)
        return -ADAM_LR * (m_hat / (jnp.sqrt(v_hat) + ADAM_EPS) + ADAM_WD * w), m, v
    c = w.shape[1]
    return _rowwise(fn, [w, g, m, v], [], [(c, F32)] * 3, name=name)


def _sum_blocks(x, name):
    n = x.shape[0]

    def body(x_ref, o_ref):
        acc = x_ref[0]
        for j in range(1, n):
            acc = acc + x_ref[j]
        o_ref[...] = acc
    return _pcall(body, out_shape=SDS(x.shape[1:], x.dtype), name=name, compiler_params=_params())(x)


def _shifted(chunk, off, rows):
    if off % 8 == 0:
        return chunk[off:off + rows]
    return pltpu.roll(chunk, chunk.shape[0] - off, 0)[:rows]


def _conv_cols(c):
    return 256 if c % 256 == 0 else 128


def _rows_at(i, tb):
    return pl.ds(i * tb, tb) if isinstance(i, int) else pl.ds(pl.multiple_of(i * tb, 8), tb)


def _chunk_before(ref, i, tb, pad):
    if isinstance(i, int) and i == 0:
        return jnp.concatenate([jnp.zeros((pad, ref.shape[1]), F32), ref[0:tb, :]], axis=0)
    if isinstance(i, int):
        return ref[i * tb - pad:(i + 1) * tb, :]
    return ref[pl.ds(pl.multiple_of(i * tb - pad, 8), tb + pad), :]


def _chunk_after(ref, i, tb, pad, last):
    if last:
        return jnp.concatenate([ref[i * tb:(i + 1) * tb, :], jnp.zeros((pad, ref.shape[1]), F32)], axis=0)
    if isinstance(i, int):
        return ref[i * tb:(i + 1) * tb + pad, :]
    return ref[pl.ds(pl.multiple_of(i * tb, 8), tb + pad), :]


def _dwconv_fwd(y, w, b, silu, name):
    t, c = y.shape
    taps = w.shape[0]
    cb, tb, pad = _conv_cols(c), CONV_ROWS, CONV_PAD

    def body(y_ref, w_ref, b_ref, *out_refs):
        def step(i, carry):
            chunk = _chunk_before(y_ref, i, tb, pad)
            acc = jnp.broadcast_to(b_ref[...], (tb, cb))
            for k in range(taps):
                acc = acc + _shifted(chunk, pad - (taps - 1) + k, tb) * w_ref[k:k + 1, :]
            rows = _rows_at(i, tb)
            out_refs[0][rows, :] = acc
            if silu:
                out_refs[1][rows, :] = acc * _sigmoid(acc)
            return carry
        step(0, 0)
        lax.fori_loop(1, t // tb, step, 0)

    n_out = 2 if silu else 1
    return _pcall(body, grid=(c // cb,),
                  in_specs=[pl.BlockSpec((t, cb), lambda j: (0, j)), pl.BlockSpec((taps, cb), lambda j: (0, j)),
                            pl.BlockSpec((1, cb), lambda j: (0, j))],
                  out_specs=[pl.BlockSpec((t, cb), lambda j: (0, j))] * n_out,
                  out_shape=[SDS((t, c), F32)] * n_out, name=name, compiler_params=_params("arbitrary"))(y, w, b)


def _dwconv_bwd(y, dout, w, name):
    t, c = y.shape
    taps = w.shape[0]
    cb, tb, pad = _conv_cols(c), CONV_ROWS, CONV_PAD
    taps_pad = -(-taps // 8) * 8
    n_steps = t // tb
    assert n_steps >= 2

    def body(y_ref, d_ref, w_ref, dy_ref, dw_ref, db_ref):
        dw_ref[...] = jnp.zeros_like(dw_ref)

        def step(i, db, last=False):
            ychunk = _chunk_before(y_ref, i, tb, pad)
            dchunk = _chunk_after(d_ref, i, tb, pad, last)
            d0 = dchunk[:tb]
            acc = jnp.zeros((tb, cb), F32)
            for k in range(taps):
                acc = acc + _shifted(dchunk, taps - 1 - k, tb) * w_ref[k:k + 1, :]
                dw_ref[k:k + 1, :] += _colsum(d0 * _shifted(ychunk, pad - (taps - 1) + k, tb))
            dy_ref[_rows_at(i, tb), :] = acc
            return db + _colsum(d0)
        db = step(0, jnp.zeros((1, cb), F32))
        db = lax.fori_loop(1, n_steps - 1, step, db)
        db_ref[...] = step(n_steps - 1, db, last=True)

    dy, dw, db = _pcall(
        body, grid=(c // cb,),
        in_specs=[pl.BlockSpec((t, cb), lambda j: (0, j)), pl.BlockSpec((t, cb), lambda j: (0, j)),
                  pl.BlockSpec((taps, cb), lambda j: (0, j))],
        out_specs=[pl.BlockSpec((t, cb), lambda j: (0, j)), pl.BlockSpec((taps_pad, cb), lambda j: (0, j)),
                   pl.BlockSpec((1, cb), lambda j: (0, j))],
        out_shape=[SDS((t, c), F32), SDS((taps_pad, c), F32), SDS((1, c), F32)], name=name,
        compiler_params=_params("arbitrary"))(y, dout, w)
    return dy, dw[:taps], db


def _head_norm(x, g):
    rstd = lax.rsqrt(_rowmean(x * x) + EPS)
    xh = x * rstd
    return xh * g, xh, rstd


def _head_norm_bwd(dy, xh, rstd, g):
    dxh = dy * g
    return rstd * (dxh - xh * _rowmean(dxh * xh)), _colsum(dy * xh)


def _attn_probs(q_ref, kp_ref, kc_ref, gq_ref, gk_ref, sinks_ref, n_heads):
    g, blk, hd = Q_PER_KV, ATTN_BLOCK, HEAD_DIM
    kv, n = pl.program_id(0), pl.program_id(1)
    q_raw = q_ref[...].reshape(g * blk, hd)
    k_raw = jnp.concatenate([kp_ref[...], kc_ref[...]], axis=0)
    qn, qh, q_rstd = _head_norm(q_raw, gq_ref[...])
    kn, kh, k_rstd = _head_norm(k_raw, gk_ref[...])
    s = lax.dot_general(qn.astype(BF16), kn.astype(BF16), _DOT_DIMS["nt"], preferred_element_type=F32) * hd ** -0.5
    row = lax.broadcasted_iota(jnp.int32, (g * blk, 1), 0)
    g_row = lax.shift_right_logical(row, 7)
    qi = lax.bitwise_and(row, blk - 1)
    kj = lax.broadcasted_iota(jnp.int32, (1, 2 * blk), 1)
    dist = qi - kj + blk
    valid = (dist >= 0) & (dist < blk) & (kj >= jnp.where(n > 0, 0, blk))
    head = (kv * g + g_row + 1).astype(F32)
    slope = jnp.exp(head * (-8.0 * math.log(2.0) / n_heads))
    s = jnp.where(valid, s - slope * dist.astype(F32), NEG_BIG)
    sink = jnp.zeros((g * blk, 1), F32)
    for gi in range(g):
        sink = jnp.where(g_row == gi, sinks_ref[kv * g + gi], sink)
    m = jnp.maximum(jnp.max(s, axis=1, keepdims=True), sink)
    e = jnp.exp(s - m)
    e_sink = jnp.exp(sink - m)
    inv = 1.0 / (jnp.sum(e, axis=1, keepdims=True) + e_sink)
    return (q_raw, qn, qh, q_rstd), (k_raw, kn, kh, k_rstd), e * inv, e_sink * inv


def _attn_specs(t):
    g, blk, hd = Q_PER_KV, ATTN_BLOCK, HEAD_DIM
    q_spec = pl.BlockSpec((None, g, blk, hd), lambda a, n: (a, 0, n, 0))
    prev_spec = pl.BlockSpec((None, blk, hd), lambda a, n: (a, jnp.maximum(n - 1, 0), 0))
    cur_spec = pl.BlockSpec((None, blk, hd), lambda a, n: (a, n, 0))
    gain_spec = pl.BlockSpec((1, hd), lambda a, n: (0, 0))
    sink_spec = pl.BlockSpec(memory_space=pltpu.SMEM)
    return q_spec, prev_spec, cur_spec, gain_spec, sink_spec


def _attn_fwd(q, k, v, gq, gk, sinks, name):
    n_kv, g, t, hd = q.shape
    blk = ATTN_BLOCK
    n_heads = n_kv * g

    def body(q_ref, kp_ref, kc_ref, vp_ref, vc_ref, gq_ref, gk_ref, sinks_ref, o_ref):
        _, _, p, _ = _attn_probs(q_ref, kp_ref, kc_ref, gq_ref, gk_ref, sinks_ref, n_heads)
        vb = jnp.concatenate([vp_ref[...], vc_ref[...]], axis=0).astype(BF16)
        o = jnp.dot(p.astype(BF16), vb, preferred_element_type=F32)
        o_ref[...] = o.reshape(g, blk, hd).astype(o_ref.dtype)

    q_spec, prev_spec, cur_spec, gain_spec, sink_spec = _attn_specs(t)
    return _pcall(body, grid=(n_kv, t // blk),
                  in_specs=[q_spec, prev_spec, cur_spec, prev_spec, cur_spec, gain_spec, gain_spec, sink_spec],
                  out_specs=q_spec, out_shape=SDS(q.shape, BF16), name=name,
                  compiler_params=_params("arbitrary", "arbitrary"))(q, k, k, v, v, gq, gk, sinks)


def _attn_bwd(q, k, v, do, gq, gk, sinks, name):
    n_kv, g, t, hd = q.shape
    blk = ATTN_BLOCK
    n_heads = n_kv * g

    def body(q_ref, kp_ref, kc_ref, vp_ref, vc_ref, do_ref, gq_ref, gk_ref, sinks_ref,
             dq_ref, dkp_ref, dkc_ref, dvp_ref, dvc_ref, dgq_ref, dgk_ref, dsink_ref):
        kv, n = pl.program_id(0), pl.program_id(1)
        (_, qn, qh, q_rstd), (_, kn, kh, k_rstd), p, p_sink = _attn_probs(
            q_ref, kp_ref, kc_ref, gq_ref, gk_ref, sinks_ref, n_heads)
        vb = jnp.concatenate([vp_ref[...], vc_ref[...]], axis=0).astype(BF16)
        dob = do_ref[...].reshape(g * blk, hd).astype(BF16)
        dp = lax.dot_general(dob, vb, _DOT_DIMS["nt"], preferred_element_type=F32)
        delta = jnp.sum(p * dp, axis=1, keepdims=True)
        ds = (p * (dp - delta) * hd ** -0.5).astype(BF16)
        dv = lax.dot_general(p.astype(BF16), dob, _DOT_DIMS["tn"], preferred_element_type=F32)
        dqn = jnp.dot(ds, kn.astype(BF16), preferred_element_type=F32)
        dkn = lax.dot_general(ds, qn.astype(BF16), _DOT_DIMS["tn"], preferred_element_type=F32)
        dq, dgq = _head_norm_bwd(dqn, qh, q_rstd, gq_ref[...])
        dk, dgk = _head_norm_bwd(dkn, kh, k_rstd, gk_ref[...])
        dq_ref[...] = dq.reshape(g, blk, hd)
        dkp_ref[...] = dk[:blk]
        dkc_ref[...] = dk[blk:]
        dvp_ref[...] = dv[:blk]
        dvc_ref[...] = dv[blk:]

        @pl.when((kv == 0) & (n == 0))
        def _():
            dgq_ref[...] = jnp.zeros_like(dgq_ref)
            dgk_ref[...] = jnp.zeros_like(dgk_ref)

        @pl.when(n == 0)
        def _():
            dsink_ref[...] = jnp.zeros_like(dsink_ref)
        dgq_ref[...] += dgq
        dgk_ref[...] += dgk
        dsink = jnp.sum((-p_sink * delta).reshape(g, blk, 1), axis=1)
        dsink_ref[...] += jnp.broadcast_to(dsink, (g, 128))

    q_spec, prev_spec, cur_spec, gain_spec, sink_spec = _attn_specs(t)
    kv_shape = SDS(k.shape, F32)
    return _pcall(
        body, grid=(n_kv, t // blk),
        in_specs=[q_spec, prev_spec, cur_spec, prev_spec, cur_spec, q_spec, gain_spec, gain_spec, sink_spec],
        out_specs=[q_spec, cur_spec, cur_spec, cur_spec, cur_spec, gain_spec, gain_spec,
                   pl.BlockSpec((None, g, 128), lambda a, n: (a, 0, 0))],
        out_shape=[SDS(q.shape, F32), kv_shape, kv_shape, kv_shape, kv_shape, SDS((1, hd), F32), SDS((1, hd), F32),
                   SDS((n_kv, g, 128), F32)],
        name=name, compiler_params=_params("arbitrary", "arbitrary"))(q, k, k, v, v, do, gq, gk, sinks)


def _col_head(width):
    return lax.shift_right_logical(lax.broadcasted_iota(jnp.int32, (1, width), 1), 6)


def _expand_heads(v, col_head):
    out = jnp.zeros((v.shape[0], col_head.shape[1]), F32)
    for r in range(SSM_R):
        out = jnp.where(col_head == r, v[:, r:r + 1], out)
    return out


def _head_sums(m, col_head):
    lane = lax.broadcasted_iota(jnp.int32, (1, SSM_R), 1)
    out = jnp.zeros((m.shape[0], SSM_R), F32)
    for r in range(SSM_R):
        out = jnp.where(lane == r, jnp.sum(jnp.where(col_head == r, m, 0.0), axis=1, keepdims=True), out)
    return out


def _ssd_specs(d_inner, n_groups, n_chunks, rev):
    l, w, n = SSM_L, SSM_R * SSM_P, SSM_N

    def cc(c):
        return n_chunks - 1 - c if rev else c
    x_spec = pl.BlockSpec((l, w), lambda g, c: (cc(c), g))
    b_spec = pl.BlockSpec((l, n), lambda g, c: (cc(c), d_inner // n + g))
    c_spec = pl.BlockSpec((l, n), lambda g, c: (cc(c), d_inner // n + n_groups + g))
    col_spec = pl.BlockSpec((None, l, SSM_R), lambda g, c: (g, cc(c), 0))
    row_spec = pl.BlockSpec((None, None, SSM_R, l), lambda g, c: (g, cc(c), 0, 0))
    state_spec = pl.BlockSpec((None, None, n, w), lambda g, c: (g, cc(c), 0, 0))
    return x_spec, b_spec, c_spec, col_spec, row_spec, state_spec


def _decay_matrix(a, a_row, r):
    l = SSM_L
    causal = lax.broadcasted_iota(jnp.int32, (l, l), 0) >= lax.broadcasted_iota(jnp.int32, (l, l), 1)
    return jnp.exp(jnp.where(causal, a[:, r:r + 1] - a_row[r:r + 1, :], NEG_BIG))


def _ssd_fwd(xbc, dt_g, cs_g, cs_row, d_inner, name):
    t = xbc.shape[0]
    l, w, n = SSM_L, SSM_R * SSM_P, SSM_N
    n_groups, n_chunks = d_inner // w, t // l

    def body(x_ref, b_ref, c_ref, dt_ref, a_ref, ar_ref, y_ref, prev_ref, h_scr):
        @pl.when(pl.program_id(1) == 0)
        def _():
            h_scr[...] = jnp.zeros_like(h_scr)
        col_head = _col_head(w)
        a, a_row = a_ref[...], ar_ref[...]
        h = h_scr[...]
        prev_ref[...] = h
        xdt = x_ref[...] * _expand_heads(dt_ref[...], col_head)
        xb, bb, cb = xdt.astype(BF16), b_ref[...].astype(BF16), c_ref[...].astype(BF16)
        a_last = a[l - 1:l, :]
        cbt = lax.dot_general(cb, bb, _DOT_DIMS["nt"], preferred_element_type=F32)
        y = jnp.zeros((l, w), F32)
        for r in range(SSM_R):
            m = (cbt * _decay_matrix(a, a_row, r)).astype(BF16)
            y = jnp.where(col_head == r, jnp.dot(m, xb, preferred_element_type=F32), y)
        y_off = jnp.dot(cb, h.astype(BF16), preferred_element_type=F32)
        y_ref[...] = y + _expand_heads(jnp.exp(a), col_head) * y_off
        fx = (xdt * _expand_heads(jnp.exp(a_last - a), col_head)).astype(BF16)
        states = lax.dot_general(bb, fx, _DOT_DIMS["tn"], preferred_element_type=F32)
        h_scr[...] = _expand_heads(jnp.exp(a_last), col_head) * h + states

    x_spec, b_spec, c_spec, col_spec, row_spec, state_spec = _ssd_specs(d_inner, n_groups, n_chunks, False)
    return _pcall(body, grid=(n_groups, n_chunks),
                  in_specs=[x_spec, b_spec, c_spec, col_spec, col_spec, row_spec],
                  out_specs=[x_spec, state_spec],
                  out_shape=[SDS((t, d_inner), F32), SDS((n_groups, n_chunks, n, w), F32)],
                  scratch_shapes=[pltpu.VMEM((n, w), F32)], name=name,
                  compiler_params=_params("arbitrary", "arbitrary"))(xbc, xbc, xbc, dt_g, cs_g, cs_row)


def _ssd_bwd(xbc, dt_g, cs_g, cs_row, prev, dy, d_inner, name):
    t = xbc.shape[0]
    l, w, n = SSM_L, SSM_R * SSM_P, SSM_N
    n_groups, n_chunks = d_inner // w, t // l

    def body(x_ref, b_ref, c_ref, dt_ref, a_ref, ar_ref, prev_ref, dy_ref,
             dx_ref, db_ref, dc_ref, ddtx_ref, dda_ref, dh_scr):
        @pl.when(pl.program_id(1) == 0)
        def _():
            dh_scr[...] = jnp.zeros_like(dh_scr)
        col_head = _col_head(w)
        a, a_row = a_ref[...], ar_ref[...]
        x, dyv = x_ref[...], dy_ref[...]
        h, dhn = prev_ref[...], dh_scr[...]
        dtx = _expand_heads(dt_ref[...], col_head)
        xdt = x * dtx
        a_last = a[l - 1:l, :]
        e_last = jnp.exp(a_last)
        xb, bb, cb = xdt.astype(BF16), b_ref[...].astype(BF16), c_ref[...].astype(BF16)
        hb, dhnb, dyb = h.astype(BF16), dhn.astype(BF16), dyv.astype(BF16)
        edy = (_expand_heads(jnp.exp(a), col_head) * dyv).astype(BF16)
        dc = lax.dot_general(edy, hb, _DOT_DIMS["nt"], preferred_element_type=F32)
        dh_scr[...] = (_expand_heads(e_last, col_head) * dhn
                       + lax.dot_general(cb, edy, _DOT_DIMS["tn"], preferred_element_type=F32))
        fx = xdt * _expand_heads(jnp.exp(a_last - a), col_head)
        g1 = jnp.dot(bb, dhnb, preferred_element_type=F32)
        db = lax.dot_general(fx.astype(BF16), dhnb, _DOT_DIMS["nt"], preferred_element_type=F32)
        q = _head_sums(fx * g1, col_head)
        d_e_last = _head_sums(_colsum(dhn * h), col_head)
        cbt = lax.dot_general(cb, bb, _DOT_DIMS["nt"], preferred_element_type=F32)
        dcbt = jnp.zeros((l, l), F32)
        dxd = jnp.zeros((l, w), F32)
        dda_diag = jnp.zeros((l, SSM_R), F32)
        rows_l = lax.broadcasted_iota(jnp.int32, (l, l), 0)
        cols_l = lax.broadcasted_iota(jnp.int32, (l, l), 1)
        upper = (cols_l >= rows_l).astype(F32)
        head_lane = lax.broadcasted_iota(jnp.int32, (1, SSM_R), 1)
        for r in range(SSM_R):
            decay = _decay_matrix(a, a_row, r)
            dyr = jnp.where(col_head == r, dyv, 0.0).astype(BF16)
            dm = lax.dot_general(dyr, xb, _DOT_DIMS["nt"], preferred_element_type=F32)
            dcbt = dcbt + dm * decay
            m = cbt * decay
            dxr = lax.dot_general(m.astype(BF16), dyb, _DOT_DIMS["tn"], preferred_element_type=F32)
            dxd = jnp.where(col_head == r, dxr, dxd)
            below = jnp.dot(upper, dm * m, precision=lax.Precision.HIGHEST, preferred_element_type=F32)
            dda_r = jnp.sum(jnp.where(cols_l < rows_l, below, 0.0), axis=1, keepdims=True)
            dda_diag = jnp.where(head_lane == r, dda_r, dda_diag)
        dcbb = dcbt.astype(BF16)
        dc_ref[...] = dc + jnp.dot(dcbb, bb, preferred_element_type=F32)
        db_ref[...] = db + lax.dot_general(dcbb, cb, _DOT_DIMS["tn"], preferred_element_type=F32)
        dxt = dxd + _expand_heads(jnp.exp(a_last - a), col_head) * g1
        dx_ref[...] = dxt * dtx
        ddtx_ref[...] = _head_sums(dxt * x, col_head)
        y_off = _expand_heads(jnp.exp(a), col_head) * jnp.dot(cb, hb, preferred_element_type=F32)
        row = lax.broadcasted_iota(jnp.int32, (l, 1), 0)
        da = _head_sums(dyv * y_off, col_head) + jnp.where(row == l - 1, e_last * d_e_last, 0.0)
        strict_lower = (cols_l < rows_l).astype(F32)
        dda_ref[...] = (dda_diag + jnp.dot(upper, da, precision=lax.Precision.HIGHEST, preferred_element_type=F32)
                        + jnp.dot(strict_lower, q, precision=lax.Precision.HIGHEST, preferred_element_type=F32))

    x_spec, b_spec, c_spec, col_spec, row_spec, state_spec = _ssd_specs(d_inner, n_groups, n_chunks, True)
    bc_out = pl.BlockSpec((l, n), lambda g, c: (n_chunks - 1 - c, g))
    return _pcall(body, grid=(n_groups, n_chunks),
                  in_specs=[x_spec, b_spec, c_spec, col_spec, col_spec, row_spec, state_spec, x_spec],
                  out_specs=[x_spec, bc_out, bc_out, col_spec, col_spec],
                  out_shape=[SDS((t, d_inner), F32), SDS((t, n_groups * n), F32), SDS((t, n_groups * n), F32),
                             SDS(dt_g.shape, F32), SDS(dt_g.shape, F32)],
                  scratch_shapes=[pltpu.VMEM((n, w), F32)], name=name,
                  compiler_params=_params("arbitrary", "arbitrary"))(xbc, xbc, xbc, dt_g, cs_g, cs_row, prev, dy)


def _softplus(x):
    return jnp.maximum(x, 0.0) + jnp.log1p(jnp.exp(-jnp.abs(x)))


def _dt_fwd(dt_raw, bias, a_log, name):
    l = SSM_L

    def fn(raw, bias, a_log):
        dt = _softplus(raw + bias)
        lower = (lax.broadcasted_iota(jnp.int32, (l, l), 0) >= lax.broadcasted_iota(jnp.int32, (l, l), 1)).astype(F32)
        cs = jnp.dot(lower, dt * -jnp.exp(a_log), precision=lax.Precision.HIGHEST, preferred_element_type=F32)
        return dt, cs
    wd = dt_raw.shape[1]
    return _rowwise(fn, [dt_raw], [bias, a_log], [(wd, F32), (wd, F32)], name=name, tm=l)


def _dt_bwd(ddtx, dda, dt_raw, dt, bias, a_log, name):
    def fn(ddtx, dda, raw, dt, bias, a_log):
        a = -jnp.exp(a_log)
        draw = (ddtx + dda * a) * _sigmoid(raw + bias)
        return draw, _colsum(draw), _colsum(dda * dt) * a
    wd = dt_raw.shape[1]
    return _rowwise(fn, [ddtx, dda, dt_raw, dt], [bias, a_log], [(wd, BF16)], [wd, wd], name=name)


MESH_IDS = pl.DeviceIdType.MESH
ANY_SPEC = pl.BlockSpec(memory_space=pl.ANY)


def _mesh_pos():
    return lax.axis_index("x"), lax.axis_index("y"), lax.axis_index("c")


def _all_gather(arrays, name):
    n = len(arrays)

    def body(*refs):
        x_refs, out_refs = refs[:n], refs[n:2 * n]
        send_sems, recv_sems, local_sems = refs[2 * n:]
        mx, my, mc = _mesh_pos()
        me, sibling = (mx, my, mc), (mx, my, 1 - mc)
        chips = [(1 - mx, my), (mx, 1 - my), (1 - mx, 1 - my)]

        def copy(a, k, block, to, src=None):
            px, py, pc = block
            slot = out_refs[a].at[4 * px + 2 * py + pc]
            return pltpu.make_async_remote_copy(
                src_ref=slot if src is None else src, dst_ref=slot, send_sem=send_sems.at[7 * a + k],
                recv_sem=recv_sems.at[7 * a + k], device_id=to, device_id_type=MESH_IDS)

        mine = [pltpu.make_async_copy(x_refs[a], out_refs[a].at[4 * mx + 2 * my + mc], local_sems.at[a])
                for a in range(n)]
        sends = []
        for a in range(n):
            mine[a].start()
            first = [copy(a, 0, me, sibling, src=x_refs[a])]
            first += [copy(a, 1 + j, me, (*chip, mc), src=x_refs[a]) for j, chip in enumerate(chips)]
            for cp in first:
                cp.start()
            sends += first
        for j, chip in enumerate(chips):
            for a in range(n):
                copy(a, 1 + j, (*chip, mc), me).wait_recv()
                passed = copy(a, 4 + j, (*chip, mc), sibling)
                passed.start()
                sends.append(passed)
        for a in range(n):
            copy(a, 0, sibling, me).wait_recv()
            for j, chip in enumerate(chips):
                copy(a, 4 + j, (*chip, 1 - mc), me).wait_recv()
        for cp in sends:
            cp.wait_send()
        for cp in mine:
            cp.wait()

    return _pcall(body, out_shape=[SDS((N_DEV,) + x.shape, x.dtype) for x in arrays], in_specs=[ANY_SPEC] * n,
                  out_specs=[ANY_SPEC] * n,
                  scratch_shapes=[pltpu.SemaphoreType.DMA((7 * n,)), pltpu.SemaphoreType.DMA((7 * n,)),
                                  pltpu.SemaphoreType.DMA((n,))], name=name)(*arrays)


def _swap_with_sibling(arrays, name):
    n = len(arrays)

    def body(*refs):
        g_refs, out_refs = refs[:n], refs[n:2 * n]
        send_sems, recv_sems = refs[2 * n:]
        mx, my, mc = _mesh_pos()
        copies = [pltpu.make_async_remote_copy(
            src_ref=g_refs[a].at[2 * k + (1 - mc)], dst_ref=out_refs[a].at[k], send_sem=send_sems.at[4 * a + k],
            recv_sem=recv_sems.at[4 * a + k], device_id=(mx, my, 1 - mc), device_id_type=MESH_IDS)
            for a in range(n) for k in range(4)]
        for cp in copies:
            cp.start()
        for cp in copies:
            cp.wait()

    return _pcall(body, out_shape=[SDS((4,) + g.shape[1:], g.dtype) for g in arrays], in_specs=[ANY_SPEC] * n,
                  out_specs=[ANY_SPEC] * n,
                  scratch_shapes=[pltpu.SemaphoreType.DMA((4 * n,)), pltpu.SemaphoreType.DMA((4 * n,))],
                  name=name)(*arrays)


HBM_SPEC = pl.BlockSpec(memory_space=pltpu.HBM)
SEM_SPEC = pl.BlockSpec(memory_space=pltpu.SEMAPHORE)
SIDE_EFFECT = pltpu.SideEffectType.DATAFLOW_SIDE_EFFECTING


def _chip_copies(src_refs, land_refs, send_sems, recv_sems, per_chip):
    mx, my, mc = _mesh_pos()
    chips = [(1 - mx, my), (mx, 1 - my), (1 - mx, 1 - my)]
    return [pltpu.make_async_remote_copy(
        src_ref=src_refs[a].at[2 * px + py] if per_chip else src_refs[a], dst_ref=land_refs[a].at[j],
        send_sem=send_sems.at[3 * a + j], recv_sem=recv_sems.at[3 * a + j], device_id=(px, py, mc),
        device_id_type=MESH_IDS) for a in range(len(src_refs)) for j, (px, py) in enumerate(chips)]


def _chips_start(arrays, per_chip, name):
    n = len(arrays)
    lands = [lax.empty((3,) + (a.shape[1:] if per_chip else a.shape), a.dtype) for a in arrays]

    def body(*refs):
        for cp in _chip_copies(refs[:n], refs[n:2 * n], refs[2 * n], refs[2 * n + 1], per_chip):
            cp.start()
        refs[-1][...] = jnp.zeros_like(refs[-1])

    res = _pcall(
        body, name=name,
        out_shape=(pltpu.SemaphoreType.DMA((3 * n,)), pltpu.SemaphoreType.DMA((3 * n,)),
                   *[pltpu.HBM(a.shape, a.dtype) for a in arrays], *[pltpu.HBM(l.shape, l.dtype) for l in lands],
                   SDS((8, 128), F32)),
        in_specs=[HBM_SPEC] * (2 * n),
        out_specs=(SEM_SPEC, SEM_SPEC, *[HBM_SPEC] * (2 * n), pl.BlockSpec(memory_space=pltpu.VMEM)),
        input_output_aliases={i: 2 + i for i in range(2 * n)},
        compiler_params=pltpu.CompilerParams(has_side_effects=SIDE_EFFECT),
    )(*[pltpu.with_memory_space_constraint(a, pltpu.HBM) for a in arrays],
      *[pltpu.with_memory_space_constraint(l, pltpu.HBM) for l in lands])
    return (res[0], res[1], list(res[2:2 + n]), list(res[2 + n:2 + 2 * n])), res[-1]


def _chips_wait(handle, after, per_chip, name):
    send_sems, recv_sems, srcs, lands = handle
    n = len(srcs)

    def body(*refs):
        for cp in _chip_copies(refs[:n], refs[n:2 * n], refs[2 * n], refs[2 * n + 1], per_chip):
            cp.wait_send()
            cp.wait_recv()

    res = _pcall(
        body, name=name, out_shape=tuple(pltpu.HBM(a.shape, a.dtype) for a in srcs + lands),
        in_specs=[HBM_SPEC] * (2 * n) + [SEM_SPEC, SEM_SPEC, ANY_SPEC], out_specs=tuple([HBM_SPEC] * (2 * n)),
        input_output_aliases={i: i for i in range(2 * n)},
        compiler_params=pltpu.CompilerParams(has_side_effects=SIDE_EFFECT),
    )(*srcs, *lands, send_sems, recv_sems, after)
    return list(res[:n]), list(res[n:])


def _gather_finish(xs, lands, name):
    n = len(xs)

    def body(*refs):
        x_refs, land_refs, out_refs = refs[:n], refs[n:2 * n], refs[2 * n:3 * n]
        send_sems, recv_sems, local_sems = refs[3 * n:]
        mx, my, mc = _mesh_pos()
        origins = [(mx, my), (1 - mx, my), (mx, 1 - my), (1 - mx, 1 - my)]
        local, remote = [], []
        for a in range(n):
            for k, (px, py) in enumerate(origins):
                src = x_refs[a] if k == 0 else land_refs[a].at[k - 1]
                slot = out_refs[a].at[4 * px + 2 * py + mc]
                local.append(pltpu.make_async_copy(src, slot, local_sems.at[4 * a + k]))
                remote.append(pltpu.make_async_remote_copy(
                    src_ref=src, dst_ref=slot, send_sem=send_sems.at[4 * a + k], recv_sem=recv_sems.at[4 * a + k],
                    device_id=(mx, my, 1 - mc), device_id_type=MESH_IDS))
        for cp in local + remote:
            cp.start()
        for a in range(n):
            for k, (px, py) in enumerate(origins):
                slot = out_refs[a].at[4 * px + 2 * py + (1 - mc)]
                pltpu.make_async_remote_copy(
                    src_ref=slot, dst_ref=slot, send_sem=send_sems.at[4 * a + k], recv_sem=recv_sems.at[4 * a + k],
                    device_id=(mx, my, 1 - mc), device_id_type=MESH_IDS).wait_recv()
        for cp in remote:
            cp.wait_send()
        for cp in local:
            cp.wait()

    return _pcall(body, out_shape=[SDS((N_DEV,) + x.shape, x.dtype) for x in xs], in_specs=[ANY_SPEC] * (2 * n),
                  out_specs=[ANY_SPEC] * n,
                  scratch_shapes=[pltpu.SemaphoreType.DMA((4 * n,)), pltpu.SemaphoreType.DMA((4 * n,)),
                                  pltpu.SemaphoreType.DMA((4 * n,))], name=name)(*xs, *lands)


def _add_core_blocks(gs, recvs, core, name):
    n = len(gs)

    def body(core_ref, *refs):
        for a in range(n):
            refs[2 * n + a][...] = (refs[a][...].astype(F32) + refs[n + a][...].astype(F32)).astype(refs[2 * n + a].dtype)

    in_specs = [pl.BlockSpec((None, None) + g.shape[1:], lambda k, core_ref: (k, core_ref[0], 0, 0)) for g in gs]
    in_specs += [pl.BlockSpec((None,) + r.shape[1:], lambda k, core_ref: (k, 0, 0)) for r in recvs]
    grid_spec = pltpu.PrefetchScalarGridSpec(
        num_scalar_prefetch=1, grid=(4,), in_specs=in_specs,
        out_specs=[pl.BlockSpec((None,) + r.shape[1:], lambda k, core_ref: (k, 0, 0)) for r in recvs])
    return _pcall(body, grid_spec=grid_spec, out_shape=[SDS(r.shape, r.dtype) for r in recvs], name=name,
                  compiler_params=_params("arbitrary"))(core, *[g.reshape((4, 2) + g.shape[1:]) for g in gs], *recvs)


def _add_chip_blocks(parts, recvs, chip, name):
    n = len(parts)
    n_row_blocks = next(nb for nb in (4, 2, 1) if all(p.shape[1] % (nb * PACK_ROW_ALIGN) == 0 for p in parts))

    def body(chip_ref, *refs):
        for a in range(n):
            r_ref = refs[n + a]
            refs[2 * n + a][...] = ((refs[a][...].astype(F32) + r_ref[0].astype(F32)) + r_ref[1].astype(F32)
                                    ) + r_ref[2].astype(F32)

    def rows(p):
        return p.shape[1] // n_row_blocks
    in_specs = [pl.BlockSpec((None, rows(p), p.shape[2]), lambda i, chip_ref: (chip_ref[0], i, 0)) for p in parts]
    in_specs += [pl.BlockSpec((3, rows(p), p.shape[2]), lambda i, chip_ref: (0, i, 0)) for p in parts]
    grid_spec = pltpu.PrefetchScalarGridSpec(
        num_scalar_prefetch=1, grid=(n_row_blocks,), in_specs=in_specs,
        out_specs=[pl.BlockSpec((rows(p), p.shape[2]), lambda i, chip_ref: (i, 0)) for p in parts])
    return _pcall(body, grid_spec=grid_spec, out_shape=[SDS(p.shape[1:], F32) for p in parts], name=name,
                  compiler_params=_params("arbitrary"))(chip, *parts, *recvs)


class _Exchange:
    def __init__(self, shards):
        self.shards = shards
        self.core = lax.axis_index("c").astype(jnp.int32).reshape(1)
        self.chip = (2 * lax.axis_index("x") + lax.axis_index("y")).astype(jnp.int32).reshape(1)

    def gather_start(self, i):
        xs = [self.shards[n][_layer_of(n, i)] for n in _layer_weights(i)]
        return _chips_start(xs, False, f"l{i}_gather_start")

    def gather_finish(self, i, handle, after):
        xs, lands = _chips_wait(handle, after, False, f"l{i}_gather_wait")
        out = {}
        for n, g in zip(_layer_weights(i), _gather_finish(xs, lands, f"l{i}_gather_finish")):
            axis = SHARD_AXIS[n] - 1
            out[n] = _Sharded(g[:, None], 0, axis) if n in DIRECT else _join_shards(g, axis)
        return out

    def reduce_start(self, tag, names, gs):
        parts = _add_core_blocks(gs, _swap_with_sibling(gs, tag + "_rs_sibling"), self.core, tag + "_rs_add_core")
        return _chips_start(parts, True, tag + "_rs_start")

    def reduce_finish(self, tag, handle, after):
        parts, lands = _chips_wait(handle, after, True, tag + "_rs_wait")
        return _add_chip_blocks(parts, lands, self.chip, tag + "_rs_add_chips")


def _layer_weights(i):
    return MIXER_WEIGHTS[i % N_MIXERS] + LAYER_WEIGHTS


def _layer_of(name, i):
    return i if name in LAYER_WEIGHTS else i // N_MIXERS


def _slab_rows(n):
    return -(-n // (PACK_COLS * PACK_ROW_ALIGN)) * PACK_ROW_ALIGN


def _pack(arrays):
    def slab(a):
        flat = a.reshape(-1)
        rows = _slab_rows(flat.shape[0])
        return jnp.pad(flat, (0, rows * PACK_COLS - flat.shape[0])).reshape(rows, PACK_COLS)
    return jnp.concatenate([slab(a) for a in arrays], axis=0)


def _unpack(slab, shapes, lead=()):
    out, row = [], 0
    for shape in shapes:
        n = math.prod(shape)
        rows = _slab_rows(n)
        part = lax.slice_in_dim(slab, row, row + rows, axis=len(lead)).reshape(lead + (rows * PACK_COLS,))
        out.append(lax.slice_in_dim(part, 0, n, axis=len(lead)).reshape(lead + tuple(shape)))
        row += rows
    return out


def _join_shards(t, axis):
    t = jnp.moveaxis(t, 0, axis)
    return t.reshape(t.shape[:axis] + (t.shape[axis] * t.shape[axis + 1],) + t.shape[axis + 2:])


def _split_shards(full, axis):
    s = full.shape
    t = full.reshape(s[:axis] + (N_DEV, s[axis] // N_DEV) + s[axis + 1:])
    return jnp.moveaxis(t, axis, 0)


def _weight_grad(lhs, rhs, name, tag):
    axis = SHARD_AXIS[name] - 1
    if name in DIRECT:
        return _mm(lhs, rhs, "tn", [BF16], name=tag, shard_out=axis)
    return _split_shards(_mm(lhs, rhs, "tn", [F32], name=tag), axis).astype(BF16)


def _attention_fwd(u, h, w, wl, j, tag):
    t, d = u.shape
    n_heads = d // HEAD_DIM
    n_kv = n_heads // Q_PER_KV
    qkv = _mm(u, wl["a_wqkv"], "nn", [F32], name=tag + "_qkv")
    q = qkv[:, :d].reshape(t, n_kv, Q_PER_KV, HEAD_DIM).transpose(1, 2, 0, 3)
    k = qkv[:, d:d + n_kv * HEAD_DIM].reshape(t, n_kv, HEAD_DIM).transpose(1, 0, 2)
    v = qkv[:, d + n_kv * HEAD_DIM:].reshape(t, n_kv, HEAD_DIM).transpose(1, 0, 2)
    gq, gk, sinks = w["a_q_norm_g"][j][None], w["a_k_norm_g"][j][None], w["a_sinks"][j]
    o = _attn_fwd(q, k, v, gq, gk, sinks, tag + "_attn")
    o2 = o.transpose(2, 0, 1, 3).reshape(t, d)
    h1 = _mm(o2, wl["a_wo"], "nn", [F32], name=tag + "_wo", epi=lambda acc, h: acc + h, rows=[h])
    return h1, (q, k, v, o2)


def _attention_bwd(ctx, u, dh1, dh1b, w, wl, j, tag, grads, big):
    q, k, v, o2 = ctx
    t, d = u.shape
    n_kv = q.shape[0]
    gq, gk, sinks = w["a_q_norm_g"][j][None], w["a_k_norm_g"][j][None], w["a_sinks"][j]
    big.append(("a_wo", j, _weight_grad(o2, dh1b, "a_wo", tag + "_dwo")))
    do2 = _mm(dh1b, wl["a_wo"], "nt", [F32], name=tag + "_do")
    do = do2.reshape(t, n_kv, Q_PER_KV, HEAD_DIM).transpose(1, 2, 0, 3)
    dq, dkp, dkc, dvp, dvc, dgq, dgk, dsink = _attn_bwd(q, k, v, do, gq, gk, sinks, tag + "_attn_bwd")

    def fold(prev_part, cur_part):
        shifted = jnp.concatenate([prev_part[:, ATTN_BLOCK:], jnp.zeros_like(prev_part[:, :ATTN_BLOCK])], axis=1)
        both = _add(shifted.reshape(n_kv * t, HEAD_DIM), cur_part.reshape(n_kv * t, HEAD_DIM), tag + "_fold")
        return both.reshape(n_kv, t, HEAD_DIM).transpose(1, 0, 2).reshape(t, n_kv * HEAD_DIM)
    dqkv = jnp.concatenate([dq.transpose(2, 0, 1, 3).reshape(t, d), fold(dkp, dkc), fold(dvp, dvc)], axis=1).astype(BF16)
    grads["a_q_norm_g"][j] = dgq[0]
    grads["a_k_norm_g"][j] = dgk[0]
    grads["a_sinks"][j] = dsink[:, :, 0].reshape(-1)
    big.append(("a_wqkv", j, _weight_grad(u, dqkv, "a_wqkv", tag + "_dwqkv")))
    return _mm(dqkv, wl["a_wqkv"], "nt", [F32], name=tag + "_du")


def _conformer_fwd(u, h, w, wl, j, tag):
    t, d = u.shape
    a = _mm(u, wl["b_w_pw1"], "nn", [F32], name=tag + "_pw1", epi=lambda acc, b: acc + b, pars=[w["b_b_pw1"][j][None]])
    y = _rowwise(lambda val, gate: val * _sigmoid(gate), [(a, d, 0), (a, d, 1)], [], [(d, F32)], name=tag + "_glu")[0]
    y2 = _dwconv_fwd(y, w["b_w_dw"][j], w["b_b_dw"][j][None], False, tag + "_dw")[0]

    def ln_silu(y2, g, b):
        mu = _rowmean(y2)
        yc = y2 - mu
        y3 = yc * lax.rsqrt(_rowmean(yc * yc) + EPS) * g + b
        return y3 * _sigmoid(y3)
    y4 = _rowwise(ln_silu, [y2], [w["b_ln_g"][j][None], w["b_ln_b"][j][None]], [(d, BF16)], name=tag + "_ln")[0]
    h1 = _mm(y4, wl["b_w_pw2"], "nn", [F32], name=tag + "_pw2", epi=lambda acc, h, b: acc + h + b, rows=[h],
             pars=[w["b_b_pw2"][j][None]])
    return h1, (a, y, y2, y4)


def _conformer_bwd(ctx, u, dh1, dh1b, dh1_colsum, w, wl, j, tag, grads, big):
    a, y, y2, y4 = ctx
    t, d = u.shape
    big.append(("b_w_pw2", j, _weight_grad(y4, dh1b, "b_w_pw2", tag + "_dwpw2")))
    grads["b_b_pw2"][j] = dh1_colsum[0]
    dy4 = _mm(dh1b, wl["b_w_pw2"], "nt", [F32], name=tag + "_dy4")

    def ln_silu_bwd(dy4, y2, g, b):
        mu = _rowmean(y2)
        yc = y2 - mu
        rstd = lax.rsqrt(_rowmean(yc * yc) + EPS)
        xh = yc * rstd
        dy3 = dy4 * _dsilu(xh * g + b)
        dxh = dy3 * g
        return rstd * (dxh - _rowmean(dxh) - xh * _rowmean(dxh * xh)), _colsum(dy3 * xh), _colsum(dy3)
    dy2, dlg, dlb = _rowwise(ln_silu_bwd, [dy4, y2], [w["b_ln_g"][j][None], w["b_ln_b"][j][None]], [(d, F32)], [d, d],
                             name=tag + "_ln_bwd")
    grads["b_ln_g"][j], grads["b_ln_b"][j] = dlg[0], dlb[0]
    dy, dw_dw, db_dw = _dwconv_bwd(y, dy2, w["b_w_dw"][j], tag + "_dw_bwd")
    grads["b_w_dw"][j], grads["b_b_dw"][j] = dw_dw, db_dw[0]

    def glu_bwd(dy, val, gate):
        s = _sigmoid(gate)
        dval, dgate = dy * s, dy * val * s * (1.0 - s)
        return dval, dgate, _colsum(dval), _colsum(dgate)
    dval, dgate, dbv, dbg = _rowwise(glu_bwd, [dy, (a, d, 0), (a, d, 1)], [], [(d, BF16), (d, BF16)], [d, d],
                                     name=tag + "_glu_bwd")
    da = jnp.concatenate([dval, dgate], axis=1)
    grads["b_b_pw1"][j] = jnp.concatenate([dbv[0], dbg[0]])
    big.append(("b_w_pw1", j, _weight_grad(u, da, "b_w_pw1", tag + "_dwpw1")))
    return _mm(da, wl["b_w_pw1"], "nt", [F32], name=tag + "_du")


def _ssd_layouts(v, n_groups):
    t = v.shape[0]
    vg = v[:, :n_groups * SSM_R].reshape(t, n_groups, SSM_R).transpose(1, 0, 2)
    return vg, vg.reshape(n_groups, t // SSM_L, SSM_L, SSM_R).transpose(0, 1, 3, 2)


def _from_group_layout(vg, width):
    n_groups, t, _ = vg.shape
    v = vg.transpose(1, 0, 2).reshape(t, n_groups * SSM_R)
    return jnp.pad(v, ((0, 0), (0, width - n_groups * SSM_R)))


def _pad_lanes(v, width):
    return jnp.pad(v, [(0, 0)] * (v.ndim - 1) + [(0, width - v.shape[-1])])


def _mamba_weights(wl, d_inner, conv_c):
    w_in = wl["c_w_in"]
    n_heads = w_in.shape[1] - d_inner - conv_c
    return (w_in[:, :d_inner], w_in[:, d_inner:d_inner + conv_c], _pad_lanes(w_in[:, d_inner + conv_c:], 128), n_heads)


def _mamba_fwd(u, h, w, wl, j, tag):
    t, d = u.shape
    d_inner = w["c_D"].shape[1] * SSM_P
    conv_c = w["c_w_conv"][j].shape[1]
    w_z, w_xbc, w_dt, n_heads = _mamba_weights(wl, d_inner, conv_c)
    n_groups = n_heads // SSM_R
    z = _mm(u, w_z, "nn", [F32], name=tag + "_in_z")
    xbc_raw = _mm(u, w_xbc, "nn", [F32], name=tag + "_in_xbc")
    dt_raw = _mm(u, w_dt, "nn", [F32], name=tag + "_in_dt")
    pre, xbc = _dwconv_fwd(xbc_raw, w["c_w_conv"][j], w["c_b_conv"][j][None], True, tag + "_conv")
    bias, a_log = _pad_lanes(w["c_dt_bias"][j][None], 128), _pad_lanes(w["c_A_log"][j][None], 128)
    dt, cs = _dt_fwd(dt_raw, bias, a_log, tag + "_dt")
    dt_g, _ = _ssd_layouts(dt, n_groups)
    cs_g, cs_row = _ssd_layouts(cs, n_groups)
    y, prev = _ssd_fwd(xbc, dt_g, cs_g, cs_row, d_inner, tag + "_ssd")
    d_cols = jnp.repeat(w["c_D"][j], SSM_P)[None]
    g = w["c_norm_g"][j][None]

    def gated_norm(y, xs, z, d_cols, g):
        yg = (y + xs * d_cols) * (z * _sigmoid(z))
        return yg * lax.rsqrt(_rowmean(yg * yg) + EPS) * g
    yn = _rowwise(gated_norm, [y, (xbc, d_inner, 0), z], [d_cols, g], [(d_inner, BF16)], name=tag + "_gnorm")[0]
    h1 = _mm(yn, wl["c_w_out"], "nn", [F32], name=tag + "_out", epi=lambda acc, h: acc + h, rows=[h])
    return h1, (z, xbc_raw, dt_raw, pre, xbc, dt, dt_g, cs_g, cs_row, y, prev, yn)


def _mamba_bwd(ctx, u, dh1, dh1b, w, wl, j, tag, grads, big):
    z, xbc_raw, dt_raw, pre, xbc, dt, dt_g, cs_g, cs_row, y, prev, yn = ctx
    t, d = u.shape
    d_inner = w["c_D"].shape[1] * SSM_P
    conv_c = w["c_w_conv"][j].shape[1]
    w_z, w_xbc, w_dt, n_heads = _mamba_weights(wl, d_inner, conv_c)
    n_groups = n_heads // SSM_R
    big.append(("c_w_out", j, _weight_grad(yn, dh1b, "c_w_out", tag + "_dwout")))
    dyn = _mm(dh1b, wl["c_w_out"], "nt", [F32], name=tag + "_dyn")
    d_cols = jnp.repeat(w["c_D"][j], SSM_P)[None]
    g = w["c_norm_g"][j][None]

    def gated_norm_bwd(dyn, y, xs, z, d_cols, g):
        sg = _sigmoid(z)
        yt = y + xs * d_cols
        yg = yt * (z * sg)
        rstd = lax.rsqrt(_rowmean(yg * yg) + EPS)
        xh = yg * rstd
        dxh = dyn * g
        dyg = rstd * (dxh - xh * _rowmean(dxh * xh))
        dyt = dyg * (z * sg)
        return dyt, dyg * yt * (sg * (1.0 + z * (1.0 - sg))), dyt * d_cols, _colsum(dyn * xh), _colsum(dyt * xs)
    dyt, dz, dxs_skip, dg, dd_cols = _rowwise(
        gated_norm_bwd, [dyn, y, (xbc, d_inner, 0), z], [d_cols, g], [(d_inner, F32), (d_inner, BF16), (d_inner, F32)],
        [d_inner, d_inner], name=tag + "_gnorm_bwd")
    grads["c_norm_g"][j] = dg[0]
    grads["c_D"][j] = dd_cols.reshape(n_heads, SSM_P).sum(axis=1)
    dx, db, dc, ddtx_g, dda_g = _ssd_bwd(xbc, dt_g, cs_g, cs_row, prev, dyt, d_inner, tag + "_ssd_bwd")
    dxs = _add(dx, dxs_skip, tag + "_dxs")
    dpost = jnp.concatenate([dxs, db, dc], axis=1)
    dpre = _rowwise(lambda dpost, pre: dpost * _dsilu(pre), [dpost, pre], [], [(conv_c, F32)], name=tag + "_silu_bwd")[0]
    dxbc, dw_conv, db_conv = _dwconv_bwd(xbc_raw, dpre, w["c_w_conv"][j], tag + "_conv_bwd")
    grads["c_w_conv"][j], grads["c_b_conv"][j] = dw_conv, db_conv[0]
    bias, a_log = _pad_lanes(w["c_dt_bias"][j][None], 128), _pad_lanes(w["c_A_log"][j][None], 128)
    ddt_raw, dbias, da_log = _dt_bwd(_from_group_layout(ddtx_g, 128), _from_group_layout(dda_g, 128), dt_raw, dt, bias,
                                     a_log, tag + "_dt_bwd")
    grads["c_dt_bias"][j], grads["c_A_log"][j] = dbias[0, :n_heads], da_log[0, :n_heads]
    dxbc_b = dxbc.astype(BF16)
    dw_in = jnp.concatenate(
        [_mm(u, dz, "tn", [F32], name=tag + "_dwin_z"), _mm(u, dxbc_b, "tn", [F32], name=tag + "_dwin_xbc"),
         _mm(u, ddt_raw, "tn", [F32], name=tag + "_dwin_dt")[:, :n_heads]], axis=1)
    big.append(("c_w_in", j, _split_shards(dw_in, SHARD_AXIS["c_w_in"] - 1).astype(BF16)))
    du = _mm(dz, w_z, "nt", [F32], name=tag + "_du_z")
    du = _mm(dxbc_b, w_xbc, "nt", [F32], name=tag + "_du_xbc", epi=lambda acc, r: acc + r, rows=[du])
    return _mm(ddt_raw, w_dt, "nt", [F32], name=tag + "_du_dt", epi=lambda acc, r: acc + r, rows=[du])


def _local_step(x, p, tgt, w, comm):
    depth = p.shape[0]
    grads = {name: {} for name in REPLICATED + tuple(n for n, _ in SMALL_SHARDED)}
    reduced = {name: {} for name, _ in BIG_SHARDED}
    saved = []
    h = x
    handle, token = comm.gather_start(0)
    wl = comm.gather_finish(0, handle, h)
    for i in range(depth):
        kind, j, tag = i % N_MIXERS, i // N_MIXERS, f"l{i}"
        token = None
        if i + 1 < depth:
            handle, token = comm.gather_start(i + 1)
        u = _rms_fwd(h, w["mix_norm_g"][i][None], tag + "_rms_mix", after=token)
        fwd = (_attention_fwd, _conformer_fwd, _mamba_fwd)[kind]
        h1, ctx = fwd(u, h, w, wl, j, tag)
        u2 = _rms_fwd(h1, w["mlp_norm_g"][i][None], tag + "_rms_mlp")
        a, r = _mm(u2, wl["m_w1"], "nn", [F32, BF16], name=tag + "_w1",
                   epi=lambda acc: (acc, jnp.square(jnp.maximum(acc, 0.0))))
        h2 = _mm(r, wl["m_w2"], "nn", [F32], name=tag + "_w2", epi=lambda acc, h: acc + h, rows=[h1])
        u3 = _rms_fwd(h2, w["ple_norm_g"][i][None], tag + "_rms_ple")
        gl = _mm(u3, wl["ple_w_gate"], "nn", [F32], name=tag + "_gate")
        pp = _mm(p[i], wl["ple_w_proj"], "nn", [F32], name=tag + "_proj")
        h3 = _ple_fwd(h2, gl, pp, tag + "_ple")
        saved.append((h, u, ctx, h1, u2, a, r, h2, u3, gl, pp, wl))
        h = h3
        if i + 1 < depth:
            wl = comm.gather_finish(i + 1, handle, h)

    dh, sq_err = _loss_and_grad(h, tgt, "loss")
    pending, token = None, None
    for i in reversed(range(depth)):
        kind, j, tag = i % N_MIXERS, i // N_MIXERS, f"l{i}"
        h0, u, ctx, h1, u2, a, r, h2, u3, gl, pp, wl = saved[i]
        big = []
        dgl, dpp = _ple_bwd(dh, gl, pp, tag + "_ple_bwd", after=token)
        big.append(("ple_w_proj", i, _weight_grad(p[i], dpp, "ple_w_proj", tag + "_dwproj")))
        big.append(("ple_w_gate", i, _weight_grad(u3, dgl, "ple_w_gate", tag + "_dwgate")))
        du3 = _mm(dgl, wl["ple_w_gate"], "nt", [F32], name=tag + "_du3")
        dh2, dh2b, dg, _ = _rms_bwd(du3, h2, w["ple_norm_g"][i][None], dh, tag + "_rms_ple_bwd")
        grads["ple_norm_g"][i] = dg[0]
        big.append(("m_w2", i, _weight_grad(r, dh2b, "m_w2", tag + "_dw2")))
        da = _mm(dh2b, wl["m_w2"], "nt", [BF16], name=tag + "_da",
                 epi=lambda acc, a: acc * (2.0 * jnp.maximum(a, 0.0)), rows=[a])
        big.append(("m_w1", i, _weight_grad(u2, da, "m_w1", tag + "_dw1")))
        du2 = _mm(da, wl["m_w1"], "nt", [F32], name=tag + "_du2")
        dh1, dh1b, dg, dh1_colsum = _rms_bwd(du2, h1, w["mlp_norm_g"][i][None], dh2, tag + "_rms_mlp_bwd")
        grads["mlp_norm_g"][i] = dg[0]
        if kind == 0:
            du = _attention_bwd(ctx, u, dh1, dh1b, w, wl, j, tag, grads, big)
        elif kind == 1:
            du = _conformer_bwd(ctx, u, dh1, dh1b, dh1_colsum, w, wl, j, tag, grads, big)
        else:
            du = _mamba_bwd(ctx, u, dh1, dh1b, w, wl, j, tag, grads, big)
        dh, _, dg, _ = _rms_bwd(du, h0, w["mix_norm_g"][i][None], dh1, tag + "_rms_mix_bwd")
        grads["mix_norm_g"][i] = dg[0]
        if pending is not None:
            _finish_reduce(comm, pending, dh, reduced)
        handle, token = comm.reduce_start(tag, [name for name, _, _ in big], [g for _, _, g in big])
        pending = (tag, handle, [(name, layer) for name, layer, _ in big])

    def stack(d):
        return {name: jnp.stack([g[k] for k in sorted(g)]) for name, g in d.items()}
    small = stack(grads)
    _finish_reduce(comm, pending, small["mix_norm_g"], reduced)
    return sq_err, dh, stack(reduced), small


def _finish_reduce(comm, pending, after, reduced):
    tag, handle, keys = pending
    for (name, layer), g in zip(keys, comm.reduce_finish(tag, handle, after)):
        reduced[name][layer] = g


def kernel(x, p, mix_norm_g, mlp_norm_g, ple_norm_g, a_wqkv, a_q_norm_g, a_k_norm_g, a_sinks, a_wo, b_w_pw1, b_b_pw1, b_w_dw, b_b_dw, b_ln_g, b_ln_b, b_w_pw2, b_b_pw2, c_w_in, c_w_conv, c_b_conv, c_dt_bias, c_A_log, c_D, c_norm_g, c_w_out, m_w1, m_w2, ple_w_proj, ple_w_gate, loss_target, m_mix_norm_g, m_mlp_norm_g, m_ple_norm_g, m_a_wqkv, m_a_q_norm_g, m_a_k_norm_g, m_a_sinks, m_a_wo, m_b_w_pw1, m_b_b_pw1, m_b_w_dw, m_b_b_dw, m_b_ln_g, m_b_ln_b, m_b_w_pw2, m_b_b_pw2, m_c_w_in, m_c_w_conv, m_c_b_conv, m_c_dt_bias, m_c_A_log, m_c_D, m_c_norm_g, m_c_w_out, m_m_w1, m_m_w2, m_ple_w_proj, m_ple_w_gate, v_mix_norm_g, v_mlp_norm_g, v_ple_norm_g, v_a_wqkv, v_a_q_norm_g, v_a_k_norm_g, v_a_sinks, v_a_wo, v_b_w_pw1, v_b_b_pw1, v_b_w_dw, v_b_b_dw, v_b_ln_g, v_b_ln_b, v_b_w_pw2, v_b_b_pw2, v_c_w_in, v_c_w_conv, v_c_b_conv, v_c_dt_bias, v_c_A_log, v_c_D, v_c_norm_g, v_c_w_out, v_m_w1, v_m_w2, v_ple_w_proj, v_ple_w_gate):
    args = dict(locals())
    wl = {n: args[n] for n in WEIGHTS}
    ml = {n: args["m_" + n] for n in WEIGHTS}
    vl = {n: args["v_" + n] for n in WEIGHTS}
    d_model = x.shape[-1]
    small_sharded = [n for n, _ in SMALL_SHARDED]

    whole = {n: wl[n] for n in REPLICATED}
    small =_all_gather([_pack([wl[n] for n in small_sharded])], "gather_small")[0]
    for (n, axis), t in zip(SMALL_SHARDED, _unpack(small, [wl[n].shape for n in small_sharded], (N_DEV,))):
        whole[n] = _join_shards(t, axis)
    comm = _Exchange({n: wl[n].astype(BF16) for n, _ in BIG_SHARDED})
    sq_err, grad_x, g_big, g_small = _local_step(x[0], p[:, 0], loss_target[0], whole, comm)
    loss = lax.psum(0.5 * jnp.sum(sq_err) / d_model, ("x", "y", "c"))

    small_names = list(REPLICATED) + small_sharded
    summed = _sum_blocks(_all_gather([_pack([g_small[n] for n in small_names])], "gather_small_grads")[0],
                         "sum_small_grads")
    g_small = dict(zip(small_names, _unpack(summed, [g_small[n].shape for n in small_names])))
    me = 4 * lax.axis_index("x") + 2 * lax.axis_index("y") + lax.axis_index("c")
    for n, axis in SMALL_SHARDED:
        size = wl[n].shape[axis]
        g_small[n] = lax.dynamic_slice_in_dim(g_small[n], me * size, size, axis)

    out = {}
    for n, _ in BIG_SHARDED:
        shape = wl[n].shape
        two_d = (shape[0] * shape[1], shape[2])
        delta, new_m, new_v = _adamw(wl[n].reshape(two_d), g_big[n].reshape(two_d), ml[n].reshape(two_d),
                                     vl[n].reshape(two_d), "adamw_" + n)
        out.update({"grad_" + n: g_big[n], "delta_" + n: delta.reshape(shape), "new_m_" + n: new_m.reshape(shape),
                    "new_v_" + n: new_v.reshape(shape)})
    g_slab = _pack([g_small[n] for n in small_names])
    delta, new_m, new_v = _adamw(_pack([wl[n] for n in small_names]), g_slab, _pack([ml[n] for n in small_names]),
                                 _pack([vl[n] for n in small_names]), "adamw_small")
    for kind, slab in (("delta", delta), ("new_m", new_m), ("new_v", new_v)):
        out.update({kind + "_" + n: a for n, a in zip(small_names, _unpack(slab, [wl[n].shape for n in small_names]))})
    out.update({"grad_" + n: g_small[n] for n in small_names})
    return (loss, grad_x[None], *[out[k + "_" + n] for k in ("grad", "delta", "new_m", "new_v") for n in WEIGHTS])
```

```python
import math
from typing import Any, NamedTuple

import jax
import jax.numpy as jnp
from jax import lax
from jax.experimental import pallas as pl
from jax.experimental.pallas import tpu as pltpu

F32 = jnp.float32
BF16 = jnp.bfloat16
SDS = jax.ShapeDtypeStruct

EPS = 1e-6
N_MIXERS = 3
HEAD_DIM = 64
Q_PER_KV = 4
ATTN_BLOCK = 128
SSM_P = 64
SSM_R = 4
SSM_N = 128
SSM_L = 128
ADAM_LR, ADAM_B1, ADAM_B2, ADAM_EPS, ADAM_WD, ADAM_STEP = 0.001, 0.9, 0.999, 1e-08, 0.01, 10
N_DEV = 8
PACK_COLS = 1024
PACK_ROW_ALIGN = 16
VMEM_LIMIT = 56 * 1024 * 1024
MM_VMEM_BUDGET = 40 * 1024 * 1024
ROW_VMEM_BUDGET = 24 * 1024 * 1024
CONV_PAD = 32
CONV_ROWS = 64
NEG_BIG = -1e30

BIG_SHARDED = (("a_wqkv", 2), ("a_wo", 1), ("b_w_pw1", 2), ("b_w_pw2", 1), ("c_w_in", 2), ("c_w_out", 1),
               ("m_w1", 2), ("m_w2", 1), ("ple_w_proj", 2), ("ple_w_gate", 1))
SMALL_SHARDED = (("b_w_dw", 2), ("c_w_conv", 2), ("c_b_conv", 1), ("c_norm_g", 1))
REPLICATED = ("mix_norm_g", "mlp_norm_g", "ple_norm_g", "a_q_norm_g", "a_k_norm_g", "a_sinks", "b_b_pw1", "b_b_dw",
              "b_ln_g", "b_ln_b", "b_b_pw2", "c_dt_bias", "c_A_log", "c_D")
DIRECT = frozenset(("a_wo", "b_w_pw1", "b_w_pw2", "c_w_out", "m_w1", "m_w2", "ple_w_proj", "ple_w_gate"))
SHARD_AXIS = dict(BIG_SHARDED + SMALL_SHARDED)
MIXER_WEIGHTS = (("a_wqkv", "a_wo"), ("b_w_pw1", "b_w_pw2"), ("c_w_in", "c_w_out"))
LAYER_WEIGHTS = ("m_w1", "m_w2", "ple_w_proj", "ple_w_gate")
WEIGHTS = ("mix_norm_g", "mlp_norm_g", "ple_norm_g", "a_wqkv", "a_q_norm_g", "a_k_norm_g", "a_sinks", "a_wo",
           "b_w_pw1", "b_b_pw1", "b_w_dw", "b_b_dw", "b_ln_g", "b_ln_b", "b_w_pw2", "b_b_pw2", "c_w_in", "c_w_conv",
           "c_b_conv", "c_dt_bias", "c_A_log", "c_D", "c_norm_g", "c_w_out", "m_w1", "m_w2", "ple_w_proj",
           "ple_w_gate")


def _pcall(body, **kw):
    return pl.pallas_call(body, **kw)


def _params(*sem):
    return pltpu.CompilerParams(dimension_semantics=sem, vmem_limit_bytes=VMEM_LIMIT)


def _sigmoid(x):
    return 1.0 / (1.0 + jnp.exp(-x))


def _dsilu(x):
    s = _sigmoid(x)
    return s * (1.0 + x * (1.0 - s))


def _colsum(x):
    return jnp.sum(x, axis=0, keepdims=True)


def _rowmean(x):
    return jnp.mean(x, axis=-1, keepdims=True)


def _rowwise(fn, rows, pars, row_outs, acc_outs=(), *, name, tm=None):
    rows = [r if isinstance(r, tuple) else (r, r.shape[1], 0) for r in rows]
    t = rows[0][0].shape[0]
    per_row = sum(w * a.dtype.itemsize for a, w, _ in rows) + sum(w * jnp.dtype(d).itemsize for w, d in row_outs)
    if tm is None:
        tm = 1024
        while tm > 8 and (t % tm or 2 * tm * per_row > ROW_VMEM_BUDGET):
            tm //= 2
    assert t % tm == 0, (name, t, tm)
    n_in, n_row = len(rows) + len(pars), len(row_outs)

    def body(*refs):
        outs = fn(*[r[...] for r in refs[:n_in]])
        outs = outs if isinstance(outs, tuple) else (outs,)
        for ref, o in zip(refs[n_in:n_in + n_row], outs[:n_row]):
            ref[...] = o.astype(ref.dtype)
        acc_refs = refs[n_in + n_row:]
        if acc_refs:
            @pl.when(pl.program_id(0) == 0)
            def _():
                for ref in acc_refs:
                    ref[...] = jnp.zeros_like(ref)
            for ref, o in zip(acc_refs, outs[n_row:]):
                ref[...] += o

    in_specs = [pl.BlockSpec((tm, w), lambda i, cb=cb: (i, cb)) for _, w, cb in rows]
    in_specs += [pl.BlockSpec(p.shape, lambda i: (0, 0)) for p in pars]
    out_specs = [pl.BlockSpec((tm, w), lambda i: (i, 0)) for w, _ in row_outs]
    out_specs += [pl.BlockSpec((1, w), lambda i: (0, 0)) for w in acc_outs]
    out_shape = [SDS((t, w), d) for w, d in row_outs] + [SDS((1, w), F32) for w in acc_outs]
    return _pcall(body, grid=(t // tm,), in_specs=in_specs, out_specs=out_specs, out_shape=out_shape, name=name,
                  compiler_params=_params("arbitrary"))(*[a for a, _, _ in rows], *pars)


_DOT_DIMS = {"nn": (((1,), (0,)), ((), ())), "nt": (((1,), (1,)), ((), ())), "tn": (((0,), (0,)), ((), ()))}


class _Sharded(NamedTuple):
    g: Any
    layer: int
    axis: int


def _mm(a, b, form, out_dtypes, *, name, epi=None, rows=(), pars=(), shard_out=None):
    sharded = isinstance(b, _Sharded)
    if sharded:
        wa, wb = b.g.shape[2:]
        b_shape = (N_DEV * wa, wb) if b.axis == 0 else (wa, N_DEV * wb)
        b_item, layer = b.g.dtype.itemsize, b.layer
    else:
        b_shape, b_item = b.shape, b.dtype.itemsize
    if form == "tn":
        k, m = a.shape
        n = b_shape[1]
    else:
        m, k = a.shape
        n = b_shape[0] if form == "nt" else b_shape[1]
    rows = [r if isinstance(r, tuple) else (r, 0) for r in rows]
    tn = next((c for c in (512, 256, 128) if n % c == 0), n)
    b_shards = tn // wb if sharded and b.axis == 1 and form == "nn" and wb < tn else 1
    out_rows, out_cols = (m // N_DEV if shard_out == 0 else None), (n // N_DEV if shard_out == 1 else None)
    if out_cols is not None and out_cols > tn:
        tn = next(c for c in (512, 256, 128) if out_cols % c == 0)

    def need(tm_):
        blk = tm_ * k * a.dtype.itemsize + tn * k * b_item
        blk += sum(tm_ * tn * jnp.dtype(d).itemsize for d in out_dtypes) + sum(tm_ * tn * r.dtype.itemsize for r, _ in rows)
        return 2 * blk

    tm = next((c for c in (1024, 512, 256, 128, 64, 32, 16) if m % c == 0 and need(c) <= MM_VMEM_BUDGET
               and (out_rows is None or out_rows % c == 0 or c % out_rows == 0)), None)
    assert tm is not None, (name, m, k, n)
    nx = len(rows) + len(pars)
    pieces = sharded and form == "nt" and b.axis == 1

    def body(*refs):
        av = refs[0][...].astype(BF16)
        if pieces:
            acc = jnp.zeros((tm, tn), F32)
            for d in range(N_DEV):
                acc += lax.dot_general(av[:, d * wb:(d + 1) * wb], refs[1][d], _DOT_DIMS["nt"], preferred_element_type=F32)
        else:
            bv = refs[1][...]
            if b_shards > 1:
                acc = jnp.concatenate([jnp.dot(av, bv[d], preferred_element_type=F32) for d in range(b_shards)], axis=1)
            else:
                if sharded and bv.ndim == 3:
                    bv = bv.reshape(bv.shape[0] * bv.shape[1], bv.shape[2])
                acc = lax.dot_general(av, bv.astype(BF16), _DOT_DIMS[form], preferred_element_type=F32)
        outs = epi(acc, *[r[...] for r in refs[2:2 + nx]]) if epi else acc
        outs = outs if isinstance(outs, tuple) else (outs,)
        for ref, o in zip(refs[2 + nx:], outs):
            if out_rows is not None and tm > out_rows:
                ref[...] = o.reshape(tm // out_rows, out_rows, tn).astype(ref.dtype)
            elif out_cols is not None and tn > out_cols:
                for d in range(tn // out_cols):
                    ref[d] = o[:, d * out_cols:(d + 1) * out_cols].astype(ref.dtype)
            else:
                ref[...] = o.astype(ref.dtype)

    a_spec = pl.BlockSpec((k, tm), lambda i, j: (0, i)) if form == "tn" else pl.BlockSpec((tm, k), lambda i, j: (i, 0))
    if not sharded:
        b_spec = pl.BlockSpec((tn, k), lambda i, j: (j, 0)) if form == "nt" else pl.BlockSpec((k, tn), lambda i, j: (0, j))
    elif form == "nn" and b.axis == 0:
        b_spec = pl.BlockSpec((N_DEV, None, wa, tn), lambda i, j: (0, layer, 0, j))
    elif form == "nn" and b_shards > 1:
        b_spec = pl.BlockSpec((b_shards, None, wa, wb), lambda i, j: (j, layer, 0, 0))
    elif form == "nn":
        per = wb // tn
        b_spec = pl.BlockSpec((None, None, wa, tn), lambda i, j: (j // per, layer, 0, j % per))
    elif form == "nt" and b.axis == 0 and tn >= wa:
        b_spec = pl.BlockSpec((tn // wa, None, wa, k), lambda i, j: (j, layer, 0, 0))
    elif form == "nt" and b.axis == 0:
        per = wa // tn
        b_spec = pl.BlockSpec((None, None, tn, k), lambda i, j: (j // per, layer, j % per, 0))
    else:
        assert form == "nt", (name, form)
        b_spec = pl.BlockSpec((N_DEV, None, tn, wb), lambda i, j: (0, layer, j, 0))
    in_specs = [a_spec, b_spec]
    in_specs += [pl.BlockSpec((tm, tn), lambda i, j, off=off: (i, j + off)) for _, off in rows]
    in_specs += [pl.BlockSpec((1, tn), lambda i, j: (0, j)) for _ in pars]
    if shard_out == 0 and tm > out_rows:
        out_specs = [pl.BlockSpec((tm // out_rows, out_rows, tn), lambda i, j: (i, 0, j)) for _ in out_dtypes]
        out_shape = [SDS((N_DEV, out_rows, n), d) for d in out_dtypes]
    elif shard_out == 0:
        per_m = out_rows // tm
        out_specs = [pl.BlockSpec((None, tm, tn), lambda i, j: (i // per_m, i % per_m, j)) for _ in out_dtypes]
        out_shape = [SDS((N_DEV, out_rows, n), d) for d in out_dtypes]
    elif shard_out == 1 and tn > out_cols:
        out_specs = [pl.BlockSpec((tn // out_cols, tm, out_cols), lambda i, j: (j, i, 0)) for _ in out_dtypes]
        out_shape = [SDS((N_DEV, m, out_cols), d) for d in out_dtypes]
    elif shard_out == 1:
        per_n = out_cols // tn
        out_specs = [pl.BlockSpec((None, tm, tn), lambda i, j: (j // per_n, i, j % per_n)) for _ in out_dtypes]
        out_shape = [SDS((N_DEV, m, out_cols), d) for d in out_dtypes]
    else:
        out_specs = [pl.BlockSpec((tm, tn), lambda i, j: (i, j)) for _ in out_dtypes]
        out_shape = [SDS((m, n), d) for d in out_dtypes]
    res = _pcall(body, grid=(m // tm, n // tn), in_specs=in_specs, out_specs=out_specs, out_shape=out_shape, name=name,
                 compiler_params=_params("arbitrary", "arbitrary"))(a, b.g if sharded else b, *[r for r, _ in rows], *pars)
    return res[0] if len(out_dtypes) == 1 else res


def _rms_fwd(x, g, name, after=None):
    def fn(x, g, *_):
        return x * lax.rsqrt(_rowmean(x * x) + EPS) * g
    return _rowwise(fn, [x], [g] + ([] if after is None else [after]), [(x.shape[1], BF16)], name=name)[0]


def _rms_bwd(dy, x, g, dres, name):
    def fn(dy, x, dres, g):
        rstd = lax.rsqrt(_rowmean(x * x) + EPS)
        xh = x * rstd
        dxh = dy * g
        tot = dres + rstd * (dxh - xh * _rowmean(dxh * xh))
        return tot, tot, _colsum(dy * xh), _colsum(tot)
    d = x.shape[1]
    return _rowwise(fn, [dy, x, dres], [g], [(d, F32), (d, BF16)], [d, d], name=name)


def _ple_fwd(h, gl, pp, name):
    return _rowwise(lambda h, gl, pp: h + _sigmoid(gl) * pp, [h, gl, pp], [], [(h.shape[1], F32)], name=name)[0]


def _ple_bwd(dh, gl, pp, name, after=None):
    def fn(dh, gl, pp, *_):
        gate = _sigmoid(gl)
        return dh * pp * gate * (1.0 - gate), dh * gate
    d = dh.shape[1]
    return _rowwise(fn, [dh, gl, pp], [] if after is None else [after], [(d, BF16), (d, BF16)], name=name)


def _loss_and_grad(h, tgt, name):
    d = h.shape[1]

    def fn(h, tgt):
        err = h - tgt
        return err * (1.0 / d), _colsum(err * err)
    return _rowwise(fn, [h, tgt], [], [(d, F32)], [d], name=name)


def _add(a, b, name):
    return _rowwise(lambda a, b: a + b, [a, b], [], [(a.shape[1], F32)], name=name)[0]


def _adamw(w, g, m, v, name):
    def fn(w, g, m, v):
        m = ADAM_B1 * m + (1.0 - ADAM_B1) * g
        v = ADAM_B2 * v + (1.0 - ADAM_B2) * (g * g)
        m_hat = m / (1.0 - ADAM_B1 ** ADAM_STEP)
        v_hat = v / (1.0 - ADAM_B2 ** ADAM_STEP)
        return -ADAM_LR * (m_hat / (jnp.sqrt(v_hat) + ADAM_EPS) + ADAM_WD * w), m, v
    c = w.shape[1]
    return _rowwise(fn, [w, g, m, v], [], [(c, F32)] * 3, name=name)


def _sum_blocks(x, name):
    n = x.shape[0]

    def body(x_ref, o_ref):
        acc = x_ref[0]
        for j in range(1, n):
            acc = acc + x_ref[j]
        o_ref[...] = acc
    return _pcall(body, out_shape=SDS(x.shape[1:], x.dtype), name=name, compiler_params=_params())(x)


def _shifted(chunk, off, rows):
    if off % 8 == 0:
        return chunk[off:off + rows]
    return pltpu.roll(chunk, chunk.shape[0] - off, 0)[:rows]


def _conv_cols(c):
    return 256 if c % 256 == 0 else 128


def _rows_at(i, tb):
    return pl.ds(i * tb, tb) if isinstance(i, int) else pl.ds(pl.multiple_of(i * tb, 8), tb)


def _chunk_before(ref, i, tb, pad):
    if isinstance(i, int) and i == 0:
        return jnp.concatenate([jnp.zeros((pad, ref.shape[1]), F32), ref[0:tb, :]], axis=0)
    if isinstance(i, int):
        return ref[i * tb - pad:(i + 1) * tb, :]
    return ref[pl.ds(pl.multiple_of(i * tb - pad, 8), tb + pad), :]


def _chunk_after(ref, i, tb, pad, last):
    if last:
        return jnp.concatenate([ref[i * tb:(i + 1) * tb, :], jnp.zeros((pad, ref.shape[1]), F32)], axis=0)
    if isinstance(i, int):
        return ref[i * tb:(i + 1) * tb + pad, :]
    return ref[pl.ds(pl.multiple_of(i * tb, 8), tb + pad), :]


def _dwconv_fwd(y, w, b, silu, name):
    t, c = y.shape
    taps = w.shape[0]
    cb, tb, pad = _conv_cols(c), CONV_ROWS, CONV_PAD

    def body(y_ref, w_ref, b_ref, *out_refs):
        def step(i, carry):
            chunk = _chunk_before(y_ref, i, tb, pad)
            acc = jnp.broadcast_to(b_ref[...], (tb, cb))
            for k in range(taps):
                acc = acc + _shifted(chunk, pad - (taps - 1) + k, tb) * w_ref[k:k + 1, :]
            rows = _rows_at(i, tb)
            out_refs[0][rows, :] = acc
            if silu:
                out_refs[1][rows, :] = acc * _sigmoid(acc)
            return carry
        step(0, 0)
        lax.fori_loop(1, t // tb, step, 0)

    n_out = 2 if silu else 1
    return _pcall(body, grid=(c // cb,),
                  in_specs=[pl.BlockSpec((t, cb), lambda j: (0, j)), pl.BlockSpec((taps, cb), lambda j: (0, j)),
                            pl.BlockSpec((1, cb), lambda j: (0, j))],
                  out_specs=[pl.BlockSpec((t, cb), lambda j: (0, j))] * n_out,
                  out_shape=[SDS((t, c), F32)] * n_out, name=name, compiler_params=_params("arbitrary"))(y, w, b)


def _dwconv_bwd(y, dout, w, name):
    t, c = y.shape
    taps = w.shape[0]
    cb, tb, pad = _conv_cols(c), CONV_ROWS, CONV_PAD
    taps_pad = -(-taps // 8) * 8
    n_steps = t // tb
    assert n_steps >= 2

    def body(y_ref, d_ref, w_ref, dy_ref, dw_ref, db_ref):
        dw_ref[...] = jnp.zeros_like(dw_ref)

        def step(i, db, last=False):
            ychunk = _chunk_before(y_ref, i, tb, pad)
            dchunk = _chunk_after(d_ref, i, tb, pad, last)
            d0 = dchunk[:tb]
            acc = jnp.zeros((tb, cb), F32)
            for k in range(taps):
                acc = acc + _shifted(dchunk, taps - 1 - k, tb) * w_ref[k:k + 1, :]
                dw_ref[k:k + 1, :] += _colsum(d0 * _shifted(ychunk, pad - (taps - 1) + k, tb))
            dy_ref[_rows_at(i, tb), :] = acc
            return db + _colsum(d0)
        db = step(0, jnp.zeros((1, cb), F32))
        db = lax.fori_loop(1, n_steps - 1, step, db)
        db_ref[...] = step(n_steps - 1, db, last=True)

    dy, dw, db = _pcall(
        body, grid=(c // cb,),
        in_specs=[pl.BlockSpec((t, cb), lambda j: (0, j)), pl.BlockSpec((t, cb), lambda j: (0, j)),
                  pl.BlockSpec((taps, cb), lambda j: (0, j))],
        out_specs=[pl.BlockSpec((t, cb), lambda j: (0, j)), pl.BlockSpec((taps_pad, cb), lambda j: (0, j)),
                   pl.BlockSpec((1, cb), lambda j: (0, j))],
        out_shape=[SDS((t, c), F32), SDS((taps_pad, c), F32), SDS((1, c), F32)], name=name,
        compiler_params=_params("arbitrary"))(y, dout, w)
    return dy, dw[:taps], db


def _head_norm(x, g):
    rstd = lax.rsqrt(_rowmean(x * x) + EPS)
    xh = x * rstd
    return xh * g, xh, rstd


def _head_norm_bwd(dy, xh, rstd, g):
    dxh = dy * g
    return rstd * (dxh - xh * _rowmean(dxh * xh)), _colsum(dy * xh)


def _attn_probs(q_ref, kp_ref, kc_ref, gq_ref, gk_ref, sinks_ref, n_heads):
    g, blk, hd = Q_PER_KV, ATTN_BLOCK, HEAD_DIM
    kv, n = pl.program_id(0), pl.program_id(1)
    q_raw = q_ref[...].reshape(g * blk, hd)
    k_raw = jnp.concatenate([kp_ref[...], kc_ref[...]], axis=0)
    qn, qh, q_rstd = _head_norm(q_raw, gq_ref[...])
    kn, kh, k_rstd = _head_norm(k_raw, gk_ref[...])
    s = lax.dot_general(qn.astype(BF16), kn.astype(BF16), _DOT_DIMS["nt"], preferred_element_type=F32) * hd ** -0.5
    row = lax.broadcasted_iota(jnp.int32, (g * blk, 1), 0)
    g_row = lax.shift_right_logical(row, 7)
    qi = lax.bitwise_and(row, blk - 1)
    kj = lax.broadcasted_iota(jnp.int32, (1, 2 * blk), 1)
    dist = qi - kj + blk
    valid = (dist >= 0) & (dist < blk) & (kj >= jnp.where(n > 0, 0, blk))
    head = (kv * g + g_row + 1).astype(F32)
    slope = jnp.exp(head * (-8.0 * math.log(2.0) / n_heads))
    s = jnp.where(valid, s - slope * dist.astype(F32), NEG_BIG)
    sink = jnp.zeros((g * blk, 1), F32)
    for gi in range(g):
        sink = jnp.where(g_row == gi, sinks_ref[kv * g + gi], sink)
    m = jnp.maximum(jnp.max(s, axis=1, keepdims=True), sink)
    e = jnp.exp(s - m)
    e_sink = jnp.exp(sink - m)
    inv = 1.0 / (jnp.sum(e, axis=1, keepdims=True) + e_sink)
    return (q_raw, qn, qh, q_rstd), (k_raw, kn, kh, k_rstd), e * inv, e_sink * inv


def _attn_specs(t):
    g, blk, hd = Q_PER_KV, ATTN_BLOCK, HEAD_DIM
    q_spec = pl.BlockSpec((None, g, blk, hd), lambda a, n: (a, 0, n, 0))
    prev_spec = pl.BlockSpec((None, blk, hd), lambda a, n: (a, jnp.maximum(n - 1, 0), 0))
    cur_spec = pl.BlockSpec((None, blk, hd), lambda a, n: (a, n, 0))
    gain_spec = pl.BlockSpec((1, hd), lambda a, n: (0, 0))
    sink_spec = pl.BlockSpec(memory_space=pltpu.SMEM)
    return q_spec, prev_spec, cur_spec, gain_spec, sink_spec


def _attn_fwd(q, k, v, gq, gk, sinks, name):
    n_kv, g, t, hd = q.shape
    blk = ATTN_BLOCK
    n_heads = n_kv * g

    def body(q_ref, kp_ref, kc_ref, vp_ref, vc_ref, gq_ref, gk_ref, sinks_ref, o_ref):
        _, _, p, _ = _attn_probs(q_ref, kp_ref, kc_ref, gq_ref, gk_ref, sinks_ref, n_heads)
        vb = jnp.concatenate([vp_ref[...], vc_ref[...]], axis=0).astype(BF16)
        o = jnp.dot(p.astype(BF16), vb, preferred_element_type=F32)
        o_ref[...] = o.reshape(g, blk, hd).astype(o_ref.dtype)

    q_spec, prev_spec, cur_spec, gain_spec, sink_spec = _attn_specs(t)
    return _pcall(body, grid=(n_kv, t // blk),
                  in_specs=[q_spec, prev_spec, cur_spec, prev_spec, cur_spec, gain_spec, gain_spec, sink_spec],
                  out_specs=q_spec, out_shape=SDS(q.shape, BF16), name=name,
                  compiler_params=_params("arbitrary", "arbitrary"))(q, k, k, v, v, gq, gk, sinks)


def _attn_bwd(q, k, v, do, gq, gk, sinks, name):
    n_kv, g, t, hd = q.shape
    blk = ATTN_BLOCK
    n_heads = n_kv * g

    def body(q_ref, kp_ref, kc_ref, vp_ref, vc_ref, do_ref, gq_ref, gk_ref, sinks_ref,
             dq_ref, dkp_ref, dkc_ref, dvp_ref, dvc_ref, dgq_ref, dgk_ref, dsink_ref):
        kv, n = pl.program_id(0), pl.program_id(1)
        (_, qn, qh, q_rstd), (_, kn, kh, k_rstd), p, p_sink = _attn_probs(
            q_ref, kp_ref, kc_ref, gq_ref, gk_ref, sinks_ref, n_heads)
        vb = jnp.concatenate([vp_ref[...], vc_ref[...]], axis=0).astype(BF16)
        dob = do_ref[...].reshape(g * blk, hd).astype(BF16)
        dp = lax.dot_general(dob, vb, _DOT_DIMS["nt"], preferred_element_type=F32)
        delta = jnp.sum(p * dp, axis=1, keepdims=True)
        ds = (p * (dp - delta) * hd ** -0.5).astype(BF16)
        dv = lax.dot_general(p.astype(BF16), dob, _DOT_DIMS["tn"], preferred_element_type=F32)
        dqn = jnp.dot(ds, kn.astype(BF16), preferred_element_type=F32)
        dkn = lax.dot_general(ds, qn.astype(BF16), _DOT_DIMS["tn"], preferred_element_type=F32)
        dq, dgq = _head_norm_bwd(dqn, qh, q_rstd, gq_ref[...])
        dk, dgk = _head_norm_bwd(dkn, kh, k_rstd, gk_ref[...])
        dq_ref[...] = dq.reshape(g, blk, hd)
        dkp_ref[...] = dk[:blk]
        dkc_ref[...] = dk[blk:]
        dvp_ref[...] = dv[:blk]
        dvc_ref[...] = dv[blk:]

        @pl.when((kv == 0) & (n == 0))
        def _():
            dgq_ref[...] = jnp.zeros_like(dgq_ref)
            dgk_ref[...] = jnp.zeros_like(dgk_ref)

        @pl.when(n == 0)
        def _():
            dsink_ref[...] = jnp.zeros_like(dsink_ref)
        dgq_ref[...] += dgq
        dgk_ref[...] += dgk
        dsink = jnp.sum((-p_sink * delta).reshape(g, blk, 1), axis=1)
        dsink_ref[...] += jnp.broadcast_to(dsink, (g, 128))

    q_spec, prev_spec, cur_spec, gain_spec, sink_spec = _attn_specs(t)
    kv_shape = SDS(k.shape, F32)
    return _pcall(
        body, grid=(n_kv, t // blk),
        in_specs=[q_spec, prev_spec, cur_spec, prev_spec, cur_spec, q_spec, gain_spec, gain_spec, sink_spec],
        out_specs=[q_spec, cur_spec, cur_spec, cur_spec, cur_spec, gain_spec, gain_spec,
                   pl.BlockSpec((None, g, 128), lambda a, n: (a, 0, 0))],
        out_shape=[SDS(q.shape, F32), kv_shape, kv_shape, kv_shape, kv_shape, SDS((1, hd), F32), SDS((1, hd), F32),
                   SDS((n_kv, g, 128), F32)],
        name=name, compiler_params=_params("arbitrary", "arbitrary"))(q, k, k, v, v, do, gq, gk, sinks)


def _col_head(width):
    return lax.shift_right_logical(lax.broadcasted_iota(jnp.int32, (1, width), 1), 6)


def _expand_heads(v, col_head):
    out = jnp.zeros((v.shape[0], col_head.shape[1]), F32)
    for r in range(SSM_R):
        out = jnp.where(col_head == r, v[:, r:r + 1], out)
    return out


def _head_sums(m, col_head):
    lane = lax.broadcasted_iota(jnp.int32, (1, SSM_R), 1)
    out = jnp.zeros((m.shape[0], SSM_R), F32)
    for r in range(SSM_R):
        out = jnp.where(lane == r, jnp.sum(jnp.where(col_head == r, m, 0.0), axis=1, keepdims=True), out)
    return out


def _ssd_specs(d_inner, n_groups, n_chunks, rev):
    l, w, n = SSM_L, SSM_R * SSM_P, SSM_N

    def cc(c):
        return n_chunks - 1 - c if rev else c
    x_spec = pl.BlockSpec((l, w), lambda g, c: (cc(c), g))
    b_spec = pl.BlockSpec((l, n), lambda g, c: (cc(c), d_inner // n + g))
    c_spec = pl.BlockSpec((l, n), lambda g, c: (cc(c), d_inner // n + n_groups + g))
    col_spec = pl.BlockSpec((None, l, SSM_R), lambda g, c: (g, cc(c), 0))
    row_spec = pl.BlockSpec((None, None, SSM_R, l), lambda g, c: (g, cc(c), 0, 0))
    state_spec = pl.BlockSpec((None, None, n, w), lambda g, c: (g, cc(c), 0, 0))
    return x_spec, b_spec, c_spec, col_spec, row_spec, state_spec


def _decay_matrix(a, a_row, r):
    l = SSM_L
    causal = lax.broadcasted_iota(jnp.int32, (l, l), 0) >= lax.broadcasted_iota(jnp.int32, (l, l), 1)
    return jnp.exp(jnp.where(causal, a[:, r:r + 1] - a_row[r:r + 1, :], NEG_BIG))


def _ssd_fwd(xbc, dt_g, cs_g, cs_row, d_inner, name):
    t = xbc.shape[0]
    l, w, n = SSM_L, SSM_R * SSM_P, SSM_N
    n_groups, n_chunks = d_inner // w, t // l

    def body(x_ref, b_ref, c_ref, dt_ref, a_ref, ar_ref, y_ref, prev_ref, h_scr):
        @pl.when(pl.program_id(1) == 0)
        def _():
            h_scr[...] = jnp.zeros_like(h_scr)
        col_head = _col_head(w)
        a, a_row = a_ref[...], ar_ref[...]
        h = h_scr[...]
        prev_ref[...] = h
        xdt = x_ref[...] * _expand_heads(dt_ref[...], col_head)
        xb, bb, cb = xdt.astype(BF16), b_ref[...].astype(BF16), c_ref[...].astype(BF16)
        a_last = a[l - 1:l, :]
        cbt = lax.dot_general(cb, bb, _DOT_DIMS["nt"], preferred_element_type=F32)
        y = jnp.zeros((l, w), F32)
        for r in range(SSM_R):
            m = (cbt * _decay_matrix(a, a_row, r)).astype(BF16)
            y = jnp.where(col_head == r, jnp.dot(m, xb, preferred_element_type=F32), y)
        y_off = jnp.dot(cb, h.astype(BF16), preferred_element_type=F32)
        y_ref[...] = y + _expand_heads(jnp.exp(a), col_head) * y_off
        fx = (xdt * _expand_heads(jnp.exp(a_last - a), col_head)).astype(BF16)
        states = lax.dot_general(bb, fx, _DOT_DIMS["tn"], preferred_element_type=F32)
        h_scr[...] = _expand_heads(jnp.exp(a_last), col_head) * h + states

    x_spec, b_spec, c_spec, col_spec, row_spec, state_spec = _ssd_specs(d_inner, n_groups, n_chunks, False)
    return _pcall(body, grid=(n_groups, n_chunks),
                  in_specs=[x_spec, b_spec, c_spec, col_spec, col_spec, row_spec],
                  out_specs=[x_spec, state_spec],
                  out_shape=[SDS((t, d_inner), F32), SDS((n_groups, n_chunks, n, w), F32)],
                  scratch_shapes=[pltpu.VMEM((n, w), F32)], name=name,
                  compiler_params=_params("arbitrary", "arbitrary"))(xbc, xbc, xbc, dt_g, cs_g, cs_row)


def _ssd_bwd(xbc, dt_g, cs_g, cs_row, prev, dy, d_inner, name):
    t = xbc.shape[0]
    l, w, n = SSM_L, SSM_R * SSM_P, SSM_N
    n_groups, n_chunks = d_inner // w, t // l

    def body(x_ref, b_ref, c_ref, dt_ref, a_ref, ar_ref, prev_ref, dy_ref,
             dx_ref, db_ref, dc_ref, ddtx_ref, dda_ref, dh_scr):
        @pl.when(pl.program_id(1) == 0)
        def _():
            dh_scr[...] = jnp.zeros_like(dh_scr)
        col_head = _col_head(w)
        a, a_row = a_ref[...], ar_ref[...]
        x, dyv = x_ref[...], dy_ref[...]
        h, dhn = prev_ref[...], dh_scr[...]
        dtx = _expand_heads(dt_ref[...], col_head)
        xdt = x * dtx
        a_last = a[l - 1:l, :]
        e_last = jnp.exp(a_last)
        xb, bb, cb = xdt.astype(BF16), b_ref[...].astype(BF16), c_ref[...].astype(BF16)
        hb, dhnb, dyb = h.astype(BF16), dhn.astype(BF16), dyv.astype(BF16)
        edy = (_expand_heads(jnp.exp(a), col_head) * dyv).astype(BF16)
        dc = lax.dot_general(edy, hb, _DOT_DIMS["nt"], preferred_element_type=F32)
        dh_scr[...] = (_expand_heads(e_last, col_head) * dhn
                       + lax.dot_general(cb, edy, _DOT_DIMS["tn"], preferred_element_type=F32))
        fx = xdt * _expand_heads(jnp.exp(a_last - a), col_head)
        g1 = jnp.dot(bb, dhnb, preferred_element_type=F32)
        db = lax.dot_general(fx.astype(BF16), dhnb, _DOT_DIMS["nt"], preferred_element_type=F32)
        q = _head_sums(fx * g1, col_head)
        d_e_last = _head_sums(_colsum(dhn * h), col_head)
        cbt = lax.dot_general(cb, bb, _DOT_DIMS["nt"], preferred_element_type=F32)
        dcbt = jnp.zeros((l, l), F32)
        dxd = jnp.zeros((l, w), F32)
        dda_diag = jnp.zeros((l, SSM_R), F32)
        rows_l = lax.broadcasted_iota(jnp.int32, (l, l), 0)
        cols_l = lax.broadcasted_iota(jnp.int32, (l, l), 1)
        upper = (cols_l >= rows_l).astype(F32)
        head_lane = lax.broadcasted_iota(jnp.int32, (1, SSM_R), 1)
        for r in range(SSM_R):
            decay = _decay_matrix(a, a_row, r)
            dyr = jnp.where(col_head == r, dyv, 0.0).astype(BF16)
            dm = lax.dot_general(dyr, xb, _DOT_DIMS["nt"], preferred_element_type=F32)
            dcbt = dcbt + dm * decay
            m = cbt * decay
            dxr = lax.dot_general(m.astype(BF16), dyb, _DOT_DIMS["tn"], preferred_element_type=F32)
            dxd = jnp.where(col_head == r, dxr, dxd)
            below = jnp.dot(upper, dm * m, precision=lax.Precision.HIGHEST, preferred_element_type=F32)
            dda_r = jnp.sum(jnp.where(cols_l < rows_l, below, 0.0), axis=1, keepdims=True)
            dda_diag = jnp.where(head_lane == r, dda_r, dda_diag)
        dcbb = dcbt.astype(BF16)
        dc_ref[...] = dc + jnp.dot(dcbb, bb, preferred_element_type=F32)
        db_ref[...] = db + lax.dot_general(dcbb, cb, _DOT_DIMS["tn"], preferred_element_type=F32)
        dxt = dxd + _expand_heads(jnp.exp(a_last - a), col_head) * g1
        dx_ref[...] = dxt * dtx
        ddtx_ref[...] = _head_sums(dxt * x, col_head)
        y_off = _expand_heads(jnp.exp(a), col_head) * jnp.dot(cb, hb, preferred_element_type=F32)
        row = lax.broadcasted_iota(jnp.int32, (l, 1), 0)
        da = _head_sums(dyv * y_off, col_head) + jnp.where(row == l - 1, e_last * d_e_last, 0.0)
        strict_lower = (cols_l < rows_l).astype(F32)
        dda_ref[...] = (dda_diag + jnp.dot(upper, da, precision=lax.Precision.HIGHEST, preferred_element_type=F32)
                        + jnp.dot(strict_lower, q, precision=lax.Precision.HIGHEST, preferred_element_type=F32))

    x_spec, b_spec, c_spec, col_spec, row_spec, state_spec = _ssd_specs(d_inner, n_groups, n_chunks, True)
    bc_out = pl.BlockSpec((l, n), lambda g, c: (n_chunks - 1 - c, g))
    return _pcall(body, grid=(n_groups, n_chunks),
                  in_specs=[x_spec, b_spec, c_spec, col_spec, col_spec, row_spec, state_spec, x_spec],
                  out_specs=[x_spec, bc_out, bc_out, col_spec, col_spec],
                  out_shape=[SDS((t, d_inner), F32), SDS((t, n_groups * n), F32), SDS((t, n_groups * n), F32),
                             SDS(dt_g.shape, F32), SDS(dt_g.shape, F32)],
                  scratch_shapes=[pltpu.VMEM((n, w), F32)], name=name,
                  compiler_params=_params("arbitrary", "arbitrary"))(xbc, xbc, xbc, dt_g, cs_g, cs_row, prev, dy)


def _softplus(x):
    return jnp.maximum(x, 0.0) + jnp.log1p(jnp.exp(-jnp.abs(x)))


def _dt_fwd(dt_raw, bias, a_log, name):
    l = SSM_L

    def fn(raw, bias, a_log):
        dt = _softplus(raw + bias)
        lower = (lax.broadcasted_iota(jnp.int32, (l, l), 0) >= lax.broadcasted_iota(jnp.int32, (l, l), 1)).astype(F32)
        cs = jnp.dot(lower, dt * -jnp.exp(a_log), precision=lax.Precision.HIGHEST, preferred_element_type=F32)
        return dt, cs
    wd = dt_raw.shape[1]
    return _rowwise(fn, [dt_raw], [bias, a_log], [(wd, F32), (wd, F32)], name=name, tm=l)


def _dt_bwd(ddtx, dda, dt_raw, dt, bias, a_log, name):
    def fn(ddtx, dda, raw, dt, bias, a_log):
        a = -jnp.exp(a_log)
        draw = (ddtx + dda * a) * _sigmoid(raw + bias)
        return draw, _colsum(draw), _colsum(dda * dt) * a
    wd = dt_raw.shape[1]
    return _rowwise(fn, [ddtx, dda, dt_raw, dt], [bias, a_log], [(wd, BF16)], [wd, wd], name=name)


MESH_IDS = pl.DeviceIdType.MESH
ANY_SPEC = pl.BlockSpec(memory_space=pl.ANY)


def _mesh_pos():
    return lax.axis_index("x"), lax.axis_index("y"), lax.axis_index("c")


def _all_gather(arrays, name):
    n = len(arrays)

    def body(*refs):
        x_refs, out_refs = refs[:n], refs[n:2 * n]
        send_sems, recv_sems, local_sems = refs[2 * n:]
        mx, my, mc = _mesh_pos()
        me, sibling = (mx, my, mc), (mx, my, 1 - mc)
        chips = [(1 - mx, my), (mx, 1 - my), (1 - mx, 1 - my)]

        def copy(a, k, block, to, src=None):
            px, py, pc = block
            slot = out_refs[a].at[4 * px + 2 * py + pc]
            return pltpu.make_async_remote_copy(
                src_ref=slot if src is None else src, dst_ref=slot, send_sem=send_sems.at[7 * a + k],
                recv_sem=recv_sems.at[7 * a + k], device_id=to, device_id_type=MESH_IDS)

        mine = [pltpu.make_async_copy(x_refs[a], out_refs[a].at[4 * mx + 2 * my + mc], local_sems.at[a])
                for a in range(n)]
        sends = []
        for a in range(n):
            mine[a].start()
            first = [copy(a, 0, me, sibling, src=x_refs[a])]
            first += [copy(a, 1 + j, me, (*chip, mc), src=x_refs[a]) for j, chip in enumerate(chips)]
            for cp in first:
                cp.start()
            sends += first
        for j, chip in enumerate(chips):
            for a in range(n):
                copy(a, 1 + j, (*chip, mc), me).wait_recv()
                passed = copy(a, 4 + j, (*chip, mc), sibling)
                passed.start()
                sends.append(passed)
        for a in range(n):
            copy(a, 0, sibling, me).wait_recv()
            for j, chip in enumerate(chips):
                copy(a, 4 + j, (*chip, 1 - mc), me).wait_recv()
        for cp in sends:
            cp.wait_send()
        for cp in mine:
            cp.wait()

    return _pcall(body, out_shape=[SDS((N_DEV,) + x.shape, x.dtype) for x in arrays], in_specs=[ANY_SPEC] * n,
                  out_specs=[ANY_SPEC] * n,
                  scratch_shapes=[pltpu.SemaphoreType.DMA((7 * n,)), pltpu.SemaphoreType.DMA((7 * n,)),
                                  pltpu.SemaphoreType.DMA((n,))], name=name)(*arrays)


def _swap_with_sibling(arrays, name):
    n = len(arrays)

    def body(*refs):
        g_refs, out_refs = refs[:n], refs[n:2 * n]
        send_sems, recv_sems = refs[2 * n:]
        mx, my, mc = _mesh_pos()
        copies = [pltpu.make_async_remote_copy(
            src_ref=g_refs[a].at[2 * k + (1 - mc)], dst_ref=out_refs[a].at[k], send_sem=send_sems.at[4 * a + k],
            recv_sem=recv_sems.at[4 * a + k], device_id=(mx, my, 1 - mc), device_id_type=MESH_IDS)
            for a in range(n) for k in range(4)]
        for cp in copies:
            cp.start()
        for cp in copies:
            cp.wait()

    return _pcall(body, out_shape=[SDS((4,) + g.shape[1:], g.dtype) for g in arrays], in_specs=[ANY_SPEC] * n,
                  out_specs=[ANY_SPEC] * n,
                  scratch_shapes=[pltpu.SemaphoreType.DMA((4 * n,)), pltpu.SemaphoreType.DMA((4 * n,))],
                  name=name)(*arrays)


HBM_SPEC = pl.BlockSpec(memory_space=pltpu.HBM)
SEM_SPEC = pl.BlockSpec(memory_space=pltpu.SEMAPHORE)
SIDE_EFFECT = pltpu.SideEffectType.DATAFLOW_SIDE_EFFECTING


def _chip_copies(src_refs, dst_refs, send_sems, recv_sems, gather):
    mx, my, mc = _mesh_pos()
    peers = [(1 - mx, my, mc), (mx, 1 - my, mc), (1 - mx, 1 - my, mc)]
    if gather:
        peers = [(mx, my, 1 - mc)] + peers
    copies = []
    for a in range(len(src_refs)):
        for j, peer in enumerate(peers):
            src = src_refs[a] if gather else src_refs[a].at[2 * peer[0] + peer[1]]
            dst = dst_refs[a].at[4 * mx + 2 * my + mc] if gather else dst_refs[a].at[j]
            k = len(peers) * a + j
            copies.append(pltpu.make_async_remote_copy(src_ref=src, dst_ref=dst, send_sem=send_sems.at[k],
                                                       recv_sem=recv_sems.at[k], device_id=peer, device_id_type=MESH_IDS))
    return copies


def _chips_start(srcs, dsts, gather, name):
    n = len(srcs)
    n_sems = (4 if gather else 3) * n
    if dsts is None:
        dsts = [lax.empty((3,) + a.shape[1:], a.dtype) for a in srcs]

    def body(*refs):
        for cp in _chip_copies(refs[:n], refs[n:2 * n], refs[2 * n], refs[2 * n + 1], gather):
            cp.start()
        refs[-1][...] = jnp.zeros_like(refs[-1])

    res = _pcall(
        body, name=name,
        out_shape=(pltpu.SemaphoreType.DMA((n_sems,)), pltpu.SemaphoreType.DMA((n_sems,)),
                   *[pltpu.HBM(a.shape, a.dtype) for a in srcs], *[pltpu.HBM(d.shape, d.dtype) for d in dsts],
                   SDS((8, 128), F32)),
        in_specs=[HBM_SPEC] * (2 * n),
        out_specs=(SEM_SPEC, SEM_SPEC, *[HBM_SPEC] * (2 * n), pl.BlockSpec(memory_space=pltpu.VMEM)),
        input_output_aliases={i: 2 + i for i in range(2 * n)},
        compiler_params=pltpu.CompilerParams(has_side_effects=SIDE_EFFECT),
    )(*[pltpu.with_memory_space_constraint(a, pltpu.HBM) for a in srcs],
      *[pltpu.with_memory_space_constraint(d, pltpu.HBM) for d in dsts])
    return (res[0], res[1], list(res[2:2 + n]), list(res[2 + n:2 + 2 * n])), res[-1]


def _chips_wait(handle, after, gather, name):
    send_sems, recv_sems, srcs, dsts = handle
    n = len(srcs)

    def body(*refs):
        for cp in _chip_copies(refs[:n], refs[n:2 * n], refs[2 * n], refs[2 * n + 1], gather):
            cp.wait_send()
            cp.wait_recv()

    res = _pcall(
        body, name=name, out_shape=tuple(pltpu.HBM(a.shape, a.dtype) for a in srcs + dsts),
        in_specs=[HBM_SPEC] * (2 * n) + [SEM_SPEC, SEM_SPEC, ANY_SPEC], out_specs=tuple([HBM_SPEC] * (2 * n)),
        input_output_aliases={i: i for i in range(2 * n)},
        compiler_params=pltpu.CompilerParams(has_side_effects=SIDE_EFFECT),
    )(*srcs, *dsts, send_sems, recv_sems, after)
    return list(res[:n]), list(res[n:])


def _place_own(xs, me, name):
    n = len(xs)
    n_row_blocks = next(nb for nb in (4, 2, 1) if all(x.shape[0] % (nb * PACK_ROW_ALIGN) == 0 for x in xs))

    def body(me_ref, *refs):
        for a in range(n):
            refs[n + a][...] = refs[a][...]

    def rows(x):
        return x.shape[0] // n_row_blocks
    grid_spec = pltpu.PrefetchScalarGridSpec(
        num_scalar_prefetch=1, grid=(n_row_blocks,),
        in_specs=[pl.BlockSpec((rows(x), x.shape[1]), lambda i, me_ref: (i, 0)) for x in xs],
        out_specs=[pl.BlockSpec((None, rows(x), x.shape[1]), lambda i, me_ref: (me_ref[0], i, 0)) for x in xs])
    return _pcall(body, grid_spec=grid_spec, out_shape=[SDS((N_DEV,) + x.shape, x.dtype) for x in xs], name=name,
                  compiler_params=_params("arbitrary"))(me, *xs)


def _forward_to_sibling(gs, name):
    n = len(gs)

    def body(*refs):
        out_refs = refs[n:2 * n]
        send_sems, recv_sems = refs[2 * n:]
        mx, my, mc = _mesh_pos()
        chips = [(1 - mx, my), (mx, 1 - my), (1 - mx, 1 - my)]
        sends = []
        for a in range(n):
            for j, (px, py) in enumerate(chips):
                slot = out_refs[a].at[4 * px + 2 * py + mc]
                sends.append(pltpu.make_async_remote_copy(
                    src_ref=slot, dst_ref=slot, send_sem=send_sems.at[3 * a + j], recv_sem=recv_sems.at[3 * a + j],
                    device_id=(mx, my, 1 - mc), device_id_type=MESH_IDS))
        for cp in sends:
            cp.start()
        for a in range(n):
            for j, (px, py) in enumerate(chips):
                slot = out_refs[a].at[4 * px + 2 * py + (1 - mc)]
                pltpu.make_async_remote_copy(
                    src_ref=slot, dst_ref=slot, send_sem=send_sems.at[3 * a + j], recv_sem=recv_sems.at[3 * a + j],
                    device_id=(mx, my, 1 - mc), device_id_type=MESH_IDS).wait_recv()
        for cp in sends:
            cp.wait_send()

    return _pcall(body, out_shape=[SDS(g.shape, g.dtype) for g in gs], in_specs=[ANY_SPEC] * n, out_specs=[ANY_SPEC] * n,
                  input_output_aliases={a: a for a in range(n)},
                  scratch_shapes=[pltpu.SemaphoreType.DMA((3 * n,)), pltpu.SemaphoreType.DMA((3 * n,))],
                  name=name)(*gs)


def _add_core_blocks(gs, recvs, core, name):
    n = len(gs)

    def body(core_ref, *refs):
        for a in range(n):
            refs[2 * n + a][...] = (refs[a][...].astype(F32) + refs[n + a][...].astype(F32)).astype(refs[2 * n + a].dtype)

    in_specs = [pl.BlockSpec((None, None) + g.shape[1:], lambda k, core_ref: (k, core_ref[0], 0, 0)) for g in gs]
    in_specs += [pl.BlockSpec((None,) + r.shape[1:], lambda k, core_ref: (k, 0, 0)) for r in recvs]
    grid_spec = pltpu.PrefetchScalarGridSpec(
        num_scalar_prefetch=1, grid=(4,), in_specs=in_specs,
        out_specs=[pl.BlockSpec((None,) + r.shape[1:], lambda k, core_ref: (k, 0, 0)) for r in recvs])
    return _pcall(body, grid_spec=grid_spec, out_shape=[SDS(r.shape, r.dtype) for r in recvs], name=name,
                  compiler_params=_params("arbitrary"))(core, *[g.reshape((4, 2) + g.shape[1:]) for g in gs], *recvs)


def _add_chip_blocks(parts, recvs, chip, name):
    n = len(parts)
    n_row_blocks = next(nb for nb in (4, 2, 1) if all(p.shape[1] % (nb * PACK_ROW_ALIGN) == 0 for p in parts))

    def body(chip_ref, *refs):
        for a in range(n):
            r_ref = refs[n + a]
            refs[2 * n + a][...] = ((refs[a][...].astype(F32) + r_ref[0].astype(F32)) + r_ref[1].astype(F32)
                                    ) + r_ref[2].astype(F32)

    def rows(p):
        return p.shape[1] // n_row_blocks
    in_specs = [pl.BlockSpec((None, rows(p), p.shape[2]), lambda i, chip_ref: (chip_ref[0], i, 0)) for p in parts]
    in_specs += [pl.BlockSpec((3, rows(p), p.shape[2]), lambda i, chip_ref: (0, i, 0)) for p in parts]
    grid_spec = pltpu.PrefetchScalarGridSpec(
        num_scalar_prefetch=1, grid=(n_row_blocks,), in_specs=in_specs,
        out_specs=[pl.BlockSpec((rows(p), p.shape[2]), lambda i, chip_ref: (i, 0)) for p in parts])
    return _pcall(body, grid_spec=grid_spec, out_shape=[SDS(p.shape[1:], F32) for p in parts], name=name,
                  compiler_params=_params("arbitrary"))(chip, *parts, *recvs)


class _Exchange:
    def __init__(self, shards):
        self.shards = shards
        self.core = lax.axis_index("c").astype(jnp.int32).reshape(1)
        self.chip = (2 * lax.axis_index("x") + lax.axis_index("y")).astype(jnp.int32).reshape(1)
        self.me = 2 * self.chip + self.core

    def gather_start(self, i):
        xs = [self.shards[n][_layer_of(n, i)] for n in _layer_weights(i)]
        return _chips_start(xs, _place_own(xs, self.me, f"l{i}_gather_own"), True, f"l{i}_gather_start")

    def gather_finish(self, i, handle, after):
        _, gs = _chips_wait(handle, after, True, f"l{i}_gather_wait")
        out = {}
        for n, g in zip(_layer_weights(i), _forward_to_sibling(gs, f"l{i}_gather_forward")):
            axis = SHARD_AXIS[n] - 1
            out[n] = _Sharded(g[:, None], 0, axis) if n in DIRECT else _join_shards(g, axis)
        return out

    def reduce_start(self, tag, names, gs):
        parts = _add_core_blocks(gs, _swap_with_sibling(gs, tag + "_rs_sibling"), self.core, tag + "_rs_add_core")
        return _chips_start(parts, None, False, tag + "_rs_start")

    def reduce_finish(self, tag, handle, after):
        parts, lands = _chips_wait(handle, after, False, tag + "_rs_wait")
        return _add_chip_blocks(parts, lands, self.chip, tag + "_rs_add_chips")


def _layer_weights(i):
    return MIXER_WEIGHTS[i % N_MIXERS] + LAYER_WEIGHTS


def _layer_of(name, i):
    return i if name in LAYER_WEIGHTS else i // N_MIXERS


def _slab_rows(n):
    return -(-n // (PACK_COLS * PACK_ROW_ALIGN)) * PACK_ROW_ALIGN


def _pack(arrays):
    def slab(a):
        flat = a.reshape(-1)
        rows = _slab_rows(flat.shape[0])
        return jnp.pad(flat, (0, rows * PACK_COLS - flat.shape[0])).reshape(rows, PACK_COLS)
    return jnp.concatenate([slab(a) for a in arrays], axis=0)


def _unpack(slab, shapes, lead=()):
    out, row = [], 0
    for shape in shapes:
        n = math.prod(shape)
        rows = _slab_rows(n)
        part = lax.slice_in_dim(slab, row, row + rows, axis=len(lead)).reshape(lead + (rows * PACK_COLS,))
        out.append(lax.slice_in_dim(part, 0, n, axis=len(lead)).reshape(lead + tuple(shape)))
        row += rows
    return out


def _join_shards(t, axis):
    t = jnp.moveaxis(t, 0, axis)
    return t.reshape(t.shape[:axis] + (t.shape[axis] * t.shape[axis + 1],) + t.shape[axis + 2:])


def _split_shards(full, axis):
    s = full.shape
    t = full.reshape(s[:axis] + (N_DEV, s[axis] // N_DEV) + s[axis + 1:])
    return jnp.moveaxis(t, axis, 0)


def _weight_grad(lhs, rhs, name, tag):
    axis = SHARD_AXIS[name] - 1
    if name in DIRECT:
        return _mm(lhs, rhs, "tn", [BF16], name=tag, shard_out=axis)
    return _split_shards(_mm(lhs, rhs, "tn", [F32], name=tag), axis).astype(BF16)


def _attention_fwd(u, h, w, wl, j, tag):
    t, d = u.shape
    n_heads = d // HEAD_DIM
    n_kv = n_heads // Q_PER_KV
    qkv = _mm(u, wl["a_wqkv"], "nn", [F32], name=tag + "_qkv")
    q = qkv[:, :d].reshape(t, n_kv, Q_PER_KV, HEAD_DIM).transpose(1, 2, 0, 3)
    k = qkv[:, d:d + n_kv * HEAD_DIM].reshape(t, n_kv, HEAD_DIM).transpose(1, 0, 2)
    v = qkv[:, d + n_kv * HEAD_DIM:].reshape(t, n_kv, HEAD_DIM).transpose(1, 0, 2)
    gq, gk, sinks = w["a_q_norm_g"][j][None], w["a_k_norm_g"][j][None], w["a_sinks"][j]
    o = _attn_fwd(q, k, v, gq, gk, sinks, tag + "_attn")
    o2 = o.transpose(2, 0, 1, 3).reshape(t, d)
    h1 = _mm(o2, wl["a_wo"], "nn", [F32], name=tag + "_wo", epi=lambda acc, h: acc + h, rows=[h])
    return h1, (q, k, v, o2)


def _attention_bwd(ctx, u, dh1, dh1b, w, wl, j, tag, grads, big):
    q, k, v, o2 = ctx
    t, d = u.shape
    n_kv = q.shape[0]
    gq, gk, sinks = w["a_q_norm_g"][j][None], w["a_k_norm_g"][j][None], w["a_sinks"][j]
    big.append(("a_wo", j, _weight_grad(o2, dh1b, "a_wo", tag + "_dwo")))
    do2 = _mm(dh1b, wl["a_wo"], "nt", [F32], name=tag + "_do")
    do = do2.reshape(t, n_kv, Q_PER_KV, HEAD_DIM).transpose(1, 2, 0, 3)
    dq, dkp, dkc, dvp, dvc, dgq, dgk, dsink = _attn_bwd(q, k, v, do, gq, gk, sinks, tag + "_attn_bwd")

    def fold(prev_part, cur_part):
        shifted = jnp.concatenate([prev_part[:, ATTN_BLOCK:], jnp.zeros_like(prev_part[:, :ATTN_BLOCK])], axis=1)
        both = _add(shifted.reshape(n_kv * t, HEAD_DIM), cur_part.reshape(n_kv * t, HEAD_DIM), tag + "_fold")
        return both.reshape(n_kv, t, HEAD_DIM).transpose(1, 0, 2).reshape(t, n_kv * HEAD_DIM)
    dqkv = jnp.concatenate([dq.transpose(2, 0, 1, 3).reshape(t, d), fold(dkp, dkc), fold(dvp, dvc)], axis=1).astype(BF16)
    grads["a_q_norm_g"][j] = dgq[0]
    grads["a_k_norm_g"][j] = dgk[0]
    grads["a_sinks"][j] = dsink[:, :, 0].reshape(-1)
    big.append(("a_wqkv", j, _weight_grad(u, dqkv, "a_wqkv", tag + "_dwqkv")))
    return _mm(dqkv, wl["a_wqkv"], "nt", [F32], name=tag + "_du")


def _conformer_fwd(u, h, w, wl, j, tag):
    t, d = u.shape
    a = _mm(u, wl["b_w_pw1"], "nn", [F32], name=tag + "_pw1", epi=lambda acc, b: acc + b, pars=[w["b_b_pw1"][j][None]])
    y = _rowwise(lambda val, gate: val * _sigmoid(gate), [(a, d, 0), (a, d, 1)], [], [(d, F32)], name=tag + "_glu")[0]
    y2 = _dwconv_fwd(y, w["b_w_dw"][j], w["b_b_dw"][j][None], False, tag + "_dw")[0]

    def ln_silu(y2, g, b):
        mu = _rowmean(y2)
        yc = y2 - mu
        y3 = yc * lax.rsqrt(_rowmean(yc * yc) + EPS) * g + b
        return y3 * _sigmoid(y3)
    y4 = _rowwise(ln_silu, [y2], [w["b_ln_g"][j][None], w["b_ln_b"][j][None]], [(d, BF16)], name=tag + "_ln")[0]
    h1 = _mm(y4, wl["b_w_pw2"], "nn", [F32], name=tag + "_pw2", epi=lambda acc, h, b: acc + h + b, rows=[h],
             pars=[w["b_b_pw2"][j][None]])
    return h1, (a, y, y2, y4)


def _conformer_bwd(ctx, u, dh1, dh1b, dh1_colsum, w, wl, j, tag, grads, big):
    a, y, y2, y4 = ctx
    t, d = u.shape
    big.append(("b_w_pw2", j, _weight_grad(y4, dh1b, "b_w_pw2", tag + "_dwpw2")))
    grads["b_b_pw2"][j] = dh1_colsum[0]
    dy4 = _mm(dh1b, wl["b_w_pw2"], "nt", [F32], name=tag + "_dy4")

    def ln_silu_bwd(dy4, y2, g, b):
        mu = _rowmean(y2)
        yc = y2 - mu
        rstd = lax.rsqrt(_rowmean(yc * yc) + EPS)
        xh = yc * rstd
        dy3 = dy4 * _dsilu(xh * g + b)
        dxh = dy3 * g
        return rstd * (dxh - _rowmean(dxh) - xh * _rowmean(dxh * xh)), _colsum(dy3 * xh), _colsum(dy3)
    dy2, dlg, dlb = _rowwise(ln_silu_bwd, [dy4, y2], [w["b_ln_g"][j][None], w["b_ln_b"][j][None]], [(d, F32)], [d, d],
                             name=tag + "_ln_bwd")
    grads["b_ln_g"][j], grads["b_ln_b"][j] = dlg[0], dlb[0]
    dy, dw_dw, db_dw = _dwconv_bwd(y, dy2, w["b_w_dw"][j], tag + "_dw_bwd")
    grads["b_w_dw"][j], grads["b_b_dw"][j] = dw_dw, db_dw[0]

    def glu_bwd(dy, val, gate):
        s = _sigmoid(gate)
        dval, dgate = dy * s, dy * val * s * (1.0 - s)
        return dval, dgate, _colsum(dval), _colsum(dgate)
    dval, dgate, dbv, dbg = _rowwise(glu_bwd, [dy, (a, d, 0), (a, d, 1)], [], [(d, BF16), (d, BF16)], [d, d],
                                     name=tag + "_glu_bwd")
    da = jnp.concatenate([dval, dgate], axis=1)
    grads["b_b_pw1"][j] = jnp.concatenate([dbv[0], dbg[0]])
    big.append(("b_w_pw1", j, _weight_grad(u, da, "b_w_pw1", tag + "_dwpw1")))
    return _mm(da, wl["b_w_pw1"], "nt", [F32], name=tag + "_du")


def _ssd_layouts(v, n_groups):
    t = v.shape[0]
    vg = v[:, :n_groups * SSM_R].reshape(t, n_groups, SSM_R).transpose(1, 0, 2)
    return vg, vg.reshape(n_groups, t // SSM_L, SSM_L, SSM_R).transpose(0, 1, 3, 2)


def _from_group_layout(vg, width):
    n_groups, t, _ = vg.shape
    v = vg.transpose(1, 0, 2).reshape(t, n_groups * SSM_R)
    return jnp.pad(v, ((0, 0), (0, width - n_groups * SSM_R)))


def _pad_lanes(v, width):
    return jnp.pad(v, [(0, 0)] * (v.ndim - 1) + [(0, width - v.shape[-1])])


def _mamba_weights(wl, d_inner, conv_c):
    w_in = wl["c_w_in"]
    n_heads = w_in.shape[1] - d_inner - conv_c
    return (w_in[:, :d_inner], w_in[:, d_inner:d_inner + conv_c], _pad_lanes(w_in[:, d_inner + conv_c:], 128), n_heads)


def _mamba_fwd(u, h, w, wl, j, tag):
    t, d = u.shape
    d_inner = w["c_D"].shape[1] * SSM_P
    conv_c = w["c_w_conv"][j].shape[1]
    w_z, w_xbc, w_dt, n_heads = _mamba_weights(wl, d_inner, conv_c)
    n_groups = n_heads // SSM_R
    z = _mm(u, w_z, "nn", [F32], name=tag + "_in_z")
    xbc_raw = _mm(u, w_xbc, "nn", [F32], name=tag + "_in_xbc")
    dt_raw = _mm(u, w_dt, "nn", [F32], name=tag + "_in_dt")
    pre, xbc = _dwconv_fwd(xbc_raw, w["c_w_conv"][j], w["c_b_conv"][j][None], True, tag + "_conv")
    bias, a_log = _pad_lanes(w["c_dt_bias"][j][None], 128), _pad_lanes(w["c_A_log"][j][None], 128)
    dt, cs = _dt_fwd(dt_raw, bias, a_log, tag + "_dt")
    dt_g, _ = _ssd_layouts(dt, n_groups)
    cs_g, cs_row = _ssd_layouts(cs, n_groups)
    y, prev = _ssd_fwd(xbc, dt_g, cs_g, cs_row, d_inner, tag + "_ssd")
    d_cols = jnp.repeat(w["c_D"][j], SSM_P)[None]
    g = w["c_norm_g"][j][None]

    def gated_norm(y, xs, z, d_cols, g):
        yg = (y + xs * d_cols) * (z * _sigmoid(z))
        return yg * lax.rsqrt(_rowmean(yg * yg) + EPS) * g
    yn = _rowwise(gated_norm, [y, (xbc, d_inner, 0), z], [d_cols, g], [(d_inner, BF16)], name=tag + "_gnorm")[0]
    h1 = _mm(yn, wl["c_w_out"], "nn", [F32], name=tag + "_out", epi=lambda acc, h: acc + h, rows=[h])
    return h1, (z, xbc_raw, dt_raw, pre, xbc, dt, dt_g, cs_g, cs_row, y, prev, yn)


def _mamba_bwd(ctx, u, dh1, dh1b, w, wl, j, tag, grads, big):
    z, xbc_raw, dt_raw, pre, xbc, dt, dt_g, cs_g, cs_row, y, prev, yn = ctx
    t, d = u.shape
    d_inner = w["c_D"].shape[1] * SSM_P
    conv_c = w["c_w_conv"][j].shape[1]
    w_z, w_xbc, w_dt, n_heads = _mamba_weights(wl, d_inner, conv_c)
    n_groups = n_heads // SSM_R
    big.append(("c_w_out", j, _weight_grad(yn, dh1b, "c_w_out", tag + "_dwout")))
    dyn = _mm(dh1b, wl["c_w_out"], "nt", [F32], name=tag + "_dyn")
    d_cols = jnp.repeat(w["c_D"][j], SSM_P)[None]
    g = w["c_norm_g"][j][None]

    def gated_norm_bwd(dyn, y, xs, z, d_cols, g):
        sg = _sigmoid(z)
        yt = y + xs * d_cols
        yg = yt * (z * sg)
        rstd = lax.rsqrt(_rowmean(yg * yg) + EPS)
        xh = yg * rstd
        dxh = dyn * g
        dyg = rstd * (dxh - xh * _rowmean(dxh * xh))
        dyt = dyg * (z * sg)
        return dyt, dyg * yt * (sg * (1.0 + z * (1.0 - sg))), dyt * d_cols, _colsum(dyn * xh), _colsum(dyt * xs)
    dyt, dz, dxs_skip, dg, dd_cols = _rowwise(
        gated_norm_bwd, [dyn, y, (xbc, d_inner, 0), z], [d_cols, g], [(d_inner, F32), (d_inner, BF16), (d_inner, F32)],
        [d_inner, d_inner], name=tag + "_gnorm_bwd")
    grads["c_norm_g"][j] = dg[0]
    grads["c_D"][j] = dd_cols.reshape(n_heads, SSM_P).sum(axis=1)
    dx, db, dc, ddtx_g, dda_g = _ssd_bwd(xbc, dt_g, cs_g, cs_row, prev, dyt, d_inner, tag + "_ssd_bwd")
    dxs = _add(dx, dxs_skip, tag + "_dxs")
    dpost = jnp.concatenate([dxs, db, dc], axis=1)
    dpre = _rowwise(lambda dpost, pre: dpost * _dsilu(pre), [dpost, pre], [], [(conv_c, F32)], name=tag + "_silu_bwd")[0]
    dxbc, dw_conv, db_conv = _dwconv_bwd(xbc_raw, dpre, w["c_w_conv"][j], tag + "_conv_bwd")
    grads["c_w_conv"][j], grads["c_b_conv"][j] = dw_conv, db_conv[0]
    bias, a_log = _pad_lanes(w["c_dt_bias"][j][None], 128), _pad_lanes(w["c_A_log"][j][None], 128)
    ddt_raw, dbias, da_log = _dt_bwd(_from_group_layout(ddtx_g, 128), _from_group_layout(dda_g, 128), dt_raw, dt, bias,
                                     a_log, tag + "_dt_bwd")
    grads["c_dt_bias"][j], grads["c_A_log"][j] = dbias[0, :n_heads], da_log[0, :n_heads]
    dxbc_b = dxbc.astype(BF16)
    dw_in = jnp.concatenate(
        [_mm(u, dz, "tn", [F32], name=tag + "_dwin_z"), _mm(u, dxbc_b, "tn", [F32], name=tag + "_dwin_xbc"),
         _mm(u, ddt_raw, "tn", [F32], name=tag + "_dwin_dt")[:, :n_heads]], axis=1)
    big.append(("c_w_in", j, _split_shards(dw_in, SHARD_AXIS["c_w_in"] - 1).astype(BF16)))
    du = _mm(dz, w_z, "nt", [F32], name=tag + "_du_z")
    du = _mm(dxbc_b, w_xbc, "nt", [F32], name=tag + "_du_xbc", epi=lambda acc, r: acc + r, rows=[du])
    return _mm(ddt_raw, w_dt, "nt", [F32], name=tag + "_du_dt", epi=lambda acc, r: acc + r, rows=[du])


def _local_step(x, p, tgt, w, comm):
    depth = p.shape[0]
    grads = {name: {} for name in REPLICATED + tuple(n for n, _ in SMALL_SHARDED)}
    reduced = {name: {} for name, _ in BIG_SHARDED}
    saved = []
    h = x
    handle, token = comm.gather_start(0)
    wl = comm.gather_finish(0, handle, h)
    for i in range(depth):
        kind, j, tag = i % N_MIXERS, i // N_MIXERS, f"l{i}"
        token = None
        if i + 1 < depth:
            handle, token = comm.gather_start(i + 1)
        u = _rms_fwd(h, w["mix_norm_g"][i][None], tag + "_rms_mix", after=token)
        fwd = (_attention_fwd, _conformer_fwd, _mamba_fwd)[kind]
        h1, ctx = fwd(u, h, w, wl, j, tag)
        u2 = _rms_fwd(h1, w["mlp_norm_g"][i][None], tag + "_rms_mlp")
        a, r = _mm(u2, wl["m_w1"], "nn", [F32, BF16], name=tag + "_w1",
                   epi=lambda acc: (acc, jnp.square(jnp.maximum(acc, 0.0))))
        h2 = _mm(r, wl["m_w2"], "nn", [F32], name=tag + "_w2", epi=lambda acc, h: acc + h, rows=[h1])
        u3 = _rms_fwd(h2, w["ple_norm_g"][i][None], tag + "_rms_ple")
        gl = _mm(u3, wl["ple_w_gate"], "nn", [F32], name=tag + "_gate")
        pp = _mm(p[i], wl["ple_w_proj"], "nn", [F32], name=tag + "_proj")
        h3 = _ple_fwd(h2, gl, pp, tag + "_ple")
        saved.append((h, u, ctx, h1, u2, a, r, h2, u3, gl, pp, wl))
        h = h3
        if i + 1 < depth:
            wl = comm.gather_finish(i + 1, handle, h)

    dh, sq_err = _loss_and_grad(h, tgt, "loss")
    pending, token = None, None
    for i in reversed(range(depth)):
        kind, j, tag = i % N_MIXERS, i // N_MIXERS, f"l{i}"
        h0, u, ctx, h1, u2, a, r, h2, u3, gl, pp, wl = saved[i]
        big = []
        dgl, dpp = _ple_bwd(dh, gl, pp, tag + "_ple_bwd", after=token)
        big.append(("ple_w_proj", i, _weight_grad(p[i], dpp, "ple_w_proj", tag + "_dwproj")))
        big.append(("ple_w_gate", i, _weight_grad(u3, dgl, "ple_w_gate", tag + "_dwgate")))
        du3 = _mm(dgl, wl["ple_w_gate"], "nt", [F32], name=tag + "_du3")
        dh2, dh2b, dg, _ = _rms_bwd(du3, h2, w["ple_norm_g"][i][None], dh, tag + "_rms_ple_bwd")
        grads["ple_norm_g"][i] = dg[0]
        big.append(("m_w2", i, _weight_grad(r, dh2b, "m_w2", tag + "_dw2")))
        da = _mm(dh2b, wl["m_w2"], "nt", [BF16], name=tag + "_da",
                 epi=lambda acc, a: acc * (2.0 * jnp.maximum(a, 0.0)), rows=[a])
        big.append(("m_w1", i, _weight_grad(u2, da, "m_w1", tag + "_dw1")))
        du2 = _mm(da, wl["m_w1"], "nt", [F32], name=tag + "_du2")
        dh1, dh1b, dg, dh1_colsum = _rms_bwd(du2, h1, w["mlp_norm_g"][i][None], dh2, tag + "_rms_mlp_bwd")
        grads["mlp_norm_g"][i] = dg[0]
        if kind == 0:
            du = _attention_bwd(ctx, u, dh1, dh1b, w, wl, j, tag, grads, big)
        elif kind == 1:
            du = _conformer_bwd(ctx, u, dh1, dh1b, dh1_colsum, w, wl, j, tag, grads, big)
        else:
            du = _mamba_bwd(ctx, u, dh1, dh1b, w, wl, j, tag, grads, big)
        dh, _, dg, _ = _rms_bwd(du, h0, w["mix_norm_g"][i][None], dh1, tag + "_rms_mix_bwd")
        grads["mix_norm_g"][i] = dg[0]
        if pending is not None:
            _finish_reduce(comm, pending, dh, reduced)
        handle, token = comm.reduce_start(tag, [name for name, _, _ in big], [g for _, _, g in big])
        pending = (tag, handle, [(name, layer) for name, layer, _ in big])

    def stack(d):
        return {name: jnp.stack([g[k] for k in sorted(g)]) for name, g in d.items()}
    small = stack(grads)
    _finish_reduce(comm, pending, small["mix_norm_g"], reduced)
    return sq_err, dh, stack(reduced), small


def _finish_reduce(comm, pending, after, reduced):
    tag, handle, keys = pending
    for (name, layer), g in zip(keys, comm.reduce_finish(tag, handle, after)):
        reduced[name][layer] = g


def kernel(x, p, mix_norm_g, mlp_norm_g, ple_norm_g, a_wqkv, a_q_norm_g, a_k_norm_g, a_sinks, a_wo, b_w_pw1, b_b_pw1, b_w_dw, b_b_dw, b_ln_g, b_ln_b, b_w_pw2, b_b_pw2, c_w_in, c_w_conv, c_b_conv, c_dt_bias, c_A_log, c_D, c_norm_g, c_w_out, m_w1, m_w2, ple_w_proj, ple_w_gate, loss_target, m_mix_norm_g, m_mlp_norm_g, m_ple_norm_g, m_a_wqkv, m_a_q_norm_g, m_a_k_norm_g, m_a_sinks, m_a_wo, m_b_w_pw1, m_b_b_pw1, m_b_w_dw, m_b_b_dw, m_b_ln_g, m_b_ln_b, m_b_w_pw2, m_b_b_pw2, m_c_w_in, m_c_w_conv, m_c_b_conv, m_c_dt_bias, m_c_A_log, m_c_D, m_c_norm_g, m_c_w_out, m_m_w1, m_m_w2, m_ple_w_proj, m_ple_w_gate, v_mix_norm_g, v_mlp_norm_g, v_ple_norm_g, v_a_wqkv, v_a_q_norm_g, v_a_k_norm_g, v_a_sinks, v_a_wo, v_b_w_pw1, v_b_b_pw1, v_b_w_dw, v_b_b_dw, v_b_ln_g, v_b_ln_b, v_b_w_pw2, v_b_b_pw2, v_c_w_in, v_c_w_conv, v_c_b_conv, v_c_dt_bias, v_c_A_log, v_c_D, v_c_norm_g, v_c_w_out, v_m_w1, v_m_w2, v_ple_w_proj, v_ple_w_gate):
    args = dict(locals())
    wl = {n: args[n] for n in WEIGHTS}
    ml = {n: args["m_" + n] for n in WEIGHTS}
    vl = {n: args["v_" + n] for n in WEIGHTS}
    d_model = x.shape[-1]
    small_sharded = [n for n, _ in SMALL_SHARDED]

    whole = {n: wl[n] for n in REPLICATED}
    small = _all_gather([_pack([wl[n] for n in small_sharded])], "gather_small")[0]
    for (n, axis), t in zip(SMALL_SHARDED, _unpack(small, [wl[n].shape for n in small_sharded], (N_DEV,))):
        whole[n] = _join_shards(t, axis)
    comm = _Exchange({n: wl[n].astype(BF16) for n, _ in BIG_SHARDED})
    sq_err, grad_x, g_big, g_small = _local_step(x[0], p[:, 0], loss_target[0], whole, comm)
    loss = lax.psum(0.5 * jnp.sum(sq_err) / d_model, ("x", "y", "c"))

    small_names = list(REPLICATED) + small_sharded
    summed = _sum_blocks(_all_gather([_pack([g_small[n] for n in small_names])], "gather_small_grads")[0],
                         "sum_small_grads")
    g_small = dict(zip(small_names, _unpack(summed, [g_small[n].shape for n in small_names])))
    me = 4 * lax.axis_index("x") + 2 * lax.axis_index("y") + lax.axis_index("c")
    for n, axis in SMALL_SHARDED:
        size = wl[n].shape[axis]
        g_small[n] = lax.dynamic_slice_in_dim(g_small[n], me * size, size, axis)

    out = {}
    for n, _ in BIG_SHARDED:
        shape = wl[n].shape
        two_d = (shape[0] * shape[1], shape[2])
        delta, new_m, new_v = _adamw(wl[n].reshape(two_d), g_big[n].reshape(two_d), ml[n].reshape(two_d),
                                     vl[n].reshape(two_d), "adamw_" + n)
        out.update({"grad_" + n: g_big[n], "delta_" + n: delta.reshape(shape), "new_m_" + n: new_m.reshape(shape),
                    "new_v_" + n: new_v.reshape(shape)})
    g_slab = _pack([g_small[n] for n in small_names])
    delta, new_m, new_v = _adamw(_pack([wl[n] for n in small_names]), g_slab, _pack([ml[n] for n in small_names]),
                                 _pack([vl[n] for n in small_names]), "adamw_small")
    for kind, slab in (("delta", delta), ("new_m", new_m), ("new_v", new_v)):
        out.update({kind + "_" + n: a for n, a in zip(small_names, _unpack(slab, [wl[n].shape for n in small_names]))})
    out.update({"grad_" + n: g_small[n] for n in small_names})
    return (loss, grad_x[None], *[out[k + "_" + n] for k in ("grad", "delta", "new_m", "new_v") for n in WEIGHTS])
```

```python
import math
from typing import Any, NamedTuple

import jax
import jax.numpy as jnp
from jax import lax
from jax.experimental import pallas as pl
from jax.experimental.pallas import tpu as pltpu

F32 = jnp.float32
BF16 = jnp.bfloat16
SDS = jax.ShapeDtypeStruct

EPS = 1e-6
N_MIXERS = 3
HEAD_DIM = 64
Q_PER_KV = 4
ATTN_BLOCK = 128
SSM_P = 64
SSM_R = 4
SSM_N = 128
SSM_L = 128
ADAM_LR, ADAM_B1, ADAM_B2, ADAM_EPS, ADAM_WD, ADAM_STEP = 0.001, 0.9, 0.999, 1e-08, 0.01, 10
N_DEV = 8
PACK_COLS = 1024
PACK_ROW_ALIGN = 16
VMEM_LIMIT = 56 * 1024 * 1024
MM_VMEM_BUDGET = 40 * 1024 * 1024
ROW_VMEM_BUDGET = 24 * 1024 * 1024
CONV_PAD = 32
CONV_ROWS = 64
NEG_BIG = -1e30

BIG_SHARDED = (("a_wqkv", 2), ("a_wo", 1), ("b_w_pw1", 2), ("b_w_pw2", 1), ("c_w_in", 2), ("c_w_out", 1),
               ("m_w1", 2), ("m_w2", 1), ("ple_w_proj", 2), ("ple_w_gate", 1))
SMALL_SHARDED = (("b_w_dw", 2), ("c_w_conv", 2), ("c_b_conv", 1), ("c_norm_g", 1))
REPLICATED = ("mix_norm_g", "mlp_norm_g", "ple_norm_g", "a_q_norm_g", "a_k_norm_g", "a_sinks", "b_b_pw1", "b_b_dw",
              "b_ln_g", "b_ln_b", "b_b_pw2", "c_dt_bias", "c_A_log", "c_D")
DIRECT = frozenset(("a_wo", "b_w_pw1", "b_w_pw2", "c_w_out", "m_w1", "m_w2", "ple_w_proj", "ple_w_gate"))
SHARD_AXIS = dict(BIG_SHARDED + SMALL_SHARDED)
MIXER_WEIGHTS = (("a_wqkv", "a_wo"), ("b_w_pw1", "b_w_pw2"), ("c_w_in", "c_w_out"))
LAYER_WEIGHTS = ("m_w1", "m_w2", "ple_w_proj", "ple_w_gate")
WEIGHTS = ("mix_norm_g", "mlp_norm_g", "ple_norm_g", "a_wqkv", "a_q_norm_g", "a_k_norm_g", "a_sinks", "a_wo",
           "b_w_pw1", "b_b_pw1", "b_w_dw", "b_b_dw", "b_ln_g", "b_ln_b", "b_w_pw2", "b_b_pw2", "c_w_in", "c_w_conv",
           "c_b_conv", "c_dt_bias", "c_A_log", "c_D", "c_norm_g", "c_w_out", "m_w1", "m_w2", "ple_w_proj",
           "ple_w_gate")


def _pcall(body, **kw):
    return pl.pallas_call(body, **kw)


def _params(*sem):
    return pltpu.CompilerParams(dimension_semantics=sem, vmem_limit_bytes=VMEM_LIMIT)


def _sigmoid(x):
    return 1.0 / (1.0 + jnp.exp(-x))


def _dsilu(x):
    s = _sigmoid(x)
    return s * (1.0 + x * (1.0 - s))


def _colsum(x):
    return jnp.sum(x, axis=0, keepdims=True)


def _rowmean(x):
    return jnp.mean(x, axis=-1, keepdims=True)


def _rowwise(fn, rows, pars, row_outs, acc_outs=(), *, name, tm=None):
    rows = [r if isinstance(r, tuple) else (r, r.shape[1], 0) for r in rows]
    t = rows[0][0].shape[0]
    per_row = sum(w * a.dtype.itemsize for a, w, _ in rows) + sum(w * jnp.dtype(d).itemsize for w, d in row_outs)
    if tm is None:
        tm = 1024
        while tm > 8 and (t % tm or 2 * tm * per_row > ROW_VMEM_BUDGET):
            tm //= 2
    assert t % tm == 0, (name, t, tm)
    n_in, n_row = len(rows) + len(pars), len(row_outs)

    def body(*refs):
        outs = fn(*[r[...] for r in refs[:n_in]])
        outs = outs if isinstance(outs, tuple) else (outs,)
        for ref, o in zip(refs[n_in:n_in + n_row], outs[:n_row]):
            ref[...] = o.astype(ref.dtype)
        acc_refs = refs[n_in + n_row:]
        if acc_refs:
            @pl.when(pl.program_id(0) == 0)
            def _():
                for ref in acc_refs:
                    ref[...] = jnp.zeros_like(ref)
            for ref, o in zip(acc_refs, outs[n_row:]):
                ref[...] += o

    in_specs = [pl.BlockSpec((tm, w), lambda i, cb=cb: (i, cb)) for _, w, cb in rows]
    in_specs += [pl.BlockSpec(p.shape, lambda i: (0, 0)) for p in pars]
    out_specs = [pl.BlockSpec((tm, w), lambda i: (i, 0)) for w, _ in row_outs]
    out_specs += [pl.BlockSpec((1, w), lambda i: (0, 0)) for w in acc_outs]
    out_shape = [SDS((t, w), d) for w, d in row_outs] + [SDS((1, w), F32) for w in acc_outs]
    return _pcall(body, grid=(t // tm,), in_specs=in_specs, out_specs=out_specs, out_shape=out_shape, name=name,
                  compiler_params=_params("arbitrary"))(*[a for a, _, _ in rows], *pars)


_DOT_DIMS = {"nn": (((1,), (0,)), ((), ())), "nt": (((1,), (1,)), ((), ())), "tn": (((0,), (0,)), ((), ()))}


class _Sharded(NamedTuple):
    g: Any
    layer: int
    axis: int


def _mm(a, b, form, out_dtypes, *, name, epi=None, rows=(), pars=(), shard_out=None):
    sharded = isinstance(b, _Sharded)
    if sharded:
        wa, wb = b.g.shape[2:]
        b_shape = (N_DEV * wa, wb) if b.axis == 0 else (wa, N_DEV * wb)
        b_item, layer = b.g.dtype.itemsize, b.layer
    else:
        b_shape, b_item = b.shape, b.dtype.itemsize
    if form == "tn":
        k, m = a.shape
        n = b_shape[1]
    else:
        m, k = a.shape
        n = b_shape[0] if form == "nt" else b_shape[1]
    rows = [r if isinstance(r, tuple) else (r, 0) for r in rows]
    tn = next((c for c in (512, 256, 128) if n % c == 0), n)
    b_shards = tn // wb if sharded and b.axis == 1 and form == "nn" and wb < tn else 1
    out_rows, out_cols = (m // N_DEV if shard_out == 0 else None), (n // N_DEV if shard_out == 1 else None)
    if out_cols is not None and out_cols > tn:
        tn = next(c for c in (512, 256, 128) if out_cols % c == 0)

    def need(tm_):
        blk = tm_ * k * a.dtype.itemsize + tn * k * b_item
        blk += sum(tm_ * tn * jnp.dtype(d).itemsize for d in out_dtypes) + sum(tm_ * tn * r.dtype.itemsize for r, _ in rows)
        return 2 * blk

    tm = next((c for c in (1024, 512, 256, 128, 64, 32, 16) if m % c == 0 and need(c) <= MM_VMEM_BUDGET
               and (out_rows is None or out_rows % c == 0 or c % out_rows == 0)), None)
    assert tm is not None, (name, m, k, n)
    nx = len(rows) + len(pars)
    pieces = sharded and form == "nt" and b.axis == 1

    def body(*refs):
        av = refs[0][...].astype(BF16)
        if pieces:
            acc = jnp.zeros((tm, tn), F32)
            for d in range(N_DEV):
                acc += lax.dot_general(av[:, d * wb:(d + 1) * wb], refs[1][d], _DOT_DIMS["nt"], preferred_element_type=F32)
        else:
            bv = refs[1][...]
            if b_shards > 1:
                acc = jnp.concatenate([jnp.dot(av, bv[d], preferred_element_type=F32) for d in range(b_shards)], axis=1)
            else:
                if sharded and bv.ndim == 3:
                    bv = bv.reshape(bv.shape[0] * bv.shape[1], bv.shape[2])
                acc = lax.dot_general(av, bv.astype(BF16), _DOT_DIMS[form], preferred_element_type=F32)
        outs = epi(acc, *[r[...] for r in refs[2:2 + nx]]) if epi else acc
        outs = outs if isinstance(outs, tuple) else (outs,)
        for ref, o in zip(refs[2 + nx:], outs):
            if out_rows is not None and tm > out_rows:
                ref[...] = o.reshape(tm // out_rows, out_rows, tn).astype(ref.dtype)
            elif out_cols is not None and tn > out_cols:
                for d in range(tn // out_cols):
                    ref[d] = o[:, d * out_cols:(d + 1) * out_cols].astype(ref.dtype)
            else:
                ref[...] = o.astype(ref.dtype)

    a_spec = pl.BlockSpec((k, tm), lambda i, j: (0, i)) if form == "tn" else pl.BlockSpec((tm, k), lambda i, j: (i, 0))
    if not sharded:
        b_spec = pl.BlockSpec((tn, k), lambda i, j: (j, 0)) if form == "nt" else pl.BlockSpec((k, tn), lambda i, j: (0, j))
    elif form == "nn" and b.axis == 0:
        b_spec = pl.BlockSpec((N_DEV, None, wa, tn), lambda i, j: (0, layer, 0, j))
    elif form == "nn" and b_shards > 1:
        b_spec = pl.BlockSpec((b_shards, None, wa, wb), lambda i, j: (j, layer, 0, 0))
    elif form == "nn":
        per = wb // tn
        b_spec = pl.BlockSpec((None, None, wa, tn), lambda i, j: (j // per, layer, 0, j % per))
    elif form == "nt" and b.axis == 0 and tn >= wa:
        b_spec = pl.BlockSpec((tn // wa, None, wa, k), lambda i, j: (j, layer, 0, 0))
    elif form == "nt" and b.axis == 0:
        per = wa // tn
        b_spec = pl.BlockSpec((None, None, tn, k), lambda i, j: (j // per, layer, j % per, 0))
    else:
        assert form == "nt", (name, form)
        b_spec = pl.BlockSpec((N_DEV, None, tn, wb), lambda i, j: (0, layer, j, 0))
    in_specs = [a_spec, b_spec]
    in_specs += [pl.BlockSpec((tm, tn), lambda i, j, off=off: (i, j + off)) for _, off in rows]
    in_specs += [pl.BlockSpec((1, tn), lambda i, j: (0, j)) for _ in pars]
    if shard_out == 0 and tm > out_rows:
        out_specs = [pl.BlockSpec((tm // out_rows, out_rows, tn), lambda i, j: (i, 0, j)) for _ in out_dtypes]
        out_shape = [SDS((N_DEV, out_rows, n), d) for d in out_dtypes]
    elif shard_out == 0:
        per_m = out_rows // tm
        out_specs = [pl.BlockSpec((None, tm, tn), lambda i, j: (i // per_m, i % per_m, j)) for _ in out_dtypes]
        out_shape = [SDS((N_DEV, out_rows, n), d) for d in out_dtypes]
    elif shard_out == 1 and tn > out_cols:
        out_specs = [pl.BlockSpec((tn // out_cols, tm, out_cols), lambda i, j: (j, i, 0)) for _ in out_dtypes]
        out_shape = [SDS((N_DEV, m, out_cols), d) for d in out_dtypes]
    elif shard_out == 1:
        per_n = out_cols // tn
        out_specs = [pl.BlockSpec((None, tm, tn), lambda i, j: (j // per_n, i, j % per_n)) for _ in out_dtypes]
        out_shape = [SDS((N_DEV, m, out_cols), d) for d in out_dtypes]
    else:
        out_specs = [pl.BlockSpec((tm, tn), lambda i, j: (i, j)) for _ in out_dtypes]
        out_shape = [SDS((m, n), d) for d in out_dtypes]
    res = _pcall(body, grid=(m // tm, n // tn), in_specs=in_specs, out_specs=out_specs, out_shape=out_shape, name=name,
                 compiler_params=_params("arbitrary", "arbitrary"))(a, b.g if sharded else b, *[r for r, _ in rows], *pars)
    return res[0] if len(out_dtypes) == 1 else res


def _rms_fwd(x, g, name, after=None):
    def fn(x, g, *_):
        return x * lax.rsqrt(_rowmean(x * x) + EPS) * g
    return _rowwise(fn, [x], [g] + ([] if after is None else [after]), [(x.shape[1], BF16)], name=name)[0]


def _rms_bwd(dy, x, g, dres, name):
    def fn(dy, x, dres, g):
        rstd = lax.rsqrt(_rowmean(x * x) + EPS)
        xh = x * rstd
        dxh = dy * g
        tot = dres + rstd * (dxh - xh * _rowmean(dxh * xh))
        return tot, tot, _colsum(dy * xh), _colsum(tot)
    d = x.shape[1]
    return _rowwise(fn, [dy, x, dres], [g], [(d, F32), (d, BF16)], [d, d], name=name)


def _ple_bwd(dh, gl, pp, name, after=None):
    def fn(dh, gl, pp, *_):
        gate = _sigmoid(gl)
        return dh * pp.astype(F32) * gate * (1.0 - gate), dh * gate
    d = dh.shape[1]
    return _rowwise(fn, [dh, gl, pp], [] if after is None else [after], [(d, BF16), (d, BF16)], name=name)


def _loss_and_grad(h, tgt, name):
    d = h.shape[1]

    def fn(h, tgt):
        err = h - tgt
        return err * (1.0 / d), _colsum(err * err)
    return _rowwise(fn, [h, tgt], [], [(d, F32)], [d], name=name)


def _add(a, b, name):
    return _rowwise(lambda a, b: a + b, [a, b], [], [(a.shape[1], F32)], name=name)[0]


def _adamw(w, g, m, v, name):
    def fn(w, g, m, v):
        m = ADAM_B1 * m + (1.0 - ADAM_B1) * g
        v = ADAM_B2 * v + (1.0 - ADAM_B2) * (g * g)
        m_hat = m / (1.0 - ADAM_B1 ** ADAM_STEP)
        v_hat = v / (1.0 - ADAM_B2 ** ADAM_STEP)
        return -ADAM_LR * (m_hat / (jnp.sqrt(v_hat) + ADAM_EPS) + ADAM_WD * w), m, v
    c = w.shape[1]
    return _rowwise(fn, [w, g, m, v], [], [(c, F32)] * 3, name=name)


def _sum_blocks(x, name):
    n = x.shape[0]

    def body(x_ref, o_ref):
        acc = x_ref[0]
        for j in range(1, n):
            acc = acc + x_ref[j]
        o_ref[...] = acc
    return _pcall(body, out_shape=SDS(x.shape[1:], x.dtype), name=name, compiler_params=_params())(x)


def _shifted(chunk, off, rows):
    if off % 8 == 0:
        return chunk[off:off + rows]
    return pltpu.roll(chunk, chunk.shape[0] - off, 0)[:rows]


def _conv_cols(c):
    return 256 if c % 256 == 0 else 128


def _rows_at(i, tb):
    return pl.ds(i * tb, tb) if isinstance(i, int) else pl.ds(pl.multiple_of(i * tb, 8), tb)


def _chunk_before(ref, i, tb, pad):
    if isinstance(i, int) and i == 0:
        return jnp.concatenate([jnp.zeros((pad, ref.shape[1]), F32), ref[0:tb, :]], axis=0)
    if isinstance(i, int):
        return ref[i * tb - pad:(i + 1) * tb, :]
    return ref[pl.ds(pl.multiple_of(i * tb - pad, 8), tb + pad), :]


def _chunk_after(ref, i, tb, pad, last):
    if last:
        return jnp.concatenate([ref[i * tb:(i + 1) * tb, :], jnp.zeros((pad, ref.shape[1]), F32)], axis=0)
    if isinstance(i, int):
        return ref[i * tb:(i + 1) * tb + pad, :]
    return ref[pl.ds(pl.multiple_of(i * tb, 8), tb + pad), :]


def _dwconv_fwd(y, w, b, silu, name):
    t, c = y.shape
    taps = w.shape[0]
    cb, tb, pad = _conv_cols(c), CONV_ROWS, CONV_PAD

    def body(y_ref, w_ref, b_ref, *out_refs):
        def step(i, carry):
            chunk = _chunk_before(y_ref, i, tb, pad)
            acc = jnp.broadcast_to(b_ref[...], (tb, cb))
            for k in range(taps):
                acc = acc + _shifted(chunk, pad - (taps - 1) + k, tb) * w_ref[k:k + 1, :]
            rows = _rows_at(i, tb)
            out_refs[0][rows, :] = acc
            if silu:
                out_refs[1][rows, :] = acc * _sigmoid(acc)
            return carry
        step(0, 0)
        lax.fori_loop(1, t // tb, step, 0)

    n_out = 2 if silu else 1
    return _pcall(body, grid=(c // cb,),
                  in_specs=[pl.BlockSpec((t, cb), lambda j: (0, j)), pl.BlockSpec((taps, cb), lambda j: (0, j)),
                            pl.BlockSpec((1, cb), lambda j: (0, j))],
                  out_specs=[pl.BlockSpec((t, cb), lambda j: (0, j))] * n_out,
                  out_shape=[SDS((t, c), F32)] * n_out, name=name, compiler_params=_params("arbitrary"))(y, w, b)


def _dwconv_bwd(y, dout, w, name):
    t, c = y.shape
    taps = w.shape[0]
    cb, tb, pad = _conv_cols(c), CONV_ROWS, CONV_PAD
    taps_pad = -(-taps // 8) * 8
    n_steps = t // tb
    assert n_steps >= 2

    def body(y_ref, d_ref, w_ref, dy_ref, dw_ref, db_ref):
        dw_ref[...] = jnp.zeros_like(dw_ref)

        def step(i, db, last=False):
            ychunk = _chunk_before(y_ref, i, tb, pad)
            dchunk = _chunk_after(d_ref, i, tb, pad, last)
            d0 = dchunk[:tb]
            acc = jnp.zeros((tb, cb), F32)
            for k in range(taps):
                acc = acc + _shifted(dchunk, taps - 1 - k, tb) * w_ref[k:k + 1, :]
                dw_ref[k:k + 1, :] += _colsum(d0 * _shifted(ychunk, pad - (taps - 1) + k, tb))
            dy_ref[_rows_at(i, tb), :] = acc
            return db + _colsum(d0)
        db = step(0, jnp.zeros((1, cb), F32))
        db = lax.fori_loop(1, n_steps - 1, step, db)
        db_ref[...] = step(n_steps - 1, db, last=True)

    dy, dw, db = _pcall(
        body, grid=(c // cb,),
        in_specs=[pl.BlockSpec((t, cb), lambda j: (0, j)), pl.BlockSpec((t, cb), lambda j: (0, j)),
                  pl.BlockSpec((taps, cb), lambda j: (0, j))],
        out_specs=[pl.BlockSpec((t, cb), lambda j: (0, j)), pl.BlockSpec((taps_pad, cb), lambda j: (0, j)),
                   pl.BlockSpec((1, cb), lambda j: (0, j))],
        out_shape=[SDS((t, c), F32), SDS((taps_pad, c), F32), SDS((1, c), F32)], name=name,
        compiler_params=_params("arbitrary"))(y, dout, w)
    return dy, dw[:taps], db


def _head_norm(x, g):
    rstd = lax.rsqrt(_rowmean(x * x) + EPS)
    xh = x * rstd
    return xh * g, xh, rstd


def _head_norm_bwd(dy, xh, rstd, g):
    dxh = dy * g
    return rstd * (dxh - xh * _rowmean(dxh * xh)), _colsum(dy * xh)


def _attn_probs(q_ref, kp_ref, kc_ref, gq_ref, gk_ref, sinks_ref, n_heads):
    g, blk, hd = Q_PER_KV, ATTN_BLOCK, HEAD_DIM
    kv, n = pl.program_id(0), pl.program_id(1)
    q_raw = q_ref[...].reshape(g * blk, hd)
    k_raw = jnp.concatenate([kp_ref[...], kc_ref[...]], axis=0)
    qn, qh, q_rstd = _head_norm(q_raw, gq_ref[...])
    kn, kh, k_rstd = _head_norm(k_raw, gk_ref[...])
    s = lax.dot_general(qn.astype(BF16), kn.astype(BF16), _DOT_DIMS["nt"], preferred_element_type=F32) * hd ** -0.5
    row = lax.broadcasted_iota(jnp.int32, (g * blk, 1), 0)
    g_row = lax.shift_right_logical(row, 7)
    qi = lax.bitwise_and(row, blk - 1)
    kj = lax.broadcasted_iota(jnp.int32, (1, 2 * blk), 1)
    dist = qi - kj + blk
    valid = (dist >= 0) & (dist < blk) & (kj >= jnp.where(n > 0, 0, blk))
    head = (kv * g + g_row + 1).astype(F32)
    slope = jnp.exp(head * (-8.0 * math.log(2.0) / n_heads))
    s = jnp.where(valid, s - slope * dist.astype(F32), NEG_BIG)
    sink = jnp.zeros((g * blk, 1), F32)
    for gi in range(g):
        sink = jnp.where(g_row == gi, sinks_ref[kv * g + gi], sink)
    m = jnp.maximum(jnp.max(s, axis=1, keepdims=True), sink)
    e = jnp.exp(s - m)
    e_sink = jnp.exp(sink - m)
    inv = 1.0 / (jnp.sum(e, axis=1, keepdims=True) + e_sink)
    return (q_raw, qn, qh, q_rstd), (k_raw, kn, kh, k_rstd), e * inv, e_sink * inv


def _attn_specs(t):
    g, blk, hd = Q_PER_KV, ATTN_BLOCK, HEAD_DIM
    q_spec = pl.BlockSpec((None, g, blk, hd), lambda a, n: (a, 0, n, 0))
    prev_spec = pl.BlockSpec((None, blk, hd), lambda a, n: (a, jnp.maximum(n - 1, 0), 0))
    cur_spec = pl.BlockSpec((None, blk, hd), lambda a, n: (a, n, 0))
    gain_spec = pl.BlockSpec((1, hd), lambda a, n: (0, 0))
    sink_spec = pl.BlockSpec(memory_space=pltpu.SMEM)
    return q_spec, prev_spec, cur_spec, gain_spec, sink_spec


def _attn_fwd(q, k, v, gq, gk, sinks, name):
    n_kv, g, t, hd = q.shape
    blk = ATTN_BLOCK
    n_heads = n_kv * g

    def body(q_ref, kp_ref, kc_ref, vp_ref, vc_ref, gq_ref, gk_ref, sinks_ref, o_ref):
        _, _, p, _ = _attn_probs(q_ref, kp_ref, kc_ref, gq_ref, gk_ref, sinks_ref, n_heads)
        vb = jnp.concatenate([vp_ref[...], vc_ref[...]], axis=0).astype(BF16)
        o = jnp.dot(p.astype(BF16), vb, preferred_element_type=F32)
        o_ref[...] = o.reshape(g, blk, hd).astype(o_ref.dtype)

    q_spec, prev_spec, cur_spec, gain_spec, sink_spec = _attn_specs(t)
    return _pcall(body, grid=(n_kv, t // blk),
                  in_specs=[q_spec, prev_spec, cur_spec, prev_spec, cur_spec, gain_spec, gain_spec, sink_spec],
                  out_specs=q_spec, out_shape=SDS(q.shape, BF16), name=name,
                  compiler_params=_params("arbitrary", "arbitrary"))(q, k, k, v, v, gq, gk, sinks)


def _attn_bwd(q, k, v, do, gq, gk, sinks, name):
    n_kv, g, t, hd = q.shape
    blk = ATTN_BLOCK
    n_heads = n_kv * g

    def body(q_ref, kp_ref, kc_ref, vp_ref, vc_ref, do_ref, gq_ref, gk_ref, sinks_ref,
             dq_ref, dkp_ref, dkc_ref, dvp_ref, dvc_ref, dgq_ref, dgk_ref, dsink_ref):
        kv, n = pl.program_id(0), pl.program_id(1)
        (_, qn, qh, q_rstd), (_, kn, kh, k_rstd), p, p_sink = _attn_probs(
            q_ref, kp_ref, kc_ref, gq_ref, gk_ref, sinks_ref, n_heads)
        vb = jnp.concatenate([vp_ref[...], vc_ref[...]], axis=0).astype(BF16)
        dob = do_ref[...].reshape(g * blk, hd).astype(BF16)
        dp = lax.dot_general(dob, vb, _DOT_DIMS["nt"], preferred_element_type=F32)
        delta = jnp.sum(p * dp, axis=1, keepdims=True)
        ds = (p * (dp - delta) * hd ** -0.5).astype(BF16)
        dv = lax.dot_general(p.astype(BF16), dob, _DOT_DIMS["tn"], preferred_element_type=F32)
        dqn = jnp.dot(ds, kn.astype(BF16), preferred_element_type=F32)
        dkn = lax.dot_general(ds, qn.astype(BF16), _DOT_DIMS["tn"], preferred_element_type=F32)
        dq, dgq = _head_norm_bwd(dqn, qh, q_rstd, gq_ref[...])
        dk, dgk = _head_norm_bwd(dkn, kh, k_rstd, gk_ref[...])
        dq_ref[...] = dq.reshape(g, blk, hd)
        dkp_ref[...] = dk[:blk]
        dkc_ref[...] = dk[blk:]
        dvp_ref[...] = dv[:blk]
        dvc_ref[...] = dv[blk:]

        @pl.when((kv == 0) & (n == 0))
        def _():
            dgq_ref[...] = jnp.zeros_like(dgq_ref)
            dgk_ref[...] = jnp.zeros_like(dgk_ref)

        @pl.when(n == 0)
        def _():
            dsink_ref[...] = jnp.zeros_like(dsink_ref)
        dgq_ref[...] += dgq
        dgk_ref[...] += dgk
        dsink = jnp.sum((-p_sink * delta).reshape(g, blk, 1), axis=1)
        dsink_ref[...] += jnp.broadcast_to(dsink, (g, 128))

    q_spec, prev_spec, cur_spec, gain_spec, sink_spec = _attn_specs(t)
    kv_shape = SDS(k.shape, F32)
    return _pcall(
        body, grid=(n_kv, t // blk),
        in_specs=[q_spec, prev_spec, cur_spec, prev_spec, cur_spec, q_spec, gain_spec, gain_spec, sink_spec],
        out_specs=[q_spec, cur_spec, cur_spec, cur_spec, cur_spec, gain_spec, gain_spec,
                   pl.BlockSpec((None, g, 128), lambda a, n: (a, 0, 0))],
        out_shape=[SDS(q.shape, F32), kv_shape, kv_shape, kv_shape, kv_shape, SDS((1, hd), F32), SDS((1, hd), F32),
                   SDS((n_kv, g, 128), F32)],
        name=name, compiler_params=_params("arbitrary", "arbitrary"))(q, k, k, v, v, do, gq, gk, sinks)


def _col_head(width):
    return lax.shift_right_logical(lax.broadcasted_iota(jnp.int32, (1, width), 1), 6)


def _expand_heads(v, col_head):
    out = jnp.zeros((v.shape[0], col_head.shape[1]), F32)
    for r in range(SSM_R):
        out = jnp.where(col_head == r, v[:, r:r + 1], out)
    return out


def _head_sums(m, col_head):
    lane = lax.broadcasted_iota(jnp.int32, (1, SSM_R), 1)
    out = jnp.zeros((m.shape[0], SSM_R), F32)
    for r in range(SSM_R):
        out = jnp.where(lane == r, jnp.sum(jnp.where(col_head == r, m, 0.0), axis=1, keepdims=True), out)
    return out


def _ssd_specs(d_inner, n_groups, n_chunks, rev):
    l, w, n = SSM_L, SSM_R * SSM_P, SSM_N

    def cc(c):
        return n_chunks - 1 - c if rev else c
    x_spec = pl.BlockSpec((l, w), lambda g, c: (cc(c), g))
    b_spec = pl.BlockSpec((l, n), lambda g, c: (cc(c), d_inner // n + g))
    c_spec = pl.BlockSpec((l, n), lambda g, c: (cc(c), d_inner // n + n_groups + g))
    col_spec = pl.BlockSpec((None, l, SSM_R), lambda g, c: (g, cc(c), 0))
    row_spec = pl.BlockSpec((None, None, SSM_R, l), lambda g, c: (g, cc(c), 0, 0))
    state_spec = pl.BlockSpec((None, None, n, w), lambda g, c: (g, cc(c), 0, 0))
    return x_spec, b_spec, c_spec, col_spec, row_spec, state_spec


def _decay_matrix(a, a_row, r):
    l = SSM_L
    causal = lax.broadcasted_iota(jnp.int32, (l, l), 0) >= lax.broadcasted_iota(jnp.int32, (l, l), 1)
    return jnp.exp(jnp.where(causal, a[:, r:r + 1] - a_row[r:r + 1, :], NEG_BIG))


def _ssd_fwd(xbc, dt_g, cs_g, cs_row, d_inner, name):
    t = xbc.shape[0]
    l, w, n = SSM_L, SSM_R * SSM_P, SSM_N
    n_groups, n_chunks = d_inner // w, t // l

    def body(x_ref, b_ref, c_ref, dt_ref, a_ref, ar_ref, y_ref, prev_ref, h_scr):
        @pl.when(pl.program_id(1) == 0)
        def _():
            h_scr[...] = jnp.zeros_like(h_scr)
        col_head = _col_head(w)
        a, a_row = a_ref[...], ar_ref[...]
        h = h_scr[...]
        prev_ref[...] = h
        xdt = x_ref[...] * _expand_heads(dt_ref[...], col_head)
        xb, bb, cb = xdt.astype(BF16), b_ref[...].astype(BF16), c_ref[...].astype(BF16)
        a_last = a[l - 1:l, :]
        cbt = lax.dot_general(cb, bb, _DOT_DIMS["nt"], preferred_element_type=F32)
        y = jnp.zeros((l, w), F32)
        for r in range(SSM_R):
            m = (cbt * _decay_matrix(a, a_row, r)).astype(BF16)
            y = jnp.where(col_head == r, jnp.dot(m, xb, preferred_element_type=F32), y)
        y_off = jnp.dot(cb, h.astype(BF16), preferred_element_type=F32)
        y_ref[...] = y + _expand_heads(jnp.exp(a), col_head) * y_off
        fx = (xdt * _expand_heads(jnp.exp(a_last - a), col_head)).astype(BF16)
        states = lax.dot_general(bb, fx, _DOT_DIMS["tn"], preferred_element_type=F32)
        h_scr[...] = _expand_heads(jnp.exp(a_last), col_head) * h + states

    x_spec, b_spec, c_spec, col_spec, row_spec, state_spec = _ssd_specs(d_inner, n_groups, n_chunks, False)
    return _pcall(body, grid=(n_groups, n_chunks),
                  in_specs=[x_spec, b_spec, c_spec, col_spec, col_spec, row_spec],
                  out_specs=[x_spec, state_spec],
                  out_shape=[SDS((t, d_inner), F32), SDS((n_groups, n_chunks, n, w), F32)],
                  scratch_shapes=[pltpu.VMEM((n, w), F32)], name=name,
                  compiler_params=_params("arbitrary", "arbitrary"))(xbc, xbc, xbc, dt_g, cs_g, cs_row)


def _ssd_bwd(xbc, dt_g, cs_g, cs_row, prev, dy, d_inner, name):
    t = xbc.shape[0]
    l, w, n = SSM_L, SSM_R * SSM_P, SSM_N
    n_groups, n_chunks = d_inner // w, t // l

    def body(x_ref, b_ref, c_ref, dt_ref, a_ref, ar_ref, prev_ref, dy_ref,
             dx_ref, db_ref, dc_ref, ddtx_ref, dda_ref, dh_scr):
        @pl.when(pl.program_id(1) == 0)
        def _():
            dh_scr[...] = jnp.zeros_like(dh_scr)
        col_head = _col_head(w)
        a, a_row = a_ref[...], ar_ref[...]
        x, dyv = x_ref[...], dy_ref[...]
        h, dhn = prev_ref[...], dh_scr[...]
        dtx = _expand_heads(dt_ref[...], col_head)
        xdt = x * dtx
        a_last = a[l - 1:l, :]
        e_last = jnp.exp(a_last)
        xb, bb, cb = xdt.astype(BF16), b_ref[...].astype(BF16), c_ref[...].astype(BF16)
        hb, dhnb, dyb = h.astype(BF16), dhn.astype(BF16), dyv.astype(BF16)
        edy = (_expand_heads(jnp.exp(a), col_head) * dyv).astype(BF16)
        dc = lax.dot_general(edy, hb, _DOT_DIMS["nt"], preferred_element_type=F32)
        dh_scr[...] = (_expand_heads(e_last, col_head) * dhn
                       + lax.dot_general(cb, edy, _DOT_DIMS["tn"], preferred_element_type=F32))
        fx = xdt * _expand_heads(jnp.exp(a_last - a), col_head)
        g1 = jnp.dot(bb, dhnb, preferred_element_type=F32)
        db = lax.dot_general(fx.astype(BF16), dhnb, _DOT_DIMS["nt"], preferred_element_type=F32)
        q = _head_sums(fx * g1, col_head)
        d_e_last = _head_sums(_colsum(dhn * h), col_head)
        cbt = lax.dot_general(cb, bb, _DOT_DIMS["nt"], preferred_element_type=F32)
        dcbt = jnp.zeros((l, l), F32)
        dxd = jnp.zeros((l, w), F32)
        dda_diag = jnp.zeros((l, SSM_R), F32)
        rows_l = lax.broadcasted_iota(jnp.int32, (l, l), 0)
        cols_l = lax.broadcasted_iota(jnp.int32, (l, l), 1)
        upper = (cols_l >= rows_l).astype(F32)
        head_lane = lax.broadcasted_iota(jnp.int32, (1, SSM_R), 1)
        for r in range(SSM_R):
            decay = _decay_matrix(a, a_row, r)
            dyr = jnp.where(col_head == r, dyv, 0.0).astype(BF16)
            dm = lax.dot_general(dyr, xb, _DOT_DIMS["nt"], preferred_element_type=F32)
            dcbt = dcbt + dm * decay
            m = cbt * decay
            dxr = lax.dot_general(m.astype(BF16), dyb, _DOT_DIMS["tn"], preferred_element_type=F32)
            dxd = jnp.where(col_head == r, dxr, dxd)
            below = jnp.dot(upper.astype(BF16), (dm * m).astype(BF16), preferred_element_type=F32)
            dda_r = jnp.sum(jnp.where(cols_l < rows_l, below, 0.0), axis=1, keepdims=True)
            dda_diag = jnp.where(head_lane == r, dda_r, dda_diag)
        dcbb = dcbt.astype(BF16)
        dc_ref[...] = dc + jnp.dot(dcbb, bb, preferred_element_type=F32)
        db_ref[...] = db + lax.dot_general(dcbb, cb, _DOT_DIMS["tn"], preferred_element_type=F32)
        dxt = dxd + _expand_heads(jnp.exp(a_last - a), col_head) * g1
        dx_ref[...] = dxt * dtx
        ddtx_ref[...] = _head_sums(dxt * x, col_head)
        y_off = _expand_heads(jnp.exp(a), col_head) * jnp.dot(cb, hb, preferred_element_type=F32)
        row = lax.broadcasted_iota(jnp.int32, (l, 1), 0)
        da = _head_sums(dyv * y_off, col_head) + jnp.where(row == l - 1, e_last * d_e_last, 0.0)
        strict_lower = (cols_l < rows_l).astype(F32)
        dda_ref[...] = (dda_diag + jnp.dot(upper, da, precision=lax.Precision.HIGHEST, preferred_element_type=F32)
                        + jnp.dot(strict_lower, q, precision=lax.Precision.HIGHEST, preferred_element_type=F32))

    x_spec, b_spec, c_spec, col_spec, row_spec, state_spec = _ssd_specs(d_inner, n_groups, n_chunks, True)
    bc_out = pl.BlockSpec((l, n), lambda g, c: (n_chunks - 1 - c, g))
    return _pcall(body, grid=(n_groups, n_chunks),
                  in_specs=[x_spec, b_spec, c_spec, col_spec, col_spec, row_spec, state_spec, x_spec],
                  out_specs=[x_spec, bc_out, bc_out, col_spec, col_spec],
                  out_shape=[SDS((t, d_inner), F32), SDS((t, n_groups * n), F32), SDS((t, n_groups * n), F32),
                             SDS(dt_g.shape, F32), SDS(dt_g.shape, F32)],
                  scratch_shapes=[pltpu.VMEM((n, w), F32)], name=name,
                  compiler_params=_params("arbitrary", "arbitrary"))(xbc, xbc, xbc, dt_g, cs_g, cs_row, prev, dy)


def _softplus(x):
    return jnp.maximum(x, 0.0) + jnp.log1p(jnp.exp(-jnp.abs(x)))


def _dt_fwd(dt_raw, bias, a_log, name):
    l = SSM_L

    def fn(raw, bias, a_log):
        dt = _softplus(raw + bias)
        lower = (lax.broadcasted_iota(jnp.int32, (l, l), 0) >= lax.broadcasted_iota(jnp.int32, (l, l), 1)).astype(F32)
        cs = jnp.dot(lower, dt * -jnp.exp(a_log), precision=lax.Precision.HIGHEST, preferred_element_type=F32)
        return dt, cs
    wd = dt_raw.shape[1]
    return _rowwise(fn, [dt_raw], [bias, a_log], [(wd, F32), (wd, F32)], name=name, tm=l)


def _dt_bwd(ddtx, dda, dt_raw, dt, bias, a_log, name):
    def fn(ddtx, dda, raw, dt, bias, a_log):
        a = -jnp.exp(a_log)
        draw = (ddtx + dda * a) * _sigmoid(raw + bias)
        return draw, _colsum(draw), _colsum(dda * dt) * a
    wd = dt_raw.shape[1]
    return _rowwise(fn, [ddtx, dda, dt_raw, dt], [bias, a_log], [(wd, BF16)], [wd, wd], name=name)


MESH_IDS = pl.DeviceIdType.MESH
ANY_SPEC = pl.BlockSpec(memory_space=pl.ANY)


def _mesh_pos():
    return lax.axis_index("x"), lax.axis_index("y"), lax.axis_index("c")


def _all_gather(arrays, name):
    n = len(arrays)

    def body(*refs):
        x_refs, out_refs = refs[:n], refs[n:2 * n]
        send_sems, recv_sems, local_sems = refs[2 * n:]
        mx, my, mc = _mesh_pos()
        me, sibling = (mx, my, mc), (mx, my, 1 - mc)
        chips = [(1 - mx, my), (mx, 1 - my), (1 - mx, 1 - my)]

        def copy(a, k, block, to, src=None):
            px, py, pc = block
            slot = out_refs[a].at[4 * px + 2 * py + pc]
            return pltpu.make_async_remote_copy(
                src_ref=slot if src is None else src, dst_ref=slot, send_sem=send_sems.at[7 * a + k],
                recv_sem=recv_sems.at[7 * a + k], device_id=to, device_id_type=MESH_IDS)

        mine = [pltpu.make_async_copy(x_refs[a], out_refs[a].at[4 * mx + 2 * my + mc], local_sems.at[a])
                for a in range(n)]
        sends = []
        for a in range(n):
            mine[a].start()
            first = [copy(a, 0, me, sibling, src=x_refs[a])]
            first += [copy(a, 1 + j, me, (*chip, mc), src=x_refs[a]) for j, chip in enumerate(chips)]
            for cp in first:
                cp.start()
            sends += first
        for j, chip in enumerate(chips):
            for a in range(n):
                copy(a, 1 + j, (*chip, mc), me).wait_recv()
                passed = copy(a, 4 + j, (*chip, mc), sibling)
                passed.start()
                sends.append(passed)
        for a in range(n):
            copy(a, 0, sibling, me).wait_recv()
            for j, chip in enumerate(chips):
                copy(a, 4 + j, (*chip, 1 - mc), me).wait_recv()
        for cp in sends:
            cp.wait_send()
        for cp in mine:
            cp.wait()

    return _pcall(body, out_shape=[SDS((N_DEV,) + x.shape, x.dtype) for x in arrays], in_specs=[ANY_SPEC] * n,
                  out_specs=[ANY_SPEC] * n,
                  scratch_shapes=[pltpu.SemaphoreType.DMA((7 * n,)), pltpu.SemaphoreType.DMA((7 * n,)),
                                  pltpu.SemaphoreType.DMA((n,))], name=name)(*arrays)


def _swap_with_sibling(arrays, name):
    n = len(arrays)

    def body(*refs):
        g_refs, out_refs = refs[:n], refs[n:2 * n]
        send_sems, recv_sems = refs[2 * n:]
        mx, my, mc = _mesh_pos()
        copies = [pltpu.make_async_remote_copy(
            src_ref=g_refs[a].at[2 * k + (1 - mc)], dst_ref=out_refs[a].at[k], send_sem=send_sems.at[4 * a + k],
            recv_sem=recv_sems.at[4 * a + k], device_id=(mx, my, 1 - mc), device_id_type=MESH_IDS)
            for a in range(n) for k in range(4)]
        for cp in copies:
            cp.start()
        for cp in copies:
            cp.wait()

    return _pcall(body, out_shape=[SDS((4,) + g.shape[1:], g.dtype) for g in arrays], in_specs=[ANY_SPEC] * n,
                  out_specs=[ANY_SPEC] * n,
                  scratch_shapes=[pltpu.SemaphoreType.DMA((4 * n,)), pltpu.SemaphoreType.DMA((4 * n,))],
                  name=name)(*arrays)


HBM_SPEC = pl.BlockSpec(memory_space=pltpu.HBM)
SEM_SPEC = pl.BlockSpec(memory_space=pltpu.SEMAPHORE)
SIDE_EFFECT = pltpu.SideEffectType.DATAFLOW_SIDE_EFFECTING


def _chip_copies(src_refs, dst_refs, send_sems, recv_sems, gather):
    mx, my, mc = _mesh_pos()
    peers = [(1 - mx, my, mc), (mx, 1 - my, mc), (1 - mx, 1 - my, mc)]
    if gather:
        peers = [(mx, my, 1 - mc)] + peers
    copies = []
    for a in range(len(src_refs)):
        for j, peer in enumerate(peers):
            src = src_refs[a] if gather else src_refs[a].at[2 * peer[0] + peer[1]]
            dst = dst_refs[a].at[4 * mx + 2 * my + mc] if gather else dst_refs[a].at[j]
            k = len(peers) * a + j
            copies.append(pltpu.make_async_remote_copy(src_ref=src, dst_ref=dst, send_sem=send_sems.at[k],
                                                       recv_sem=recv_sems.at[k], device_id=peer, device_id_type=MESH_IDS))
    return copies


def _chips_start(srcs, dsts, gather, name):
    n = len(srcs)
    n_sems = (4 if gather else 3) * n
    if dsts is None:
        dsts = [lax.empty((3,) + a.shape[1:], a.dtype) for a in srcs]

    def body(*refs):
        for cp in _chip_copies(refs[:n], refs[n:2 * n], refs[2 * n], refs[2 * n + 1], gather):
            cp.start()
        refs[-1][...] = jnp.zeros_like(refs[-1])

    res = _pcall(
        body, name=name,
        out_shape=(pltpu.SemaphoreType.DMA((n_sems,)), pltpu.SemaphoreType.DMA((n_sems,)),
                   *[pltpu.HBM(a.shape, a.dtype) for a in srcs], *[pltpu.HBM(d.shape, d.dtype) for d in dsts],
                   SDS((8, 128), F32)),
        in_specs=[HBM_SPEC] * (2 * n),
        out_specs=(SEM_SPEC, SEM_SPEC, *[HBM_SPEC] * (2 * n), pl.BlockSpec(memory_space=pltpu.VMEM)),
        input_output_aliases={i: 2 + i for i in range(2 * n)},
        compiler_params=pltpu.CompilerParams(has_side_effects=SIDE_EFFECT),
    )(*[pltpu.with_memory_space_constraint(a, pltpu.HBM) for a in srcs],
      *[pltpu.with_memory_space_constraint(d, pltpu.HBM) for d in dsts])
    return (res[0], res[1], list(res[2:2 + n]), list(res[2 + n:2 + 2 * n])), res[-1]


def _chips_wait(handle, after, gather, name):
    send_sems, recv_sems, srcs, dsts = handle
    n = len(srcs)

    def body(*refs):
        for cp in _chip_copies(refs[:n], refs[n:2 * n], refs[2 * n], refs[2 * n + 1], gather):
            cp.wait_send()
            cp.wait_recv()

    res = _pcall(
        body, name=name, out_shape=tuple(pltpu.HBM(a.shape, a.dtype) for a in srcs + dsts),
        in_specs=[HBM_SPEC] * (2 * n) + [SEM_SPEC, SEM_SPEC, ANY_SPEC], out_specs=tuple([HBM_SPEC] * (2 * n)),
        input_output_aliases={i: i for i in range(2 * n)},
        compiler_params=pltpu.CompilerParams(has_side_effects=SIDE_EFFECT),
    )(*srcs, *dsts, send_sems, recv_sems, after)
    return list(res[:n]), list(res[n:])


def _place_own(xs, me, name):
    n = len(xs)
    n_row_blocks = next(nb for nb in (4, 2, 1) if all(x.shape[0] % (nb * PACK_ROW_ALIGN) == 0 for x in xs))

    def body(me_ref, *refs):
        for a in range(n):
            refs[n + a][...] = refs[a][...]

    def rows(x):
        return x.shape[0] // n_row_blocks
    grid_spec = pltpu.PrefetchScalarGridSpec(
        num_scalar_prefetch=1, grid=(n_row_blocks,),
        in_specs=[pl.BlockSpec((rows(x), x.shape[1]), lambda i, me_ref: (i, 0)) for x in xs],
        out_specs=[pl.BlockSpec((None, rows(x), x.shape[1]), lambda i, me_ref: (me_ref[0], i, 0)) for x in xs])
    return _pcall(body, grid_spec=grid_spec, out_shape=[SDS((N_DEV,) + x.shape, x.dtype) for x in xs], name=name,
                  compiler_params=_params("arbitrary"))(me, *xs)


def _forward_to_sibling(gs, name):
    n = len(gs)

    def body(*refs):
        out_refs = refs[n:2 * n]
        send_sems, recv_sems = refs[2 * n:]
        mx, my, mc = _mesh_pos()
        chips = [(1 - mx, my), (mx, 1 - my), (1 - mx, 1 - my)]
        sends = []
        for a in range(n):
            for j, (px, py) in enumerate(chips):
                slot = out_refs[a].at[4 * px + 2 * py + mc]
                sends.append(pltpu.make_async_remote_copy(
                    src_ref=slot, dst_ref=slot, send_sem=send_sems.at[3 * a + j], recv_sem=recv_sems.at[3 * a + j],
                    device_id=(mx, my, 1 - mc), device_id_type=MESH_IDS))
        for cp in sends:
            cp.start()
        for a in range(n):
            for j, (px, py) in enumerate(chips):
                slot = out_refs[a].at[4 * px + 2 * py + (1 - mc)]
                pltpu.make_async_remote_copy(
                    src_ref=slot, dst_ref=slot, send_sem=send_sems.at[3 * a + j], recv_sem=recv_sems.at[3 * a + j],
                    device_id=(mx, my, 1 - mc), device_id_type=MESH_IDS).wait_recv()
        for cp in sends:
            cp.wait_send()

    return _pcall(body, out_shape=[SDS(g.shape, g.dtype) for g in gs], in_specs=[ANY_SPEC] * n, out_specs=[ANY_SPEC] * n,
                  input_output_aliases={a: a for a in range(n)},
                  scratch_shapes=[pltpu.SemaphoreType.DMA((3 * n,)), pltpu.SemaphoreType.DMA((3 * n,))],
                  name=name)(*gs)


def _add_core_blocks(gs, recvs, core, name):
    n = len(gs)

    def body(core_ref, *refs):
        for a in range(n):
            refs[2 * n + a][...] = (refs[a][...].astype(F32) + refs[n + a][...].astype(F32)).astype(refs[2 * n + a].dtype)

    in_specs = [pl.BlockSpec((None, None) + g.shape[1:], lambda k, core_ref: (k, core_ref[0], 0, 0)) for g in gs]
    in_specs += [pl.BlockSpec((None,) + r.shape[1:], lambda k, core_ref: (k, 0, 0)) for r in recvs]
    grid_spec = pltpu.PrefetchScalarGridSpec(
        num_scalar_prefetch=1, grid=(4,), in_specs=in_specs,
        out_specs=[pl.BlockSpec((None,) + r.shape[1:], lambda k, core_ref: (k, 0, 0)) for r in recvs])
    return _pcall(body, grid_spec=grid_spec, out_shape=[SDS(r.shape, r.dtype) for r in recvs], name=name,
                  compiler_params=_params("arbitrary"))(core, *[g.reshape((4, 2) + g.shape[1:]) for g in gs], *recvs)


def _add_chip_blocks(parts, recvs, chip, name):
    n = len(parts)
    n_row_blocks = next(nb for nb in (4, 2, 1) if all(p.shape[1] % (nb * PACK_ROW_ALIGN) == 0 for p in parts))

    def body(chip_ref, *refs):
        for a in range(n):
            r_ref = refs[n + a]
            refs[2 * n + a][...] = ((refs[a][...].astype(F32) + r_ref[0].astype(F32)) + r_ref[1].astype(F32)
                                    ) + r_ref[2].astype(F32)

    def rows(p):
        return p.shape[1] // n_row_blocks
    in_specs = [pl.BlockSpec((None, rows(p), p.shape[2]), lambda i, chip_ref: (chip_ref[0], i, 0)) for p in parts]
    in_specs += [pl.BlockSpec((3, rows(p), p.shape[2]), lambda i, chip_ref: (0, i, 0)) for p in parts]
    grid_spec = pltpu.PrefetchScalarGridSpec(
        num_scalar_prefetch=1, grid=(n_row_blocks,), in_specs=in_specs,
        out_specs=[pl.BlockSpec((rows(p), p.shape[2]), lambda i, chip_ref: (i, 0)) for p in parts])
    return _pcall(body, grid_spec=grid_spec, out_shape=[SDS(p.shape[1:], F32) for p in parts], name=name,
                  compiler_params=_params("arbitrary"))(chip, *parts, *recvs)


class _Exchange:
    def __init__(self, shards):
        self.shards = shards
        self.core = lax.axis_index("c").astype(jnp.int32).reshape(1)
        self.chip = (2 * lax.axis_index("x") + lax.axis_index("y")).astype(jnp.int32).reshape(1)
        self.me = 2 * self.chip + self.core

    def gather_start(self, i):
        xs = [self.shards[n][_layer_of(n, i)] for n in _layer_weights(i)]
        return _chips_start(xs, _place_own(xs, self.me, f"l{i}_gather_own"), True, f"l{i}_gather_start")

    def gather_finish(self, i, handle, after):
        _, gs = _chips_wait(handle, after, True, f"l{i}_gather_wait")
        out = {}
        for n, g in zip(_layer_weights(i), _forward_to_sibling(gs, f"l{i}_gather_forward")):
            axis = SHARD_AXIS[n] - 1
            out[n] = _Sharded(g[:, None], 0, axis) if n in DIRECT else _join_shards(g, axis)
        return out

    def reduce_start(self, tag, names, gs):
        parts = _add_core_blocks(gs, _swap_with_sibling(gs, tag + "_rs_sibling"), self.core, tag + "_rs_add_core")
        return _chips_start(parts, None, False, tag + "_rs_start")

    def reduce_finish(self, tag, handle, after):
        parts, lands = _chips_wait(handle, after, False, tag + "_rs_wait")
        return _add_chip_blocks(parts, lands, self.chip, tag + "_rs_add_chips")


def _layer_weights(i):
    return MIXER_WEIGHTS[i % N_MIXERS] + LAYER_WEIGHTS


def _layer_of(name, i):
    return i if name in LAYER_WEIGHTS else i // N_MIXERS


def _slab_rows(n):
    return -(-n // (PACK_COLS * PACK_ROW_ALIGN)) * PACK_ROW_ALIGN


def _pack(arrays):
    def slab(a):
        flat = a.reshape(-1)
        rows = _slab_rows(flat.shape[0])
        return jnp.pad(flat, (0, rows * PACK_COLS - flat.shape[0])).reshape(rows, PACK_COLS)
    return jnp.concatenate([slab(a) for a in arrays], axis=0)


def _unpack(slab, shapes, lead=()):
    out, row = [], 0
    for shape in shapes:
        n = math.prod(shape)
        rows = _slab_rows(n)
        part = lax.slice_in_dim(slab, row, row + rows, axis=len(lead)).reshape(lead + (rows * PACK_COLS,))
        out.append(lax.slice_in_dim(part, 0, n, axis=len(lead)).reshape(lead + tuple(shape)))
        row += rows
    return out


def _join_shards(t, axis):
    t = jnp.moveaxis(t, 0, axis)
    return t.reshape(t.shape[:axis] + (t.shape[axis] * t.shape[axis + 1],) + t.shape[axis + 2:])


def _split_shards(full, axis):
    s = full.shape
    t = full.reshape(s[:axis] + (N_DEV, s[axis] // N_DEV) + s[axis + 1:])
    return jnp.moveaxis(t, axis, 0)


def _weight_grad(lhs, rhs, name, tag):
    axis = SHARD_AXIS[name] - 1
    if name in DIRECT:
        return _mm(lhs, rhs, "tn", [BF16], name=tag, shard_out=axis)
    return _split_shards(_mm(lhs, rhs, "tn", [F32], name=tag), axis).astype(BF16)


def _attention_fwd(u, h, w, wl, j, tag):
    t, d = u.shape
    n_heads = d // HEAD_DIM
    n_kv = n_heads // Q_PER_KV
    qkv = _mm(u, wl["a_wqkv"], "nn", [F32], name=tag + "_qkv")
    q = qkv[:, :d].reshape(t, n_kv, Q_PER_KV, HEAD_DIM).transpose(1, 2, 0, 3)
    k = qkv[:, d:d + n_kv * HEAD_DIM].reshape(t, n_kv, HEAD_DIM).transpose(1, 0, 2)
    v = qkv[:, d + n_kv * HEAD_DIM:].reshape(t, n_kv, HEAD_DIM).transpose(1, 0, 2)
    gq, gk, sinks = w["a_q_norm_g"][j][None], w["a_k_norm_g"][j][None], w["a_sinks"][j]
    o = _attn_fwd(q, k, v, gq, gk, sinks, tag + "_attn")
    o2 = o.transpose(2, 0, 1, 3).reshape(t, d)
    h1 = _mm(o2, wl["a_wo"], "nn", [F32], name=tag + "_wo", epi=lambda acc, h: acc + h, rows=[h])
    return h1, (q, k, v, o2)


def _attention_bwd(ctx, u, dh1, dh1b, w, wl, j, tag, grads, big):
    q, k, v, o2 = ctx
    t, d = u.shape
    n_kv = q.shape[0]
    gq, gk, sinks = w["a_q_norm_g"][j][None], w["a_k_norm_g"][j][None], w["a_sinks"][j]
    big.append(("a_wo", j, _weight_grad(o2, dh1b, "a_wo", tag + "_dwo")))
    do2 = _mm(dh1b, wl["a_wo"], "nt", [F32], name=tag + "_do")
    do = do2.reshape(t, n_kv, Q_PER_KV, HEAD_DIM).transpose(1, 2, 0, 3)
    dq, dkp, dkc, dvp, dvc, dgq, dgk, dsink = _attn_bwd(q, k, v, do, gq, gk, sinks, tag + "_attn_bwd")

    def fold(prev_part, cur_part):
        shifted = jnp.concatenate([prev_part[:, ATTN_BLOCK:], jnp.zeros_like(prev_part[:, :ATTN_BLOCK])], axis=1)
        both = _add(shifted.reshape(n_kv * t, HEAD_DIM), cur_part.reshape(n_kv * t, HEAD_DIM), tag + "_fold")
        return both.reshape(n_kv, t, HEAD_DIM).transpose(1, 0, 2).reshape(t, n_kv * HEAD_DIM)
    dqkv = jnp.concatenate([dq.transpose(2, 0, 1, 3).reshape(t, d), fold(dkp, dkc), fold(dvp, dvc)], axis=1).astype(BF16)
    grads["a_q_norm_g"][j] = dgq[0]
    grads["a_k_norm_g"][j] = dgk[0]
    grads["a_sinks"][j] = dsink[:, :, 0].reshape(-1)
    big.append(("a_wqkv", j, _weight_grad(u, dqkv, "a_wqkv", tag + "_dwqkv")))
    return _mm(dqkv, wl["a_wqkv"], "nt", [F32], name=tag + "_du")


def _conformer_fwd(u, h, w, wl, j, tag):
    t, d = u.shape
    a = _mm(u, wl["b_w_pw1"], "nn", [F32], name=tag + "_pw1", epi=lambda acc, b: acc + b, pars=[w["b_b_pw1"][j][None]])
    y = _rowwise(lambda val, gate: val * _sigmoid(gate), [(a, d, 0), (a, d, 1)], [], [(d, F32)], name=tag + "_glu")[0]
    y2 = _dwconv_fwd(y, w["b_w_dw"][j], w["b_b_dw"][j][None], False, tag + "_dw")[0]

    def ln_silu(y2, g, b):
        mu = _rowmean(y2)
        yc = y2 - mu
        y3 = yc * lax.rsqrt(_rowmean(yc * yc) + EPS) * g + b
        return y3 * _sigmoid(y3)
    y4 = _rowwise(ln_silu, [y2], [w["b_ln_g"][j][None], w["b_ln_b"][j][None]], [(d, BF16)], name=tag + "_ln")[0]
    h1 = _mm(y4, wl["b_w_pw2"], "nn", [F32], name=tag + "_pw2", epi=lambda acc, h, b: acc + h + b, rows=[h],
             pars=[w["b_b_pw2"][j][None]])
    return h1, (a, y, y2, y4)


def _conformer_bwd(ctx, u, dh1, dh1b, dh1_colsum, w, wl, j, tag, grads, big):
    a, y, y2, y4 = ctx
    t, d = u.shape
    big.append(("b_w_pw2", j, _weight_grad(y4, dh1b, "b_w_pw2", tag + "_dwpw2")))
    grads["b_b_pw2"][j] = dh1_colsum[0]
    dy4 = _mm(dh1b, wl["b_w_pw2"], "nt", [F32], name=tag + "_dy4")

    def ln_silu_bwd(dy4, y2, g, b):
        mu = _rowmean(y2)
        yc = y2 - mu
        rstd = lax.rsqrt(_rowmean(yc * yc) + EPS)
        xh = yc * rstd
        dy3 = dy4 * _dsilu(xh * g + b)
        dxh = dy3 * g
        return rstd * (dxh - _rowmean(dxh) - xh * _rowmean(dxh * xh)), _colsum(dy3 * xh), _colsum(dy3)
    dy2, dlg, dlb = _rowwise(ln_silu_bwd, [dy4, y2], [w["b_ln_g"][j][None], w["b_ln_b"][j][None]], [(d, F32)], [d, d],
                             name=tag + "_ln_bwd")
    grads["b_ln_g"][j], grads["b_ln_b"][j] = dlg[0], dlb[0]
    dy, dw_dw, db_dw = _dwconv_bwd(y, dy2, w["b_w_dw"][j], tag + "_dw_bwd")
    grads["b_w_dw"][j], grads["b_b_dw"][j] = dw_dw, db_dw[0]

    def glu_bwd(dy, val, gate):
        s = _sigmoid(gate)
        dval, dgate = dy * s, dy * val * s * (1.0 - s)
        return dval, dgate, _colsum(dval), _colsum(dgate)
    dval, dgate, dbv, dbg = _rowwise(glu_bwd, [dy, (a, d, 0), (a, d, 1)], [], [(d, BF16), (d, BF16)], [d, d],
                                     name=tag + "_glu_bwd")
    da = jnp.concatenate([dval, dgate], axis=1)
    grads["b_b_pw1"][j] = jnp.concatenate([dbv[0], dbg[0]])
    big.append(("b_w_pw1", j, _weight_grad(u, da, "b_w_pw1", tag + "_dwpw1")))
    return _mm(da, wl["b_w_pw1"], "nt", [F32], name=tag + "_du")


def _ssd_layouts(v, n_groups):
    t = v.shape[0]
    vg = v[:, :n_groups * SSM_R].reshape(t, n_groups, SSM_R).transpose(1, 0, 2)
    return vg, vg.reshape(n_groups, t // SSM_L, SSM_L, SSM_R).transpose(0, 1, 3, 2)


def _from_group_layout(vg, width):
    n_groups, t, _ = vg.shape
    v = vg.transpose(1, 0, 2).reshape(t, n_groups * SSM_R)
    return jnp.pad(v, ((0, 0), (0, width - n_groups * SSM_R)))


def _pad_lanes(v, width):
    return jnp.pad(v, [(0, 0)] * (v.ndim - 1) + [(0, width - v.shape[-1])])


def _mamba_weights(wl, d_inner, conv_c):
    w_in = wl["c_w_in"]
    n_heads = w_in.shape[1] - d_inner - conv_c
    return (w_in[:, :d_inner], w_in[:, d_inner:d_inner + conv_c], _pad_lanes(w_in[:, d_inner + conv_c:], 128), n_heads)


def _mamba_fwd(u, h, w, wl, j, tag):
    t, d = u.shape
    d_inner = w["c_D"].shape[1] * SSM_P
    conv_c = w["c_w_conv"][j].shape[1]
    w_z, w_xbc, w_dt, n_heads = _mamba_weights(wl, d_inner, conv_c)
    n_groups = n_heads // SSM_R
    z = _mm(u, w_z, "nn", [F32], name=tag + "_in_z")
    xbc_raw = _mm(u, w_xbc, "nn", [F32], name=tag + "_in_xbc")
    dt_raw = _mm(u, w_dt, "nn", [F32], name=tag + "_in_dt")
    pre, xbc = _dwconv_fwd(xbc_raw, w["c_w_conv"][j], w["c_b_conv"][j][None], True, tag + "_conv")
    bias, a_log = _pad_lanes(w["c_dt_bias"][j][None], 128), _pad_lanes(w["c_A_log"][j][None], 128)
    dt, cs = _dt_fwd(dt_raw, bias, a_log, tag + "_dt")
    dt_g, _ = _ssd_layouts(dt, n_groups)
    cs_g, cs_row = _ssd_layouts(cs, n_groups)
    y, prev = _ssd_fwd(xbc, dt_g, cs_g, cs_row, d_inner, tag + "_ssd")
    d_cols = jnp.repeat(w["c_D"][j], SSM_P)[None]
    g = w["c_norm_g"][j][None]

    def gated_norm(y, xs, z, d_cols, g):
        yg = (y + xs * d_cols) * (z * _sigmoid(z))
        return yg * lax.rsqrt(_rowmean(yg * yg) + EPS) * g
    yn = _rowwise(gated_norm, [y, (xbc, d_inner, 0), z], [d_cols, g], [(d_inner, BF16)], name=tag + "_gnorm")[0]
    h1 = _mm(yn, wl["c_w_out"], "nn", [F32], name=tag + "_out", epi=lambda acc, h: acc + h, rows=[h])
    return h1, (z, xbc_raw, dt_raw, pre, xbc, dt, dt_g, cs_g, cs_row, y, prev, yn)


def _mamba_bwd(ctx, u, dh1, dh1b, w, wl, j, tag, grads, big):
    z, xbc_raw, dt_raw, pre, xbc, dt, dt_g, cs_g, cs_row, y, prev, yn = ctx
    t, d = u.shape
    d_inner = w["c_D"].shape[1] * SSM_P
    conv_c = w["c_w_conv"][j].shape[1]
    w_z, w_xbc, w_dt, n_heads = _mamba_weights(wl, d_inner, conv_c)
    n_groups = n_heads // SSM_R
    big.append(("c_w_out", j, _weight_grad(yn, dh1b, "c_w_out", tag + "_dwout")))
    dyn = _mm(dh1b, wl["c_w_out"], "nt", [F32], name=tag + "_dyn")
    d_cols = jnp.repeat(w["c_D"][j], SSM_P)[None]
    g = w["c_norm_g"][j][None]

    def gated_norm_bwd(dyn, y, xs, z, d_cols, g):
        sg = _sigmoid(z)
        yt = y + xs * d_cols
        yg = yt * (z * sg)
        rstd = lax.rsqrt(_rowmean(yg * yg) + EPS)
        xh = yg * rstd
        dxh = dyn * g
        dyg = rstd * (dxh - xh * _rowmean(dxh * xh))
        dyt = dyg * (z * sg)
        return dyt, dyg * yt * (sg * (1.0 + z * (1.0 - sg))), dyt * d_cols, _colsum(dyn * xh), _colsum(dyt * xs)
    dyt, dz, dxs_skip, dg, dd_cols = _rowwise(
        gated_norm_bwd, [dyn, y, (xbc, d_inner, 0), z], [d_cols, g], [(d_inner, F32), (d_inner, BF16), (d_inner, F32)],
        [d_inner, d_inner], name=tag + "_gnorm_bwd")
    grads["c_norm_g"][j] = dg[0]
    grads["c_D"][j] = dd_cols.reshape(n_heads, SSM_P).sum(axis=1)
    dx, db, dc, ddtx_g, dda_g = _ssd_bwd(xbc, dt_g, cs_g, cs_row, prev, dyt, d_inner, tag + "_ssd_bwd")
    dxs = _add(dx, dxs_skip, tag + "_dxs")
    dpost = jnp.concatenate([dxs, db, dc], axis=1)
    dpre = _rowwise(lambda dpost, pre: dpost * _dsilu(pre), [dpost, pre], [], [(conv_c, F32)], name=tag + "_silu_bwd")[0]
    dxbc, dw_conv, db_conv = _dwconv_bwd(xbc_raw, dpre, w["c_w_conv"][j], tag + "_conv_bwd")
    grads["c_w_conv"][j], grads["c_b_conv"][j] = dw_conv, db_conv[0]
    bias, a_log = _pad_lanes(w["c_dt_bias"][j][None], 128), _pad_lanes(w["c_A_log"][j][None], 128)
    ddt_raw, dbias, da_log = _dt_bwd(_from_group_layout(ddtx_g, 128), _from_group_layout(dda_g, 128), dt_raw, dt, bias,
                                     a_log, tag + "_dt_bwd")
    grads["c_dt_bias"][j], grads["c_A_log"][j] = dbias[0, :n_heads], da_log[0, :n_heads]
    dxbc_b = dxbc.astype(BF16)
    dw_in = jnp.concatenate(
        [_mm(u, dz, "tn", [F32], name=tag + "_dwin_z"), _mm(u, dxbc_b, "tn", [F32], name=tag + "_dwin_xbc"),
         _mm(u, ddt_raw, "tn", [F32], name=tag + "_dwin_dt")[:, :n_heads]], axis=1)
    big.append(("c_w_in", j, _split_shards(dw_in, SHARD_AXIS["c_w_in"] - 1).astype(BF16)))
    du = _mm(dz, w_z, "nt", [F32], name=tag + "_du_z")
    du = _mm(dxbc_b, w_xbc, "nt", [F32], name=tag + "_du_xbc", epi=lambda acc, r: acc + r, rows=[du])
    return _mm(ddt_raw, w_dt, "nt", [F32], name=tag + "_du_dt", epi=lambda acc, r: acc + r, rows=[du])


def _local_step(x, p, tgt, w, comm, first):
    depth = p.shape[0]
    grads = {name: {} for name in REPLICATED + tuple(n for n, _ in SMALL_SHARDED)}
    reduced = {name: {} for name, _ in BIG_SHARDED}
    saved = []
    h = x
    wl = comm.gather_finish(0, first, w[SMALL_SHARDED[0][0]])
    for i in range(depth):
        kind, j, tag = i % N_MIXERS, i // N_MIXERS, f"l{i}"
        token = None
        if i + 1 < depth:
            handle, token = comm.gather_start(i + 1)
        u = _rms_fwd(h, w["mix_norm_g"][i][None], tag + "_rms_mix", after=token)
        fwd = (_attention_fwd, _conformer_fwd, _mamba_fwd)[kind]
        h1, ctx = fwd(u, h, w, wl, j, tag)
        u2 = _rms_fwd(h1, w["mlp_norm_g"][i][None], tag + "_rms_mlp")
        a, r = _mm(u2, wl["m_w1"], "nn", [BF16, BF16], name=tag + "_w1",
                   epi=lambda acc: (acc, jnp.square(jnp.maximum(acc, 0.0))))
        h2 = _mm(r, wl["m_w2"], "nn", [F32], name=tag + "_w2", epi=lambda acc, h: acc + h, rows=[h1])
        u3 = _rms_fwd(h2, w["ple_norm_g"][i][None], tag + "_rms_ple")
        gl = _mm(u3, wl["ple_w_gate"], "nn", [F32], name=tag + "_gate")
        h3, pp = _mm(p[i], wl["ple_w_proj"], "nn", [F32, BF16], name=tag + "_proj",
                     epi=lambda acc, h, gl: (h + _sigmoid(gl) * acc, acc), rows=[h2, gl])
        saved.append((h, u, ctx, h1, u2, a, r, h2, u3, gl, pp, wl))
        h = h3
        if i + 1 < depth:
            wl = comm.gather_finish(i + 1, handle, h)

    dh, sq_err = _loss_and_grad(h, tgt, "loss")
    pending, token = None, None
    for i in reversed(range(depth)):
        kind, j, tag = i % N_MIXERS, i // N_MIXERS, f"l{i}"
        h0, u, ctx, h1, u2, a, r, h2, u3, gl, pp, wl = saved[i]
        big = []
        dgl, dpp = _ple_bwd(dh, gl, pp, tag + "_ple_bwd", after=token)
        big.append(("ple_w_proj", i, _weight_grad(p[i], dpp, "ple_w_proj", tag + "_dwproj")))
        big.append(("ple_w_gate", i, _weight_grad(u3, dgl, "ple_w_gate", tag + "_dwgate")))
        du3 = _mm(dgl, wl["ple_w_gate"], "nt", [F32], name=tag + "_du3")
        dh2, dh2b, dg, _ = _rms_bwd(du3, h2, w["ple_norm_g"][i][None], dh, tag + "_rms_ple_bwd")
        grads["ple_norm_g"][i] = dg[0]
        big.append(("m_w2", i, _weight_grad(r, dh2b, "m_w2", tag + "_dw2")))
        da = _mm(dh2b, wl["m_w2"], "nt", [BF16], name=tag + "_da",
                 epi=lambda acc, a: acc * (2.0 * jnp.maximum(a.astype(F32), 0.0)), rows=[a])
        big.append(("m_w1", i, _weight_grad(u2, da, "m_w1", tag + "_dw1")))
        du2 = _mm(da, wl["m_w1"], "nt", [F32], name=tag + "_du2")
        dh1, dh1b, dg, dh1_colsum = _rms_bwd(du2, h1, w["mlp_norm_g"][i][None], dh2, tag + "_rms_mlp_bwd")
        grads["mlp_norm_g"][i] = dg[0]
        if kind == 0:
            du = _attention_bwd(ctx, u, dh1, dh1b, w, wl, j, tag, grads, big)
        elif kind == 1:
            du = _conformer_bwd(ctx, u, dh1, dh1b, dh1_colsum, w, wl, j, tag, grads, big)
        else:
            du = _mamba_bwd(ctx, u, dh1, dh1b, w, wl, j, tag, grads, big)
        dh, _, dg, _ = _rms_bwd(du, h0, w["mix_norm_g"][i][None], dh1, tag + "_rms_mix_bwd")
        grads["mix_norm_g"][i] = dg[0]
        if pending is not None:
            _finish_reduce(comm, pending, dh, reduced)
        handle, token = comm.reduce_start(tag, [name for name, _, _ in big], [g for _, _, g in big])
        pending = (tag, handle, [(name, layer) for name, layer, _ in big])

    def stack(d):
        return {name: jnp.stack([g[k] for k in sorted(g)]) for name, g in d.items()}
    return sq_err, dh, reduced, stack(grads), pending


def _finish_reduce(comm, pending, after, reduced):
    tag, handle, keys = pending
    for (name, layer), g in zip(keys, comm.reduce_finish(tag, handle, after)):
        reduced[name][layer] = g


def kernel(x, p, mix_norm_g, mlp_norm_g, ple_norm_g, a_wqkv, a_q_norm_g, a_k_norm_g, a_sinks, a_wo, b_w_pw1, b_b_pw1, b_w_dw, b_b_dw, b_ln_g, b_ln_b, b_w_pw2, b_b_pw2, c_w_in, c_w_conv, c_b_conv, c_dt_bias, c_A_log, c_D, c_norm_g, c_w_out, m_w1, m_w2, ple_w_proj, ple_w_gate, loss_target, m_mix_norm_g, m_mlp_norm_g, m_ple_norm_g, m_a_wqkv, m_a_q_norm_g, m_a_k_norm_g, m_a_sinks, m_a_wo, m_b_w_pw1, m_b_b_pw1, m_b_w_dw, m_b_b_dw, m_b_ln_g, m_b_ln_b, m_b_w_pw2, m_b_b_pw2, m_c_w_in, m_c_w_conv, m_c_b_conv, m_c_dt_bias, m_c_A_log, m_c_D, m_c_norm_g, m_c_w_out, m_m_w1, m_m_w2, m_ple_w_proj, m_ple_w_gate, v_mix_norm_g, v_mlp_norm_g, v_ple_norm_g, v_a_wqkv, v_a_q_norm_g, v_a_k_norm_g, v_a_sinks, v_a_wo, v_b_w_pw1, v_b_b_pw1, v_b_w_dw, v_b_b_dw, v_b_ln_g, v_b_ln_b, v_b_w_pw2, v_b_b_pw2, v_c_w_in, v_c_w_conv, v_c_b_conv, v_c_dt_bias, v_c_A_log, v_c_D, v_c_norm_g, v_c_w_out, v_m_w1, v_m_w2, v_ple_w_proj, v_ple_w_gate):
    args = dict(locals())
    wl = {n: args[n] for n in WEIGHTS}
    ml = {n: args["m_" + n] for n in WEIGHTS}
    vl = {n: args["v_" + n] for n in WEIGHTS}
    d_model = x.shape[-1]
    small_sharded = [n for n, _ in SMALL_SHARDED]

    comm = _Exchange({n: wl[n].astype(BF16) for n, _ in BIG_SHARDED})
    first, token = comm.gather_start(0)
    whole = {n: wl[n] for n in REPLICATED}
    small = _all_gather([_pack([wl[n] for n in small_sharded]) + token[0, 0]], "gather_small")[0]
    for (n, axis), t in zip(SMALL_SHARDED, _unpack(small, [wl[n].shape for n in small_sharded], (N_DEV,))):
        whole[n] = _join_shards(t, axis)
    sq_err, grad_x, reduced, g_small, pending = _local_step(x[0], p[:, 0], loss_target[0], whole, comm, first)
    loss = lax.psum(0.5 * jnp.sum(sq_err) / d_model, ("x", "y", "c"))

    small_names = list(REPLICATED) + small_sharded
    summed = _sum_blocks(_all_gather([_pack([g_small[n] for n in small_names])], "gather_small_grads")[0],
                         "sum_small_grads")
    _finish_reduce(comm, pending, summed, reduced)
    g_big = {n: jnp.stack([g[k] for k in sorted(g)]) for n, g in reduced.items()}
    g_small = dict(zip(small_names, _unpack(summed, [g_small[n].shape for n in small_names])))
    me = 4 * lax.axis_index("x") + 2 * lax.axis_index("y") + lax.axis_index("c")
    for n, axis in SMALL_SHARDED:
        size = wl[n].shape[axis]
        g_small[n] = lax.dynamic_slice_in_dim(g_small[n], me * size, size, axis)

    out = {}
    for n, _ in BIG_SHARDED:
        shape = wl[n].shape
        two_d = (shape[0] * shape[1], shape[2])
        delta, new_m, new_v = _adamw(wl[n].reshape(two_d), g_big[n].reshape(two_d), ml[n].reshape(two_d),
                                     vl[n].reshape(two_d), "adamw_" + n)
        out.update({"grad_" + n: g_big[n], "delta_" + n: delta.reshape(shape), "new_m_" + n: new_m.reshape(shape),
                    "new_v_" + n: new_v.reshape(shape)})
    g_slab = _pack([g_small[n] for n in small_names])
    delta, new_m, new_v = _adamw(_pack([wl[n] for n in small_names]), g_slab, _pack([ml[n] for n in small_names]),
                                 _pack([vl[n] for n in small_names]), "adamw_small")
    for kind, slab in (("delta", delta), ("new_m", new_m), ("new_v", new_v)):
        out.update({kind + "_" + n: a for n, a in zip(small_names, _unpack(slab, [wl[n].shape for n in small_names]))})
    out.update({"grad_" + n: g_small[n] for n in small_names})
    return (loss, grad_x[None], *[out[k + "_" + n] for k in ("grad", "delta", "new_m", "new_v") for n in WEIGHTS])
```

```python
import math
from typing import Any, NamedTuple

import jax
import jax.numpy as jnp
from jax import lax
from jax.experimental import pallas as pl
from jax.experimental.pallas import tpu as pltpu

F32 = jnp.float32
BF16 = jnp.bfloat16
SDS = jax.ShapeDtypeStruct

EPS = 1e-6
N_MIXERS = 3
HEAD_DIM = 64
Q_PER_KV = 4
ATTN_BLOCK = 128
SSM_P = 64
SSM_R = 4
SSM_N = 128
SSM_L = 128
ADAM_LR, ADAM_B1, ADAM_B2, ADAM_EPS, ADAM_WD, ADAM_STEP = 0.001, 0.9, 0.999, 1e-08, 0.01, 10
N_DEV = 8
PACK_COLS = 1024
PACK_ROW_ALIGN = 16
VMEM_LIMIT = 56 * 1024 * 1024
MM_VMEM_BUDGET = 40 * 1024 * 1024
ROW_VMEM_BUDGET = 24 * 1024 * 1024
CONV_PAD = 32
CONV_ROWS = 64
NEG_BIG = -1e30

BIG_SHARDED = (("a_wqkv", 2), ("a_wo", 1), ("b_w_pw1", 2), ("b_w_pw2", 1), ("c_w_in", 2), ("c_w_out", 1),
               ("m_w1", 2), ("m_w2", 1), ("ple_w_proj", 2), ("ple_w_gate", 1))
SMALL_SHARDED = (("b_w_dw", 2), ("c_w_conv", 2), ("c_b_conv", 1), ("c_norm_g", 1))
REPLICATED = ("mix_norm_g", "mlp_norm_g", "ple_norm_g", "a_q_norm_g", "a_k_norm_g", "a_sinks", "b_b_pw1", "b_b_dw",
              "b_ln_g", "b_ln_b", "b_b_pw2", "c_dt_bias", "c_A_log", "c_D")
DIRECT = frozenset(("a_wo", "b_w_pw1", "b_w_pw2", "c_w_out", "m_w1", "m_w2", "ple_w_proj", "ple_w_gate"))
SHARD_AXIS = dict(BIG_SHARDED + SMALL_SHARDED)
MIXER_WEIGHTS = (("a_wqkv", "a_wo"), ("b_w_pw1", "b_w_pw2"), ("c_w_in", "c_w_out"))
LAYER_WEIGHTS = ("m_w1", "m_w2", "ple_w_proj", "ple_w_gate")
WEIGHTS = ("mix_norm_g", "mlp_norm_g", "ple_norm_g", "a_wqkv", "a_q_norm_g", "a_k_norm_g", "a_sinks", "a_wo",
           "b_w_pw1", "b_b_pw1", "b_w_dw", "b_b_dw", "b_ln_g", "b_ln_b", "b_w_pw2", "b_b_pw2", "c_w_in", "c_w_conv",
           "c_b_conv", "c_dt_bias", "c_A_log", "c_D", "c_norm_g", "c_w_out", "m_w1", "m_w2", "ple_w_proj",
           "ple_w_gate")


def _pcall(body, **kw):
    return pl.pallas_call(body, **kw)


def _params(*sem):
    return pltpu.CompilerParams(dimension_semantics=sem, vmem_limit_bytes=VMEM_LIMIT)


def _sigmoid(x):
    return 1.0 / (1.0 + jnp.exp(-x))


def _dsilu(x):
    s = _sigmoid(x)
    return s * (1.0 + x * (1.0 - s))


def _colsum(x):
    return jnp.sum(x, axis=0, keepdims=True)


def _rowmean(x):
    return jnp.mean(x, axis=-1, keepdims=True)


def _rowwise(fn, rows, pars, row_outs, acc_outs=(), *, name, tm=None):
    rows = [r if isinstance(r, tuple) else (r, r.shape[1], 0) for r in rows]
    t = rows[0][0].shape[0]
    per_row = sum(w * a.dtype.itemsize for a, w, _ in rows) + sum(w * jnp.dtype(d).itemsize for w, d in row_outs)
    if tm is None:
        tm = 1024
        while tm > 8 and (t % tm or 2 * tm * per_row > ROW_VMEM_BUDGET):
            tm //= 2
    assert t % tm == 0, (name, t, tm)
    n_in, n_row = len(rows) + len(pars), len(row_outs)

    def body(*refs):
        outs = fn(*[r[...] for r in refs[:n_in]])
        outs = outs if isinstance(outs, tuple) else (outs,)
        for ref, o in zip(refs[n_in:n_in + n_row], outs[:n_row]):
            ref[...] = o.astype(ref.dtype)
        acc_refs = refs[n_in + n_row:]
        if acc_refs:
            @pl.when(pl.program_id(0) == 0)
            def _():
                for ref in acc_refs:
                    ref[...] = jnp.zeros_like(ref)
            for ref, o in zip(acc_refs, outs[n_row:]):
                ref[...] += o

    in_specs = [pl.BlockSpec((tm, w), lambda i, cb=cb: (i, cb)) for _, w, cb in rows]
    in_specs += [pl.BlockSpec(p.shape, lambda i: (0, 0)) for p in pars]
    out_specs = [pl.BlockSpec((tm, w), lambda i: (i, 0)) for w, _ in row_outs]
    out_specs += [pl.BlockSpec((1, w), lambda i: (0, 0)) for w in acc_outs]
    out_shape = [SDS((t, w), d) for w, d in row_outs] + [SDS((1, w), F32) for w in acc_outs]
    return _pcall(body, grid=(t // tm,), in_specs=in_specs, out_specs=out_specs, out_shape=out_shape, name=name,
                  compiler_params=_params("arbitrary"))(*[a for a, _, _ in rows], *pars)


_DOT_DIMS = {"nn": (((1,), (0,)), ((), ())), "nt": (((1,), (1,)), ((), ())), "tn": (((0,), (0,)), ((), ()))}


class _Sharded(NamedTuple):
    g: Any
    layer: int
    axis: int


def _mm(a, b, form, out_dtypes, *, name, epi=None, rows=(), pars=(), shard_out=None):
    sharded = isinstance(b, _Sharded)
    if sharded:
        wa, wb = b.g.shape[2:]
        b_shape = (N_DEV * wa, wb) if b.axis == 0 else (wa, N_DEV * wb)
        b_item, layer = b.g.dtype.itemsize, b.layer
    else:
        b_shape, b_item = b.shape, b.dtype.itemsize
    if form == "tn":
        k, m = a.shape
        n = b_shape[1]
    else:
        m, k = a.shape
        n = b_shape[0] if form == "nt" else b_shape[1]
    rows = [r if isinstance(r, tuple) else (r, 0) for r in rows]
    tn = next((c for c in (512, 256, 128) if n % c == 0), n)
    b_shards = tn // wb if sharded and b.axis == 1 and form == "nn" and wb < tn else 1
    out_rows, out_cols = (m // N_DEV if shard_out == 0 else None), (n // N_DEV if shard_out == 1 else None)
    if out_cols is not None and out_cols > tn:
        tn = next(c for c in (512, 256, 128) if out_cols % c == 0)

    def need(tm_):
        blk = tm_ * k * a.dtype.itemsize + tn * k * b_item
        blk += sum(tm_ * tn * jnp.dtype(d).itemsize for d in out_dtypes) + sum(tm_ * tn * r.dtype.itemsize for r, _ in rows)
        return 2 * blk

    tm = next((c for c in (1024, 512, 256, 128, 64, 32, 16) if m % c == 0 and need(c) <= MM_VMEM_BUDGET
               and (out_rows is None or out_rows % c == 0 or c % out_rows == 0)), None)
    assert tm is not None, (name, m, k, n)
    nx = len(rows) + len(pars)
    pieces = sharded and form == "nt" and b.axis == 1

    def body(*refs):
        av = refs[0][...].astype(BF16)
        if pieces:
            acc = jnp.zeros((tm, tn), F32)
            for d in range(N_DEV):
                acc += lax.dot_general(av[:, d * wb:(d + 1) * wb], refs[1][d], _DOT_DIMS["nt"], preferred_element_type=F32)
        else:
            bv = refs[1][...]
            if b_shards > 1:
                acc = jnp.concatenate([jnp.dot(av, bv[d], preferred_element_type=F32) for d in range(b_shards)], axis=1)
            else:
                if sharded and bv.ndim == 3:
                    bv = bv.reshape(bv.shape[0] * bv.shape[1], bv.shape[2])
                acc = lax.dot_general(av, bv.astype(BF16), _DOT_DIMS[form], preferred_element_type=F32)
        outs = epi(acc, *[r[...] for r in refs[2:2 + nx]]) if epi else acc
        outs = outs if isinstance(outs, tuple) else (outs,)
        for ref, o in zip(refs[2 + nx:], outs):
            if out_rows is not None and tm > out_rows:
                ref[...] = o.reshape(tm // out_rows, out_rows, tn).astype(ref.dtype)
            elif out_cols is not None and tn > out_cols:
                for d in range(tn // out_cols):
                    ref[d] = o[:, d * out_cols:(d + 1) * out_cols].astype(ref.dtype)
            else:
                ref[...] = o.astype(ref.dtype)

    a_spec = pl.BlockSpec((k, tm), lambda i, j: (0, i)) if form == "tn" else pl.BlockSpec((tm, k), lambda i, j: (i, 0))
    if not sharded:
        b_spec = pl.BlockSpec((tn, k), lambda i, j: (j, 0)) if form == "nt" else pl.BlockSpec((k, tn), lambda i, j: (0, j))
    elif form == "nn" and b.axis == 0:
        b_spec = pl.BlockSpec((N_DEV, None, wa, tn), lambda i, j: (0, layer, 0, j))
    elif form == "nn" and b_shards > 1:
        b_spec = pl.BlockSpec((b_shards, None, wa, wb), lambda i, j: (j, layer, 0, 0))
    elif form == "nn":
        per = wb // tn
        b_spec = pl.BlockSpec((None, None, wa, tn), lambda i, j: (j // per, layer, 0, j % per))
    elif form == "nt" and b.axis == 0 and tn >= wa:
        b_spec = pl.BlockSpec((tn // wa, None, wa, k), lambda i, j: (j, layer, 0, 0))
    elif form == "nt" and b.axis == 0:
        per = wa // tn
        b_spec = pl.BlockSpec((None, None, tn, k), lambda i, j: (j // per, layer, j % per, 0))
    else:
        assert form == "nt", (name, form)
        b_spec = pl.BlockSpec((N_DEV, None, tn, wb), lambda i, j: (0, layer, j, 0))
    in_specs = [a_spec, b_spec]
    in_specs += [pl.BlockSpec((tm, tn), lambda i, j, off=off: (i, j + off)) for _, off in rows]
    in_specs += [pl.BlockSpec((1, tn), lambda i, j: (0, j)) for _ in pars]
    if shard_out == 0 and tm > out_rows:
        out_specs = [pl.BlockSpec((tm // out_rows, out_rows, tn), lambda i, j: (i, 0, j)) for _ in out_dtypes]
        out_shape = [SDS((N_DEV, out_rows, n), d) for d in out_dtypes]
    elif shard_out == 0:
        per_m = out_rows // tm
        out_specs = [pl.BlockSpec((None, tm, tn), lambda i, j: (i // per_m, i % per_m, j)) for _ in out_dtypes]
        out_shape = [SDS((N_DEV, out_rows, n), d) for d in out_dtypes]
    elif shard_out == 1 and tn > out_cols:
        out_specs = [pl.BlockSpec((tn // out_cols, tm, out_cols), lambda i, j: (j, i, 0)) for _ in out_dtypes]
        out_shape = [SDS((N_DEV, m, out_cols), d) for d in out_dtypes]
    elif shard_out == 1:
        per_n = out_cols // tn
        out_specs = [pl.BlockSpec((None, tm, tn), lambda i, j: (j // per_n, i, j % per_n)) for _ in out_dtypes]
        out_shape = [SDS((N_DEV, m, out_cols), d) for d in out_dtypes]
    else:
        out_specs = [pl.BlockSpec((tm, tn), lambda i, j: (i, j)) for _ in out_dtypes]
        out_shape = [SDS((m, n), d) for d in out_dtypes]
    res = _pcall(body, grid=(m // tm, n // tn), in_specs=in_specs, out_specs=out_specs, out_shape=out_shape, name=name,
                 compiler_params=_params("arbitrary", "arbitrary"))(a, b.g if sharded else b, *[r for r, _ in rows], *pars)
    return res[0] if len(out_dtypes) == 1 else res


def _rms_fwd(x, g, name, after=None):
    def fn(x, g, *_):
        return x * lax.rsqrt(_rowmean(x * x) + EPS) * g
    return _rowwise(fn, [x], [g] + ([] if after is None else [after]), [(x.shape[1], BF16)], name=name)[0]


def _rms_bwd(dy, x, g, dres, name):
    def fn(dy, x, dres, g):
        rstd = lax.rsqrt(_rowmean(x * x) + EPS)
        xh = x * rstd
        dxh = dy * g
        tot = dres + rstd * (dxh - xh * _rowmean(dxh * xh))
        return tot, tot, _colsum(dy * xh), _colsum(tot)
    d = x.shape[1]
    return _rowwise(fn, [dy, x, dres], [g], [(d, F32), (d, BF16)], [d, d], name=name)


def _ple_bwd(dh, gl, pp, name, after=None):
    def fn(dh, gl, pp, *_):
        gate = _sigmoid(gl)
        return dh * pp.astype(F32) * gate * (1.0 - gate), dh * gate
    d = dh.shape[1]
    return _rowwise(fn, [dh, gl, pp], [] if after is None else [after], [(d, BF16), (d, BF16)], name=name)


def _loss_and_grad(h, tgt, name):
    d = h.shape[1]

    def fn(h, tgt):
        err = h - tgt
        return err * (1.0 / d), _colsum(err * err)
    return _rowwise(fn, [h, tgt], [], [(d, F32)], [d], name=name)


def _add(a, b, name):
    return _rowwise(lambda a, b: a + b, [a, b], [], [(a.shape[1], F32)], name=name)[0]


def _adamw(w, g, m, v, name):
    def fn(w, g, m, v):
        m = ADAM_B1 * m + (1.0 - ADAM_B1) * g
        v = ADAM_B2 * v + (1.0 - ADAM_B2) * (g * g)
        m_hat = m / (1.0 - ADAM_B1 ** ADAM_STEP)
        v_hat = v / (1.0 - ADAM_B2 ** ADAM_STEP)
        return -ADAM_LR * (m_hat / (jnp.sqrt(v_hat) + ADAM_EPS) + ADAM_WD * w), m, v
    c = w.shape[1]
    return _rowwise(fn, [w, g, m, v], [], [(c, F32)] * 3, name=name)


def _sum_blocks(x, name):
    n = x.shape[0]

    def body(x_ref, o_ref):
        acc = x_ref[0]
        for j in range(1, n):
            acc = acc + x_ref[j]
        o_ref[...] = acc
    return _pcall(body, out_shape=SDS(x.shape[1:], x.dtype), name=name, compiler_params=_params())(x)


def _shifted(chunk, off, rows):
    if off % 8 == 0:
        return chunk[off:off + rows]
    return pltpu.roll(chunk, chunk.shape[0] - off, 0)[:rows]


def _conv_cols(c):
    return 256 if c % 256 == 0 else 128


def _rows_at(i, tb):
    return pl.ds(i * tb, tb) if isinstance(i, int) else pl.ds(pl.multiple_of(i * tb, 8), tb)


def _chunk_before(ref, i, tb, pad):
    if isinstance(i, int) and i == 0:
        return jnp.concatenate([jnp.zeros((pad, ref.shape[1]), F32), ref[0:tb, :]], axis=0)
    if isinstance(i, int):
        return ref[i * tb - pad:(i + 1) * tb, :]
    return ref[pl.ds(pl.multiple_of(i * tb - pad, 8), tb + pad), :]


def _chunk_after(ref, i, tb, pad, last):
    if last:
        return jnp.concatenate([ref[i * tb:(i + 1) * tb, :], jnp.zeros((pad, ref.shape[1]), F32)], axis=0)
    if isinstance(i, int):
        return ref[i * tb:(i + 1) * tb + pad, :]
    return ref[pl.ds(pl.multiple_of(i * tb, 8), tb + pad), :]


def _dwconv_fwd(y, w, b, silu, name):
    t, c = y.shape
    taps = w.shape[0]
    cb, tb, pad = _conv_cols(c), CONV_ROWS, CONV_PAD

    def body(y_ref, w_ref, b_ref, *out_refs):
        def step(i, carry):
            chunk = _chunk_before(y_ref, i, tb, pad)
            acc = jnp.broadcast_to(b_ref[...], (tb, cb))
            for k in range(taps):
                acc = acc + _shifted(chunk, pad - (taps - 1) + k, tb) * w_ref[k:k + 1, :]
            rows = _rows_at(i, tb)
            out_refs[0][rows, :] = acc
            if silu:
                out_refs[1][rows, :] = acc * _sigmoid(acc)
            return carry
        step(0, 0)
        lax.fori_loop(1, t // tb, step, 0)

    n_out = 2 if silu else 1
    return _pcall(body, grid=(c // cb,),
                  in_specs=[pl.BlockSpec((t, cb), lambda j: (0, j)), pl.BlockSpec((taps, cb), lambda j: (0, j)),
                            pl.BlockSpec((1, cb), lambda j: (0, j))],
                  out_specs=[pl.BlockSpec((t, cb), lambda j: (0, j))] * n_out,
                  out_shape=[SDS((t, c), F32)] * n_out, name=name, compiler_params=_params("arbitrary"))(y, w, b)


def _dwconv_bwd(y, dout, w, name):
    t, c = y.shape
    taps = w.shape[0]
    cb, tb, pad = _conv_cols(c), CONV_ROWS, CONV_PAD
    taps_pad = -(-taps // 8) * 8
    n_steps = t // tb
    assert n_steps >= 2

    def body(y_ref, d_ref, w_ref, dy_ref, dw_ref, db_ref):
        dw_ref[...] = jnp.zeros_like(dw_ref)

        def step(i, db, last=False):
            ychunk = _chunk_before(y_ref, i, tb, pad)
            dchunk = _chunk_after(d_ref, i, tb, pad, last)
            d0 = dchunk[:tb]
            acc = jnp.zeros((tb, cb), F32)
            for k in range(taps):
                acc = acc + _shifted(dchunk, taps - 1 - k, tb) * w_ref[k:k + 1, :]
                dw_ref[k:k + 1, :] += _colsum(d0 * _shifted(ychunk, pad - (taps - 1) + k, tb))
            dy_ref[_rows_at(i, tb), :] = acc
            return db + _colsum(d0)
        db = step(0, jnp.zeros((1, cb), F32))
        db = lax.fori_loop(1, n_steps - 1, step, db)
        db_ref[...] = step(n_steps - 1, db, last=True)

    dy, dw, db = _pcall(
        body, grid=(c // cb,),
        in_specs=[pl.BlockSpec((t, cb), lambda j: (0, j)), pl.BlockSpec((t, cb), lambda j: (0, j)),
                  pl.BlockSpec((taps, cb), lambda j: (0, j))],
        out_specs=[pl.BlockSpec((t, cb), lambda j: (0, j)), pl.BlockSpec((taps_pad, cb), lambda j: (0, j)),
                   pl.BlockSpec((1, cb), lambda j: (0, j))],
        out_shape=[SDS((t, c), F32), SDS((taps_pad, c), F32), SDS((1, c), F32)], name=name,
        compiler_params=_params("arbitrary"))(y, dout, w)
    return dy, dw[:taps], db


def _head_norm(x, g):
    rstd = lax.rsqrt(_rowmean(x * x) + EPS)
    xh = x * rstd
    return xh * g, xh, rstd


def _head_norm_bwd(dy, xh, rstd, g):
    dxh = dy * g
    return rstd * (dxh - xh * _rowmean(dxh * xh)), _colsum(dy * xh)


def _attn_probs(q_ref, kp_ref, kc_ref, gq_ref, gk_ref, sinks_ref, n_heads):
    g, blk, hd = Q_PER_KV, ATTN_BLOCK, HEAD_DIM
    kv, n = pl.program_id(0), pl.program_id(1)
    q_raw = q_ref[...].reshape(g * blk, hd)
    k_raw = jnp.concatenate([kp_ref[...], kc_ref[...]], axis=0)
    qn, qh, q_rstd = _head_norm(q_raw, gq_ref[...])
    kn, kh, k_rstd = _head_norm(k_raw, gk_ref[...])
    s = lax.dot_general(qn.astype(BF16), kn.astype(BF16), _DOT_DIMS["nt"], preferred_element_type=F32) * hd ** -0.5
    row = lax.broadcasted_iota(jnp.int32, (g * blk, 1), 0)
    g_row = lax.shift_right_logical(row, 7)
    qi = lax.bitwise_and(row, blk - 1)
    kj = lax.broadcasted_iota(jnp.int32, (1, 2 * blk), 1)
    dist = qi - kj + blk
    valid = (dist >= 0) & (dist < blk) & (kj >= jnp.where(n > 0, 0, blk))
    head = (kv * g + g_row + 1).astype(F32)
    slope = jnp.exp(head * (-8.0 * math.log(2.0) / n_heads))
    s = jnp.where(valid, s - slope * dist.astype(F32), NEG_BIG)
    sink = jnp.zeros((g * blk, 1), F32)
    for gi in range(g):
        sink = jnp.where(g_row == gi, sinks_ref[kv * g + gi], sink)
    m = jnp.maximum(jnp.max(s, axis=1, keepdims=True), sink)
    e = jnp.exp(s - m)
    e_sink = jnp.exp(sink - m)
    inv = 1.0 / (jnp.sum(e, axis=1, keepdims=True) + e_sink)
    return (q_raw, qn, qh, q_rstd), (k_raw, kn, kh, k_rstd), e * inv, e_sink * inv


def _attn_specs(t):
    g, blk, hd = Q_PER_KV, ATTN_BLOCK, HEAD_DIM
    q_spec = pl.BlockSpec((None, g, blk, hd), lambda a, n: (a, 0, n, 0))
    prev_spec = pl.BlockSpec((None, blk, hd), lambda a, n: (a, jnp.maximum(n - 1, 0), 0))
    cur_spec = pl.BlockSpec((None, blk, hd), lambda a, n: (a, n, 0))
    gain_spec = pl.BlockSpec((1, hd), lambda a, n: (0, 0))
    sink_spec = pl.BlockSpec(memory_space=pltpu.SMEM)
    return q_spec, prev_spec, cur_spec, gain_spec, sink_spec


def _attn_fwd(q, k, v, gq, gk, sinks, name):
    n_kv, g, t, hd = q.shape
    blk = ATTN_BLOCK
    n_heads = n_kv * g

    def body(q_ref, kp_ref, kc_ref, vp_ref, vc_ref, gq_ref, gk_ref, sinks_ref, o_ref):
        _, _, p, _ = _attn_probs(q_ref, kp_ref, kc_ref, gq_ref, gk_ref, sinks_ref, n_heads)
        vb = jnp.concatenate([vp_ref[...], vc_ref[...]], axis=0).astype(BF16)
        o = jnp.dot(p.astype(BF16), vb, preferred_element_type=F32)
        o_ref[...] = o.reshape(g, blk, hd).astype(o_ref.dtype)

    q_spec, prev_spec, cur_spec, gain_spec, sink_spec = _attn_specs(t)
    return _pcall(body, grid=(n_kv, t // blk),
                  in_specs=[q_spec, prev_spec, cur_spec, prev_spec, cur_spec, gain_spec, gain_spec, sink_spec],
                  out_specs=q_spec, out_shape=SDS(q.shape, BF16), name=name,
                  compiler_params=_params("arbitrary", "arbitrary"))(q, k, k, v, v, gq, gk, sinks)


def _attn_bwd(q, k, v, do, gq, gk, sinks, name):
    n_kv, g, t, hd = q.shape
    blk = ATTN_BLOCK
    n_heads = n_kv * g

    def body(q_ref, kp_ref, kc_ref, vp_ref, vc_ref, do_ref, gq_ref, gk_ref, sinks_ref,
             dq_ref, dkp_ref, dkc_ref, dvp_ref, dvc_ref, dgq_ref, dgk_ref, dsink_ref):
        kv, n = pl.program_id(0), pl.program_id(1)
        (_, qn, qh, q_rstd), (_, kn, kh, k_rstd), p, p_sink = _attn_probs(
            q_ref, kp_ref, kc_ref, gq_ref, gk_ref, sinks_ref, n_heads)
        vb = jnp.concatenate([vp_ref[...], vc_ref[...]], axis=0).astype(BF16)
        dob = do_ref[...].reshape(g * blk, hd).astype(BF16)
        dp = lax.dot_general(dob, vb, _DOT_DIMS["nt"], preferred_element_type=F32)
        delta = jnp.sum(p * dp, axis=1, keepdims=True)
        ds = (p * (dp - delta) * hd ** -0.5).astype(BF16)
        dv = lax.dot_general(p.astype(BF16), dob, _DOT_DIMS["tn"], preferred_element_type=F32)
        dqn = jnp.dot(ds, kn.astype(BF16), preferred_element_type=F32)
        dkn = lax.dot_general(ds, qn.astype(BF16), _DOT_DIMS["tn"], preferred_element_type=F32)
        dq, dgq = _head_norm_bwd(dqn, qh, q_rstd, gq_ref[...])
        dk, dgk = _head_norm_bwd(dkn, kh, k_rstd, gk_ref[...])
        dq_ref[...] = dq.reshape(g, blk, hd)
        dkp_ref[...] = dk[:blk]
        dkc_ref[...] = dk[blk:]
        dvp_ref[...] = dv[:blk]
        dvc_ref[...] = dv[blk:]

        @pl.when((kv == 0) & (n == 0))
        def _():
            dgq_ref[...] = jnp.zeros_like(dgq_ref)
            dgk_ref[...] = jnp.zeros_like(dgk_ref)

        @pl.when(n == 0)
        def _():
            dsink_ref[...] = jnp.zeros_like(dsink_ref)
        dgq_ref[...] += dgq
        dgk_ref[...] += dgk
        dsink = jnp.sum((-p_sink * delta).reshape(g, blk, 1), axis=1)
        dsink_ref[...] += jnp.broadcast_to(dsink, (g, 128))

    q_spec, prev_spec, cur_spec, gain_spec, sink_spec = _attn_specs(t)
    kv_shape = SDS(k.shape, F32)
    return _pcall(
        body, grid=(n_kv, t // blk),
        in_specs=[q_spec, prev_spec, cur_spec, prev_spec, cur_spec, q_spec, gain_spec, gain_spec, sink_spec],
        out_specs=[q_spec, cur_spec, cur_spec, cur_spec, cur_spec, gain_spec, gain_spec,
                   pl.BlockSpec((None, g, 128), lambda a, n: (a, 0, 0))],
        out_shape=[SDS(q.shape, F32), kv_shape, kv_shape, kv_shape, kv_shape, SDS((1, hd), F32), SDS((1, hd), F32),
                   SDS((n_kv, g, 128), F32)],
        name=name, compiler_params=_params("arbitrary", "arbitrary"))(q, k, k, v, v, do, gq, gk, sinks)


def _col_head(width):
    return lax.shift_right_logical(lax.broadcasted_iota(jnp.int32, (1, width), 1), 6)


def _expand_heads(v, col_head):
    out = jnp.zeros((v.shape[0], col_head.shape[1]), F32)
    for r in range(SSM_R):
        out = jnp.where(col_head == r, v[:, r:r + 1], out)
    return out


def _head_sums(m, col_head):
    lane = lax.broadcasted_iota(jnp.int32, (1, SSM_R), 1)
    out = jnp.zeros((m.shape[0], SSM_R), F32)
    for r in range(SSM_R):
        out = jnp.where(lane == r, jnp.sum(jnp.where(col_head == r, m, 0.0), axis=1, keepdims=True), out)
    return out


def _ssd_specs(d_inner, n_groups, n_chunks, rev):
    l, w, n = SSM_L, SSM_R * SSM_P, SSM_N

    def cc(c):
        return n_chunks - 1 - c if rev else c
    x_spec = pl.BlockSpec((l, w), lambda g, c: (cc(c), g))
    b_spec = pl.BlockSpec((l, n), lambda g, c: (cc(c), d_inner // n + g))
    c_spec = pl.BlockSpec((l, n), lambda g, c: (cc(c), d_inner // n + n_groups + g))
    col_spec = pl.BlockSpec((None, l, SSM_R), lambda g, c: (g, cc(c), 0))
    row_spec = pl.BlockSpec((None, None, SSM_R, l), lambda g, c: (g, cc(c), 0, 0))
    state_spec = pl.BlockSpec((None, None, n, w), lambda g, c: (g, cc(c), 0, 0))
    return x_spec, b_spec, c_spec, col_spec, row_spec, state_spec


def _decay_matrix(a, a_row, r):
    l = SSM_L
    causal = lax.broadcasted_iota(jnp.int32, (l, l), 0) >= lax.broadcasted_iota(jnp.int32, (l, l), 1)
    return jnp.exp(jnp.where(causal, a[:, r:r + 1] - a_row[r:r + 1, :], NEG_BIG))


def _ssd_fwd(xbc, dt_g, cs_g, cs_row, d_inner, name):
    t = xbc.shape[0]
    l, w, n = SSM_L, SSM_R * SSM_P, SSM_N
    n_groups, n_chunks = d_inner // w, t // l

    def body(x_ref, b_ref, c_ref, dt_ref, a_ref, ar_ref, y_ref, prev_ref, h_scr):
        @pl.when(pl.program_id(1) == 0)
        def _():
            h_scr[...] = jnp.zeros_like(h_scr)
        col_head = _col_head(w)
        a, a_row = a_ref[...], ar_ref[...]
        h = h_scr[...]
        prev_ref[...] = h
        xdt = x_ref[...] * _expand_heads(dt_ref[...], col_head)
        xb, bb, cb = xdt.astype(BF16), b_ref[...].astype(BF16), c_ref[...].astype(BF16)
        a_last = a[l - 1:l, :]
        cbt = lax.dot_general(cb, bb, _DOT_DIMS["nt"], preferred_element_type=F32)
        y = jnp.zeros((l, w), F32)
        for r in range(SSM_R):
            m = (cbt * _decay_matrix(a, a_row, r)).astype(BF16)
            y = jnp.where(col_head == r, jnp.dot(m, xb, preferred_element_type=F32), y)
        y_off = jnp.dot(cb, h.astype(BF16), preferred_element_type=F32)
        y_ref[...] = y + _expand_heads(jnp.exp(a), col_head) * y_off
        fx = (xdt * _expand_heads(jnp.exp(a_last - a), col_head)).astype(BF16)
        states = lax.dot_general(bb, fx, _DOT_DIMS["tn"], preferred_element_type=F32)
        h_scr[...] = _expand_heads(jnp.exp(a_last), col_head) * h + states

    x_spec, b_spec, c_spec, col_spec, row_spec, state_spec = _ssd_specs(d_inner, n_groups, n_chunks, False)
    return _pcall(body, grid=(n_groups, n_chunks),
                  in_specs=[x_spec, b_spec, c_spec, col_spec, col_spec, row_spec],
                  out_specs=[x_spec, state_spec],
                  out_shape=[SDS((t, d_inner), F32), SDS((n_groups, n_chunks, n, w), F32)],
                  scratch_shapes=[pltpu.VMEM((n, w), F32)], name=name,
                  compiler_params=_params("arbitrary", "arbitrary"))(xbc, xbc, xbc, dt_g, cs_g, cs_row)


def _ssd_bwd(xbc, dt_g, cs_g, cs_row, prev, dy, d_inner, name):
    t = xbc.shape[0]
    l, w, n = SSM_L, SSM_R * SSM_P, SSM_N
    n_groups, n_chunks = d_inner // w, t // l

    def body(x_ref, b_ref, c_ref, dt_ref, a_ref, ar_ref, prev_ref, dy_ref,
             dx_ref, db_ref, dc_ref, ddtx_ref, dda_ref, dh_scr):
        @pl.when(pl.program_id(1) == 0)
        def _():
            dh_scr[...] = jnp.zeros_like(dh_scr)
        col_head = _col_head(w)
        a, a_row = a_ref[...], ar_ref[...]
        x, dyv = x_ref[...], dy_ref[...]
        h, dhn = prev_ref[...], dh_scr[...]
        dtx = _expand_heads(dt_ref[...], col_head)
        xdt = x * dtx
        a_last = a[l - 1:l, :]
        e_last = jnp.exp(a_last)
        xb, bb, cb = xdt.astype(BF16), b_ref[...].astype(BF16), c_ref[...].astype(BF16)
        hb, dhnb, dyb = h.astype(BF16), dhn.astype(BF16), dyv.astype(BF16)
        edy = (_expand_heads(jnp.exp(a), col_head) * dyv).astype(BF16)
        dc = lax.dot_general(edy, hb, _DOT_DIMS["nt"], preferred_element_type=F32)
        dh_scr[...] = (_expand_heads(e_last, col_head) * dhn
                       + lax.dot_general(cb, edy, _DOT_DIMS["tn"], preferred_element_type=F32))
        fx = xdt * _expand_heads(jnp.exp(a_last - a), col_head)
        g1 = jnp.dot(bb, dhnb, preferred_element_type=F32)
        db = lax.dot_general(fx.astype(BF16), dhnb, _DOT_DIMS["nt"], preferred_element_type=F32)
        q = _head_sums(fx * g1, col_head)
        d_e_last = _head_sums(_colsum(dhn * h), col_head)
        cbt = lax.dot_general(cb, bb, _DOT_DIMS["nt"], preferred_element_type=F32)
        dcbt = jnp.zeros((l, l), F32)
        dxd = jnp.zeros((l, w), F32)
        dda_diag = jnp.zeros((l, SSM_R), F32)
        rows_l = lax.broadcasted_iota(jnp.int32, (l, l), 0)
        cols_l = lax.broadcasted_iota(jnp.int32, (l, l), 1)
        upper = (cols_l >= rows_l).astype(F32)
        head_lane = lax.broadcasted_iota(jnp.int32, (1, SSM_R), 1)
        for r in range(SSM_R):
            decay = _decay_matrix(a, a_row, r)
            dyr = jnp.where(col_head == r, dyv, 0.0).astype(BF16)
            dm = lax.dot_general(dyr, xb, _DOT_DIMS["nt"], preferred_element_type=F32)
            dcbt = dcbt + dm * decay
            m = cbt * decay
            dxr = lax.dot_general(m.astype(BF16), dyb, _DOT_DIMS["tn"], preferred_element_type=F32)
            dxd = jnp.where(col_head == r, dxr, dxd)
            below = jnp.dot(upper.astype(BF16), (dm * m).astype(BF16), preferred_element_type=F32)
            dda_r = jnp.sum(jnp.where(cols_l < rows_l, below, 0.0), axis=1, keepdims=True)
            dda_diag = jnp.where(head_lane == r, dda_r, dda_diag)
        dcbb = dcbt.astype(BF16)
        dc_ref[...] = dc + jnp.dot(dcbb, bb, preferred_element_type=F32)
        db_ref[...] = db + lax.dot_general(dcbb, cb, _DOT_DIMS["tn"], preferred_element_type=F32)
        dxt = dxd + _expand_heads(jnp.exp(a_last - a), col_head) * g1
        dx_ref[...] = dxt * dtx
        ddtx_ref[...] = _head_sums(dxt * x, col_head)
        y_off = _expand_heads(jnp.exp(a), col_head) * jnp.dot(cb, hb, preferred_element_type=F32)
        row = lax.broadcasted_iota(jnp.int32, (l, 1), 0)
        da = _head_sums(dyv * y_off, col_head) + jnp.where(row == l - 1, e_last * d_e_last, 0.0)
        strict_lower = (cols_l < rows_l).astype(F32)
        dda_ref[...] = (dda_diag + jnp.dot(upper, da, precision=lax.Precision.HIGHEST, preferred_element_type=F32)
                        + jnp.dot(strict_lower, q, precision=lax.Precision.HIGHEST, preferred_element_type=F32))

    x_spec, b_spec, c_spec, col_spec, row_spec, state_spec = _ssd_specs(d_inner, n_groups, n_chunks, True)
    bc_out = pl.BlockSpec((l, n), lambda g, c: (n_chunks - 1 - c, g))
    return _pcall(body, grid=(n_groups, n_chunks),
                  in_specs=[x_spec, b_spec, c_spec, col_spec, col_spec, row_spec, state_spec, x_spec],
                  out_specs=[x_spec, bc_out, bc_out, col_spec, col_spec],
                  out_shape=[SDS((t, d_inner), F32), SDS((t, n_groups * n), F32), SDS((t, n_groups * n), F32),
                             SDS(dt_g.shape, F32), SDS(dt_g.shape, F32)],
                  scratch_shapes=[pltpu.VMEM((n, w), F32)], name=name,
                  compiler_params=_params("arbitrary", "arbitrary"))(xbc, xbc, xbc, dt_g, cs_g, cs_row, prev, dy)


def _softplus(x):
    return jnp.maximum(x, 0.0) + jnp.log1p(jnp.exp(-jnp.abs(x)))


def _dt_fwd(dt_raw, bias, a_log, name):
    l = SSM_L

    def fn(raw, bias, a_log):
        dt = _softplus(raw + bias)
        lower = (lax.broadcasted_iota(jnp.int32, (l, l), 0) >= lax.broadcasted_iota(jnp.int32, (l, l), 1)).astype(F32)
        cs = jnp.dot(lower, dt * -jnp.exp(a_log), precision=lax.Precision.HIGHEST, preferred_element_type=F32)
        return dt, cs
    wd = dt_raw.shape[1]
    return _rowwise(fn, [dt_raw], [bias, a_log], [(wd, F32), (wd, F32)], name=name, tm=l)


def _dt_bwd(ddtx, dda, dt_raw, dt, bias, a_log, name):
    def fn(ddtx, dda, raw, dt, bias, a_log):
        a = -jnp.exp(a_log)
        draw = (ddtx + dda * a) * _sigmoid(raw + bias)
        return draw, _colsum(draw), _colsum(dda * dt) * a
    wd = dt_raw.shape[1]
    return _rowwise(fn, [ddtx, dda, dt_raw, dt], [bias, a_log], [(wd, BF16)], [wd, wd], name=name)


MESH_IDS = pl.DeviceIdType.MESH
ANY_SPEC = pl.BlockSpec(memory_space=pl.ANY)


def _mesh_pos():
    return lax.axis_index("x"), lax.axis_index("y"), lax.axis_index("c")


def _all_gather(arrays, name):
    n = len(arrays)

    def body(*refs):
        x_refs, out_refs = refs[:n], refs[n:2 * n]
        send_sems, recv_sems, local_sems = refs[2 * n:]
        mx, my, mc = _mesh_pos()
        me, sibling = (mx, my, mc), (mx, my, 1 - mc)
        chips = [(1 - mx, my), (mx, 1 - my), (1 - mx, 1 - my)]

        def copy(a, k, block, to, src=None):
            px, py, pc = block
            slot = out_refs[a].at[4 * px + 2 * py + pc]
            return pltpu.make_async_remote_copy(
                src_ref=slot if src is None else src, dst_ref=slot, send_sem=send_sems.at[7 * a + k],
                recv_sem=recv_sems.at[7 * a + k], device_id=to, device_id_type=MESH_IDS)

        mine = [pltpu.make_async_copy(x_refs[a], out_refs[a].at[4 * mx + 2 * my + mc], local_sems.at[a])
                for a in range(n)]
        sends = []
        for a in range(n):
            mine[a].start()
            first = [copy(a, 0, me, sibling, src=x_refs[a])]
            first += [copy(a, 1 + j, me, (*chip, mc), src=x_refs[a]) for j, chip in enumerate(chips)]
            for cp in first:
                cp.start()
            sends += first
        for j, chip in enumerate(chips):
            for a in range(n):
                copy(a, 1 + j, (*chip, mc), me).wait_recv()
                passed = copy(a, 4 + j, (*chip, mc), sibling)
                passed.start()
                sends.append(passed)
        for a in range(n):
            copy(a, 0, sibling, me).wait_recv()
            for j, chip in enumerate(chips):
                copy(a, 4 + j, (*chip, 1 - mc), me).wait_recv()
        for cp in sends:
            cp.wait_send()
        for cp in mine:
            cp.wait()

    return _pcall(body, out_shape=[SDS((N_DEV,) + x.shape, x.dtype) for x in arrays], in_specs=[ANY_SPEC] * n,
                  out_specs=[ANY_SPEC] * n,
                  scratch_shapes=[pltpu.SemaphoreType.DMA((7 * n,)), pltpu.SemaphoreType.DMA((7 * n,)),
                                  pltpu.SemaphoreType.DMA((n,))], name=name)(*arrays)


def _swap_with_sibling(arrays, name):
    n = len(arrays)

    def body(*refs):
        g_refs, out_refs = refs[:n], refs[n:2 * n]
        send_sems, recv_sems = refs[2 * n:]
        mx, my, mc = _mesh_pos()
        copies = [pltpu.make_async_remote_copy(
            src_ref=g_refs[a].at[2 * k + (1 - mc)], dst_ref=out_refs[a].at[k], send_sem=send_sems.at[4 * a + k],
            recv_sem=recv_sems.at[4 * a + k], device_id=(mx, my, 1 - mc), device_id_type=MESH_IDS)
            for a in range(n) for k in range(4)]
        for cp in copies:
            cp.start()
        for cp in copies:
            cp.wait()

    return _pcall(body, out_shape=[SDS((4,) + g.shape[1:], g.dtype) for g in arrays], in_specs=[ANY_SPEC] * n,
                  out_specs=[ANY_SPEC] * n,
                  scratch_shapes=[pltpu.SemaphoreType.DMA((4 * n,)), pltpu.SemaphoreType.DMA((4 * n,))],
                  name=name)(*arrays)


HBM_SPEC = pl.BlockSpec(memory_space=pltpu.HBM)
SEM_SPEC = pl.BlockSpec(memory_space=pltpu.SEMAPHORE)
SIDE_EFFECT = pltpu.SideEffectType.DATAFLOW_SIDE_EFFECTING


def _chip_copies(src_refs, dst_refs, send_sems, recv_sems, gather):
    mx, my, mc = _mesh_pos()
    peers = [(1 - mx, my, mc), (mx, 1 - my, mc), (1 - mx, 1 - my, mc)]
    if gather:
        peers = [(mx, my, 1 - mc)] + peers
    copies = []
    for a in range(len(src_refs)):
        for j, peer in enumerate(peers):
            src = src_refs[a] if gather else src_refs[a].at[2 * peer[0] + peer[1]]
            dst = dst_refs[a].at[4 * mx + 2 * my + mc] if gather else dst_refs[a].at[j]
            k = len(peers) * a + j
            copies.append(pltpu.make_async_remote_copy(src_ref=src, dst_ref=dst, send_sem=send_sems.at[k],
                                                       recv_sem=recv_sems.at[k], device_id=peer, device_id_type=MESH_IDS))
    return copies


def _chips_start(srcs, dsts, gather, name):
    n = len(srcs)
    n_sems = (4 if gather else 3) * n
    if dsts is None:
        dsts = [lax.empty((3,) + a.shape[1:], a.dtype) for a in srcs]

    def body(*refs):
        for cp in _chip_copies(refs[:n], refs[n:2 * n], refs[2 * n], refs[2 * n + 1], gather):
            cp.start()
        refs[-1][...] = jnp.zeros_like(refs[-1])

    res = _pcall(
        body, name=name,
        out_shape=(pltpu.SemaphoreType.DMA((n_sems,)), pltpu.SemaphoreType.DMA((n_sems,)),
                   *[pltpu.HBM(a.shape, a.dtype) for a in srcs], *[pltpu.HBM(d.shape, d.dtype) for d in dsts],
                   SDS((8, 128), F32)),
        in_specs=[HBM_SPEC] * (2 * n),
        out_specs=(SEM_SPEC, SEM_SPEC, *[HBM_SPEC] * (2 * n), pl.BlockSpec(memory_space=pltpu.VMEM)),
        input_output_aliases={i: 2 + i for i in range(2 * n)},
        compiler_params=pltpu.CompilerParams(has_side_effects=SIDE_EFFECT),
    )(*[pltpu.with_memory_space_constraint(a, pltpu.HBM) for a in srcs],
      *[pltpu.with_memory_space_constraint(d, pltpu.HBM) for d in dsts])
    return (res[0], res[1], list(res[2:2 + n]), list(res[2 + n:2 + 2 * n])), res[-1]


def _chips_wait(handle, after, gather, name):
    send_sems, recv_sems, srcs, dsts = handle
    n = len(srcs)

    def body(*refs):
        for cp in _chip_copies(refs[:n], refs[n:2 * n], refs[2 * n], refs[2 * n + 1], gather):
            cp.wait_send()
            cp.wait_recv()

    res = _pcall(
        body, name=name, out_shape=tuple(pltpu.HBM(a.shape, a.dtype) for a in srcs + dsts),
        in_specs=[HBM_SPEC] * (2 * n) + [SEM_SPEC, SEM_SPEC, ANY_SPEC], out_specs=tuple([HBM_SPEC] * (2 * n)),
        input_output_aliases={i: i for i in range(2 * n)},
        compiler_params=pltpu.CompilerParams(has_side_effects=SIDE_EFFECT),
    )(*srcs, *dsts, send_sems, recv_sems, after)
    return list(res[:n]), list(res[n:])


def _place_own(xs, me, name):
    n = len(xs)
    n_row_blocks = next(nb for nb in (4, 2, 1) if all(x.shape[0] % (nb * PACK_ROW_ALIGN) == 0 for x in xs))

    def body(me_ref, *refs):
        for a in range(n):
            refs[n + a][...] = refs[a][...]

    def rows(x):
        return x.shape[0] // n_row_blocks
    grid_spec = pltpu.PrefetchScalarGridSpec(
        num_scalar_prefetch=1, grid=(n_row_blocks,),
        in_specs=[pl.BlockSpec((rows(x), x.shape[1]), lambda i, me_ref: (i, 0)) for x in xs],
        out_specs=[pl.BlockSpec((None, rows(x), x.shape[1]), lambda i, me_ref: (me_ref[0], i, 0)) for x in xs])
    return _pcall(body, grid_spec=grid_spec, out_shape=[SDS((N_DEV,) + x.shape, x.dtype) for x in xs], name=name,
                  compiler_params=_params("arbitrary"))(me, *xs)


def _forward_to_sibling(gs, name):
    n = len(gs)

    def body(*refs):
        out_refs = refs[n:2 * n]
        send_sems, recv_sems = refs[2 * n:]
        mx, my, mc = _mesh_pos()
        chips = [(1 - mx, my), (mx, 1 - my), (1 - mx, 1 - my)]
        sends = []
        for a in range(n):
            for j, (px, py) in enumerate(chips):
                slot = out_refs[a].at[4 * px + 2 * py + mc]
                sends.append(pltpu.make_async_remote_copy(
                    src_ref=slot, dst_ref=slot, send_sem=send_sems.at[3 * a + j], recv_sem=recv_sems.at[3 * a + j],
                    device_id=(mx, my, 1 - mc), device_id_type=MESH_IDS))
        for cp in sends:
            cp.start()
        for a in range(n):
            for j, (px, py) in enumerate(chips):
                slot = out_refs[a].at[4 * px + 2 * py + (1 - mc)]
                pltpu.make_async_remote_copy(
                    src_ref=slot, dst_ref=slot, send_sem=send_sems.at[3 * a + j], recv_sem=recv_sems.at[3 * a + j],
                    device_id=(mx, my, 1 - mc), device_id_type=MESH_IDS).wait_recv()
        for cp in sends:
            cp.wait_send()

    return _pcall(body, out_shape=[SDS(g.shape, g.dtype) for g in gs], in_specs=[ANY_SPEC] * n, out_specs=[ANY_SPEC] * n,
                  input_output_aliases={a: a for a in range(n)},
                  scratch_shapes=[pltpu.SemaphoreType.DMA((3 * n,)), pltpu.SemaphoreType.DMA((3 * n,))],
                  name=name)(*gs)


def _add_core_blocks(gs, recvs, core, name):
    n = len(gs)

    def body(core_ref, *refs):
        for a in range(n):
            refs[2 * n + a][...] = (refs[a][...].astype(F32) + refs[n + a][...].astype(F32)).astype(refs[2 * n + a].dtype)

    in_specs = [pl.BlockSpec((None, None) + g.shape[1:], lambda k, core_ref: (k, core_ref[0], 0, 0)) for g in gs]
    in_specs += [pl.BlockSpec((None,) + r.shape[1:], lambda k, core_ref: (k, 0, 0)) for r in recvs]
    grid_spec = pltpu.PrefetchScalarGridSpec(
        num_scalar_prefetch=1, grid=(4,), in_specs=in_specs,
        out_specs=[pl.BlockSpec((None,) + r.shape[1:], lambda k, core_ref: (k, 0, 0)) for r in recvs])
    return _pcall(body, grid_spec=grid_spec, out_shape=[SDS(r.shape, r.dtype) for r in recvs], name=name,
                  compiler_params=_params("arbitrary"))(core, *[g.reshape((4, 2) + g.shape[1:]) for g in gs], *recvs)


def _add_chip_blocks(parts, recvs, chip, name):
    n = len(parts)
    n_row_blocks = next(nb for nb in (4, 2, 1) if all(p.shape[1] % (nb * PACK_ROW_ALIGN) == 0 for p in parts))

    def body(chip_ref, *refs):
        for a in range(n):
            r_ref = refs[n + a]
            refs[2 * n + a][...] = ((refs[a][...].astype(F32) + r_ref[0].astype(F32)) + r_ref[1].astype(F32)
                                    ) + r_ref[2].astype(F32)

    def rows(p):
        return p.shape[1] // n_row_blocks
    in_specs = [pl.BlockSpec((None, rows(p), p.shape[2]), lambda i, chip_ref: (chip_ref[0], i, 0)) for p in parts]
    in_specs += [pl.BlockSpec((3, rows(p), p.shape[2]), lambda i, chip_ref: (0, i, 0)) for p in parts]
    grid_spec = pltpu.PrefetchScalarGridSpec(
        num_scalar_prefetch=1, grid=(n_row_blocks,), in_specs=in_specs,
        out_specs=[pl.BlockSpec((rows(p), p.shape[2]), lambda i, chip_ref: (i, 0)) for p in parts])
    return _pcall(body, grid_spec=grid_spec, out_shape=[SDS(p.shape[1:], F32) for p in parts], name=name,
                  compiler_params=_params("arbitrary"))(chip, *parts, *recvs)


class _Exchange:
    def __init__(self, shards):
        self.shards = shards
        self.core = lax.axis_index("c").astype(jnp.int32).reshape(1)
        self.chip = (2 * lax.axis_index("x") + lax.axis_index("y")).astype(jnp.int32).reshape(1)
        self.me = 2 * self.chip + self.core

    def gather_start(self, i):
        xs = [self.shards[n][_layer_of(n, i)] for n in _layer_weights(i)]
        return _chips_start(xs, _place_own(xs, self.me, f"l{i}_gather_own"), True, f"l{i}_gather_start")

    def gather_finish(self, i, handle, after):
        _, gs = _chips_wait(handle, after, True, f"l{i}_gather_wait")
        out = {}
        for n, g in zip(_layer_weights(i), _forward_to_sibling(gs, f"l{i}_gather_forward")):
            axis = SHARD_AXIS[n] - 1
            out[n] = _Sharded(g[:, None], 0, axis) if n in DIRECT else _join_shards(g, axis)
        return out

    def reduce_start(self, tag, names, gs):
        parts = _add_core_blocks(gs, _swap_with_sibling(gs, tag + "_rs_sibling"), self.core, tag + "_rs_add_core")
        return _chips_start(parts, None, False, tag + "_rs_start")

    def reduce_finish(self, tag, handle, after):
        parts, lands = _chips_wait(handle, after, False, tag + "_rs_wait")
        return _add_chip_blocks(parts, lands, self.chip, tag + "_rs_add_chips")


def _layer_weights(i):
    return MIXER_WEIGHTS[i % N_MIXERS] + LAYER_WEIGHTS


def _layer_of(name, i):
    return i if name in LAYER_WEIGHTS else i // N_MIXERS


def _slab_rows(n):
    return -(-n // (PACK_COLS * PACK_ROW_ALIGN)) * PACK_ROW_ALIGN


def _pack(arrays):
    def slab(a):
        flat = a.reshape(-1)
        rows = _slab_rows(flat.shape[0])
        return jnp.pad(flat, (0, rows * PACK_COLS - flat.shape[0])).reshape(rows, PACK_COLS)
    return jnp.concatenate([slab(a) for a in arrays], axis=0)


def _unpack(slab, shapes, lead=()):
    out, row = [], 0
    for shape in shapes:
        n = math.prod(shape)
        rows = _slab_rows(n)
        part = lax.slice_in_dim(slab, row, row + rows, axis=len(lead)).reshape(lead + (rows * PACK_COLS,))
        out.append(lax.slice_in_dim(part, 0, n, axis=len(lead)).reshape(lead + tuple(shape)))
        row += rows
    return out


def _join_shards(t, axis):
    t = jnp.moveaxis(t, 0, axis)
    return t.reshape(t.shape[:axis] + (t.shape[axis] * t.shape[axis + 1],) + t.shape[axis + 2:])


def _split_shards(full, axis):
    s = full.shape
    t = full.reshape(s[:axis] + (N_DEV, s[axis] // N_DEV) + s[axis + 1:])
    return jnp.moveaxis(t, axis, 0)


def _weight_grad(lhs, rhs, name, tag):
    axis = SHARD_AXIS[name] - 1
    if name in DIRECT:
        return _mm(lhs, rhs, "tn", [BF16], name=tag, shard_out=axis)
    return _split_shards(_mm(lhs, rhs, "tn", [F32], name=tag), axis).astype(BF16)


def _attention_fwd(u, h, w, wl, j, tag):
    t, d = u.shape
    n_heads = d // HEAD_DIM
    n_kv = n_heads // Q_PER_KV
    qkv = _mm(u, wl["a_wqkv"], "nn", [F32], name=tag + "_qkv")
    q = qkv[:, :d].reshape(t, n_kv, Q_PER_KV, HEAD_DIM).transpose(1, 2, 0, 3)
    k = qkv[:, d:d + n_kv * HEAD_DIM].reshape(t, n_kv, HEAD_DIM).transpose(1, 0, 2)
    v = qkv[:, d + n_kv * HEAD_DIM:].reshape(t, n_kv, HEAD_DIM).transpose(1, 0, 2)
    gq, gk, sinks = w["a_q_norm_g"][j][None], w["a_k_norm_g"][j][None], w["a_sinks"][j]
    o = _attn_fwd(q, k, v, gq, gk, sinks, tag + "_attn")
    o2 = o.transpose(2, 0, 1, 3).reshape(t, d)
    h1 = _mm(o2, wl["a_wo"], "nn", [F32], name=tag + "_wo", epi=lambda acc, h: acc + h, rows=[h])
    return h1, (q, k, v, o2)


def _attention_bwd(ctx, u, dh1, dh1b, w, wl, j, tag, grads, big, after=None):
    q, k, v, o2 = ctx
    t, d = u.shape
    n_kv = q.shape[0]
    gq, gk, sinks = w["a_q_norm_g"][j][None], w["a_k_norm_g"][j][None], w["a_sinks"][j]
    if after is not None:
        gq = gq + after[:1, :HEAD_DIM]
    big.append(("a_wo", j, _weight_grad(o2, dh1b, "a_wo", tag + "_dwo")))
    do2 = _mm(dh1b, wl["a_wo"], "nt", [F32], name=tag + "_do")
    do = do2.reshape(t, n_kv, Q_PER_KV, HEAD_DIM).transpose(1, 2, 0, 3)
    dq, dkp, dkc, dvp, dvc, dgq, dgk, dsink = _attn_bwd(q, k, v, do, gq, gk, sinks, tag + "_attn_bwd")

    def fold(prev_part, cur_part):
        shifted = jnp.concatenate([prev_part[:, ATTN_BLOCK:], jnp.zeros_like(prev_part[:, :ATTN_BLOCK])], axis=1)
        both = _add(shifted.reshape(n_kv * t, HEAD_DIM), cur_part.reshape(n_kv * t, HEAD_DIM), tag + "_fold")
        return both.reshape(n_kv, t, HEAD_DIM).transpose(1, 0, 2).reshape(t, n_kv * HEAD_DIM)
    dqkv = jnp.concatenate([dq.transpose(2, 0, 1, 3).reshape(t, d), fold(dkp, dkc), fold(dvp, dvc)], axis=1).astype(BF16)
    grads["a_q_norm_g"][j] = dgq[0]
    grads["a_k_norm_g"][j] = dgk[0]
    grads["a_sinks"][j] = dsink[:, :, 0].reshape(-1)
    big.append(("a_wqkv", j, _weight_grad(u, dqkv, "a_wqkv", tag + "_dwqkv")))
    return _mm(dqkv, wl["a_wqkv"], "nt", [F32], name=tag + "_du")


def _conformer_fwd(u, h, w, wl, j, tag):
    t, d = u.shape
    a = _mm(u, wl["b_w_pw1"], "nn", [F32], name=tag + "_pw1", epi=lambda acc, b: acc + b, pars=[w["b_b_pw1"][j][None]])
    y = _rowwise(lambda val, gate: val * _sigmoid(gate), [(a, d, 0), (a, d, 1)], [], [(d, F32)], name=tag + "_glu")[0]
    y2 = _dwconv_fwd(y, w["b_w_dw"][j], w["b_b_dw"][j][None], False, tag + "_dw")[0]

    def ln_silu(y2, g, b):
        mu = _rowmean(y2)
        yc = y2 - mu
        y3 = yc * lax.rsqrt(_rowmean(yc * yc) + EPS) * g + b
        return y3 * _sigmoid(y3)
    y4 = _rowwise(ln_silu, [y2], [w["b_ln_g"][j][None], w["b_ln_b"][j][None]], [(d, BF16)], name=tag + "_ln")[0]
    h1 = _mm(y4, wl["b_w_pw2"], "nn", [F32], name=tag + "_pw2", epi=lambda acc, h, b: acc + h + b, rows=[h],
             pars=[w["b_b_pw2"][j][None]])
    return h1, (a, y, y2, y4)


def _conformer_bwd(ctx, u, dh1, dh1b, dh1_colsum, w, wl, j, tag, grads, big):
    a, y, y2, y4 = ctx
    t, d = u.shape
    big.append(("b_w_pw2", j, _weight_grad(y4, dh1b, "b_w_pw2", tag + "_dwpw2")))
    grads["b_b_pw2"][j] = dh1_colsum[0]
    dy4 = _mm(dh1b, wl["b_w_pw2"], "nt", [F32], name=tag + "_dy4")

    def ln_silu_bwd(dy4, y2, g, b):
        mu = _rowmean(y2)
        yc = y2 - mu
        rstd = lax.rsqrt(_rowmean(yc * yc) + EPS)
        xh = yc * rstd
        dy3 = dy4 * _dsilu(xh * g + b)
        dxh = dy3 * g
        return rstd * (dxh - _rowmean(dxh) - xh * _rowmean(dxh * xh)), _colsum(dy3 * xh), _colsum(dy3)
    dy2, dlg, dlb = _rowwise(ln_silu_bwd, [dy4, y2], [w["b_ln_g"][j][None], w["b_ln_b"][j][None]], [(d, F32)], [d, d],
                             name=tag + "_ln_bwd")
    grads["b_ln_g"][j], grads["b_ln_b"][j] = dlg[0], dlb[0]
    dy, dw_dw, db_dw = _dwconv_bwd(y, dy2, w["b_w_dw"][j], tag + "_dw_bwd")
    grads["b_w_dw"][j], grads["b_b_dw"][j] = dw_dw, db_dw[0]

    def glu_bwd(dy, val, gate):
        s = _sigmoid(gate)
        dval, dgate = dy * s, dy * val * s * (1.0 - s)
        return dval, dgate, _colsum(dval), _colsum(dgate)
    dval, dgate, dbv, dbg = _rowwise(glu_bwd, [dy, (a, d, 0), (a, d, 1)], [], [(d, BF16), (d, BF16)], [d, d],
                                     name=tag + "_glu_bwd")
    da = jnp.concatenate([dval, dgate], axis=1)
    grads["b_b_pw1"][j] = jnp.concatenate([dbv[0], dbg[0]])
    big.append(("b_w_pw1", j, _weight_grad(u, da, "b_w_pw1", tag + "_dwpw1")))
    return _mm(da, wl["b_w_pw1"], "nt", [F32], name=tag + "_du")


def _ssd_layouts(v, n_groups):
    t = v.shape[0]
    vg = v[:, :n_groups * SSM_R].reshape(t, n_groups, SSM_R).transpose(1, 0, 2)
    return vg, vg.reshape(n_groups, t // SSM_L, SSM_L, SSM_R).transpose(0, 1, 3, 2)


def _from_group_layout(vg, width):
    n_groups, t, _ = vg.shape
    v = vg.transpose(1, 0, 2).reshape(t, n_groups * SSM_R)
    return jnp.pad(v, ((0, 0), (0, width - n_groups * SSM_R)))


def _pad_lanes(v, width):
    return jnp.pad(v, [(0, 0)] * (v.ndim - 1) + [(0, width - v.shape[-1])])


def _mamba_weights(wl, d_inner, conv_c):
    w_in = wl["c_w_in"]
    n_heads = w_in.shape[1] - d_inner - conv_c
    return (w_in[:, :d_inner], w_in[:, d_inner:d_inner + conv_c], _pad_lanes(w_in[:, d_inner + conv_c:], 128), n_heads)


def _mamba_fwd(u, h, w, wl, j, tag):
    t, d = u.shape
    d_inner = w["c_D"].shape[1] * SSM_P
    conv_c = w["c_w_conv"][j].shape[1]
    w_z, w_xbc, w_dt, n_heads = _mamba_weights(wl, d_inner, conv_c)
    n_groups = n_heads // SSM_R
    z = _mm(u, w_z, "nn", [F32], name=tag + "_in_z")
    xbc_raw = _mm(u, w_xbc, "nn", [F32], name=tag + "_in_xbc")
    dt_raw = _mm(u, w_dt, "nn", [F32], name=tag + "_in_dt")
    pre, xbc = _dwconv_fwd(xbc_raw, w["c_w_conv"][j], w["c_b_conv"][j][None], True, tag + "_conv")
    bias, a_log = _pad_lanes(w["c_dt_bias"][j][None], 128), _pad_lanes(w["c_A_log"][j][None], 128)
    dt, cs = _dt_fwd(dt_raw, bias, a_log, tag + "_dt")
    dt_g, _ = _ssd_layouts(dt, n_groups)
    cs_g, cs_row = _ssd_layouts(cs, n_groups)
    y, prev = _ssd_fwd(xbc, dt_g, cs_g, cs_row, d_inner, tag + "_ssd")
    d_cols = jnp.repeat(w["c_D"][j], SSM_P)[None]
    g = w["c_norm_g"][j][None]

    def gated_norm(y, xs, z, d_cols, g):
        yg = (y + xs * d_cols) * (z * _sigmoid(z))
        return yg * lax.rsqrt(_rowmean(yg * yg) + EPS) * g
    yn = _rowwise(gated_norm, [y, (xbc, d_inner, 0), z], [d_cols, g], [(d_inner, BF16)], name=tag + "_gnorm")[0]
    h1 = _mm(yn, wl["c_w_out"], "nn", [F32], name=tag + "_out", epi=lambda acc, h: acc + h, rows=[h])
    return h1, (z, xbc_raw, dt_raw, pre, xbc, dt, dt_g, cs_g, cs_row, y, prev, yn)


def _mamba_bwd(ctx, u, dh1, dh1b, w, wl, j, tag, grads, big):
    z, xbc_raw, dt_raw, pre, xbc, dt, dt_g, cs_g, cs_row, y, prev, yn = ctx
    t, d = u.shape
    d_inner = w["c_D"].shape[1] * SSM_P
    conv_c = w["c_w_conv"][j].shape[1]
    w_z, w_xbc, w_dt, n_heads = _mamba_weights(wl, d_inner, conv_c)
    n_groups = n_heads // SSM_R
    big.append(("c_w_out", j, _weight_grad(yn, dh1b, "c_w_out", tag + "_dwout")))
    dyn = _mm(dh1b, wl["c_w_out"], "nt", [F32], name=tag + "_dyn")
    d_cols = jnp.repeat(w["c_D"][j], SSM_P)[None]
    g = w["c_norm_g"][j][None]

    def gated_norm_bwd(dyn, y, xs, z, d_cols, g):
        sg = _sigmoid(z)
        yt = y + xs * d_cols
        yg = yt * (z * sg)
        rstd = lax.rsqrt(_rowmean(yg * yg) + EPS)
        xh = yg * rstd
        dxh = dyn * g
        dyg = rstd * (dxh - xh * _rowmean(dxh * xh))
        dyt = dyg * (z * sg)
        return dyt, dyg * yt * (sg * (1.0 + z * (1.0 - sg))), dyt * d_cols, _colsum(dyn * xh), _colsum(dyt * xs)
    dyt, dz, dxs_skip, dg, dd_cols = _rowwise(
        gated_norm_bwd, [dyn, y, (xbc, d_inner, 0), z], [d_cols, g], [(d_inner, F32), (d_inner, BF16), (d_inner, F32)],
        [d_inner, d_inner], name=tag + "_gnorm_bwd")
    grads["c_norm_g"][j] = dg[0]
    grads["c_D"][j] = dd_cols.reshape(n_heads, SSM_P).sum(axis=1)
    dx, db, dc, ddtx_g, dda_g = _ssd_bwd(xbc, dt_g, cs_g, cs_row, prev, dyt, d_inner, tag + "_ssd_bwd")
    dxs = _add(dx, dxs_skip, tag + "_dxs")
    dpost = jnp.concatenate([dxs, db, dc], axis=1)
    dpre = _rowwise(lambda dpost, pre: dpost * _dsilu(pre), [dpost, pre], [], [(conv_c, F32)], name=tag + "_silu_bwd")[0]
    dxbc, dw_conv, db_conv = _dwconv_bwd(xbc_raw, dpre, w["c_w_conv"][j], tag + "_conv_bwd")
    grads["c_w_conv"][j], grads["c_b_conv"][j] = dw_conv, db_conv[0]
    bias, a_log = _pad_lanes(w["c_dt_bias"][j][None], 128), _pad_lanes(w["c_A_log"][j][None], 128)
    ddt_raw, dbias, da_log = _dt_bwd(_from_group_layout(ddtx_g, 128), _from_group_layout(dda_g, 128), dt_raw, dt, bias,
                                     a_log, tag + "_dt_bwd")
    grads["c_dt_bias"][j], grads["c_A_log"][j] = dbias[0, :n_heads], da_log[0, :n_heads]
    dxbc_b = dxbc.astype(BF16)
    dw_in = jnp.concatenate(
        [_mm(u, dz, "tn", [F32], name=tag + "_dwin_z"), _mm(u, dxbc_b, "tn", [F32], name=tag + "_dwin_xbc"),
         _mm(u, ddt_raw, "tn", [F32], name=tag + "_dwin_dt")[:, :n_heads]], axis=1)
    big.append(("c_w_in", j, _split_shards(dw_in, SHARD_AXIS["c_w_in"] - 1).astype(BF16)))
    du = _mm(dz, w_z, "nt", [F32], name=tag + "_du_z")
    du = _mm(dxbc_b, w_xbc, "nt", [F32], name=tag + "_du_xbc", epi=lambda acc, r: acc + r, rows=[du])
    return _mm(ddt_raw, w_dt, "nt", [F32], name=tag + "_du_dt", epi=lambda acc, r: acc + r, rows=[du])


def _local_step(x, p, tgt, w, comm, first):
    depth = p.shape[0]
    grads = {name: {} for name in REPLICATED + tuple(n for n, _ in SMALL_SHARDED)}
    reduced = {name: {} for name, _ in BIG_SHARDED}
    saved = []
    h = x
    wl = comm.gather_finish(0, first, w[SMALL_SHARDED[0][0]])
    for i in range(depth):
        kind, j, tag = i % N_MIXERS, i // N_MIXERS, f"l{i}"
        token = None
        if i + 1 < depth:
            handle, token = comm.gather_start(i + 1)
        u = _rms_fwd(h, w["mix_norm_g"][i][None], tag + "_rms_mix", after=token)
        fwd = (_attention_fwd, _conformer_fwd, _mamba_fwd)[kind]
        h1, ctx = fwd(u, h, w, wl, j, tag)
        u2 = _rms_fwd(h1, w["mlp_norm_g"][i][None], tag + "_rms_mlp")
        a, r = _mm(u2, wl["m_w1"], "nn", [BF16, BF16], name=tag + "_w1",
                   epi=lambda acc: (acc, jnp.square(jnp.maximum(acc, 0.0))))
        h2 = _mm(r, wl["m_w2"], "nn", [F32], name=tag + "_w2", epi=lambda acc, h: acc + h, rows=[h1])
        u3 = _rms_fwd(h2, w["ple_norm_g"][i][None], tag + "_rms_ple")
        gl = _mm(u3, wl["ple_w_gate"], "nn", [F32], name=tag + "_gate")
        h3, pp = _mm(p[i], wl["ple_w_proj"], "nn", [F32, BF16], name=tag + "_proj",
                     epi=lambda acc, h, gl: (h + _sigmoid(gl) * acc, acc), rows=[h2, gl])
        saved.append((h, u, ctx, h1, u2, a, r, h2, u3, gl, pp, wl))
        h = h3
        if i + 1 < depth:
            wl = comm.gather_finish(i + 1, handle, h)

    dh, sq_err = _loss_and_grad(h, tgt, "loss")
    pending, token = None, None
    for i in reversed(range(depth)):
        kind, j, tag = i % N_MIXERS, i // N_MIXERS, f"l{i}"
        h0, u, ctx, h1, u2, a, r, h2, u3, gl, pp, wl = saved[i]
        big = []
        dgl, dpp = _ple_bwd(dh, gl, pp, tag + "_ple_bwd", after=token)
        big.append(("ple_w_proj", i, _weight_grad(p[i], dpp, "ple_w_proj", tag + "_dwproj")))
        big.append(("ple_w_gate", i, _weight_grad(u3, dgl, "ple_w_gate", tag + "_dwgate")))
        du3 = _mm(dgl, wl["ple_w_gate"], "nt", [F32], name=tag + "_du3")
        dh2, dh2b, dg, _ = _rms_bwd(du3, h2, w["ple_norm_g"][i][None], dh, tag + "_rms_ple_bwd")
        grads["ple_norm_g"][i] = dg[0]
        big.append(("m_w2", i, _weight_grad(r, dh2b, "m_w2", tag + "_dw2")))
        da = _mm(dh2b, wl["m_w2"], "nt", [BF16], name=tag + "_da",
                 epi=lambda acc, a: acc * (2.0 * jnp.maximum(a.astype(F32), 0.0)), rows=[a])
        big.append(("m_w1", i, _weight_grad(u2, da, "m_w1", tag + "_dw1")))
        du2 = _mm(da, wl["m_w1"], "nt", [F32], name=tag + "_du2")
        dh1, dh1b, dg, dh1_colsum = _rms_bwd(du2, h1, w["mlp_norm_g"][i][None], dh2, tag + "_rms_mlp_bwd")
        grads["mlp_norm_g"][i] = dg[0]
        early_token = None
        if i == 0:
            handle, early_token = comm.reduce_start(tag + "m", [name for name, _, _ in big], [g for _, _, g in big])
            pending_early = (tag + "m", handle, [(name, layer) for name, layer, _ in big])
            big = []
        if kind == 0:
            du = _attention_bwd(ctx, u, dh1, dh1b, w, wl, j, tag, grads, big, early_token)
        elif kind == 1:
            du = _conformer_bwd(ctx, u, dh1, dh1b, dh1_colsum, w, wl, j, tag, grads, big)
        else:
            du = _mamba_bwd(ctx, u, dh1, dh1b, w, wl, j, tag, grads, big)
        dh, _, dg, _ = _rms_bwd(du, h0, w["mix_norm_g"][i][None], dh1, tag + "_rms_mix_bwd")
        grads["mix_norm_g"][i] = dg[0]
        if pending is not None:
            _finish_reduce(comm, pending, dh, reduced)
        handle, token = comm.reduce_start(tag, [name for name, _, _ in big], [g for _, _, g in big])
        pending = (tag, handle, [(name, layer) for name, layer, _ in big])

    def stack(d):
        return {name: jnp.stack([g[k] for k in sorted(g)]) for name, g in d.items()}
    return sq_err, dh, reduced, stack(grads), [pending_early, pending]


def _finish_reduce(comm, pending, after, reduced):
    tag, handle, keys = pending
    for (name, layer), g in zip(keys, comm.reduce_finish(tag, handle, after)):
        reduced[name][layer] = g


def kernel(x, p, mix_norm_g, mlp_norm_g, ple_norm_g, a_wqkv, a_q_norm_g, a_k_norm_g, a_sinks, a_wo, b_w_pw1, b_b_pw1, b_w_dw, b_b_dw, b_ln_g, b_ln_b, b_w_pw2, b_b_pw2, c_w_in, c_w_conv, c_b_conv, c_dt_bias, c_A_log, c_D, c_norm_g, c_w_out, m_w1, m_w2, ple_w_proj, ple_w_gate, loss_target, m_mix_norm_g, m_mlp_norm_g, m_ple_norm_g, m_a_wqkv, m_a_q_norm_g, m_a_k_norm_g, m_a_sinks, m_a_wo, m_b_w_pw1, m_b_b_pw1, m_b_w_dw, m_b_b_dw, m_b_ln_g, m_b_ln_b, m_b_w_pw2, m_b_b_pw2, m_c_w_in, m_c_w_conv, m_c_b_conv, m_c_dt_bias, m_c_A_log, m_c_D, m_c_norm_g, m_c_w_out, m_m_w1, m_m_w2, m_ple_w_proj, m_ple_w_gate, v_mix_norm_g, v_mlp_norm_g, v_ple_norm_g, v_a_wqkv, v_a_q_norm_g, v_a_k_norm_g, v_a_sinks, v_a_wo, v_b_w_pw1, v_b_b_pw1, v_b_w_dw, v_b_b_dw, v_b_ln_g, v_b_ln_b, v_b_w_pw2, v_b_b_pw2, v_c_w_in, v_c_w_conv, v_c_b_conv, v_c_dt_bias, v_c_A_log, v_c_D, v_c_norm_g, v_c_w_out, v_m_w1, v_m_w2, v_ple_w_proj, v_ple_w_gate):
    args = dict(locals())
    wl = {n: args[n] for n in WEIGHTS}
    ml = {n: args["m_" + n] for n in WEIGHTS}
    vl = {n: args["v_" + n] for n in WEIGHTS}
    d_model = x.shape[-1]
    small_sharded = [n for n, _ in SMALL_SHARDED]

    comm = _Exchange({n: wl[n].astype(BF16) for n, _ in BIG_SHARDED})
    first, token = comm.gather_start(0)
    whole = {n: wl[n] for n in REPLICATED}
    small = _all_gather([_pack([wl[n] for n in small_sharded]) + token[0, 0]], "gather_small")[0]
    for (n, axis), t in zip(SMALL_SHARDED, _unpack(small, [wl[n].shape for n in small_sharded], (N_DEV,))):
        whole[n] = _join_shards(t, axis)
    sq_err, grad_x, reduced, g_small, pending = _local_step(x[0], p[:, 0], loss_target[0], whole, comm, first)
    loss = lax.psum(0.5 * jnp.sum(sq_err) / d_model, ("x", "y", "c"))

    small_names = list(REPLICATED) + small_sharded
    summed = _sum_blocks(_all_gather([_pack([g_small[n] for n in small_names])], "gather_small_grads")[0],
                         "sum_small_grads")
    for in_flight in pending:
        _finish_reduce(comm, in_flight, summed, reduced)
    g_big = {n: jnp.stack([g[k] for k in sorted(g)]) for n, g in reduced.items()}
    g_small = dict(zip(small_names, _unpack(summed, [g_small[n].shape for n in small_names])))
    me = 4 * lax.axis_index("x") + 2 * lax.axis_index("y") + lax.axis_index("c")
    for n, axis in SMALL_SHARDED:
        size = wl[n].shape[axis]
        g_small[n] = lax.dynamic_slice_in_dim(g_small[n], me * size, size, axis)

    out = {}
    for n, _ in BIG_SHARDED:
        shape = wl[n].shape
        two_d = (shape[0] * shape[1], shape[2])
        delta, new_m, new_v = _adamw(wl[n].reshape(two_d), g_big[n].reshape(two_d), ml[n].reshape(two_d),
                                     vl[n].reshape(two_d), "adamw_" + n)
        out.update({"grad_" + n: g_big[n], "delta_" + n: delta.reshape(shape), "new_m_" + n: new_m.reshape(shape),
                    "new_v_" + n: new_v.reshape(shape)})
    g_slab = _pack([g_small[n] for n in small_names])
    delta, new_m, new_v = _adamw(_pack([wl[n] for n in small_names]), g_slab, _pack([ml[n] for n in small_names]),
                                 _pack([vl[n] for n in small_names]), "adamw_small")
    for kind, slab in (("delta", delta), ("new_m", new_m), ("new_v", new_v)):
        out.update({kind + "_" + n: a for n, a in zip(small_names, _unpack(slab, [wl[n].shape for n in small_names]))})
    out.update({"grad_" + n: g_small[n] for n in small_names})
    return (loss, grad_x[None], *[out[k + "_" + n] for k in ("grad", "delta", "new_m", "new_v") for n in WEIGHTS])
```
